```python
import math
import jax, jax.numpy as jnp
from jax import lax
import numpy as np

D_MODEL = 2048
BATCH = 2
SEQ = 4096
DEPTH = 1

MIX_WIDTH = D_MODEL
SSM_WIDTH = MIX_WIDTH // 2
ATTN_WIDTH = MIX_WIDTH - SSM_WIDTH
SSM_GROUP = 16
SSM_GROUPS = SSM_WIDTH // SSM_GROUP
SSM_STATE = 64
HEAD_DIM = 64
N_HEADS = ATTN_WIDTH // HEAD_DIM
Q_BLOCK = 128
D_FF = ((8 * D_MODEL // 3 + 255) // 256) * 256
N_MOD = 6
EPS = 1e-6
DT_MIN = 1e-3
DT_MAX = 1e-1
IN_COLS = SSM_WIDTH + 3 * ATTN_WIDTH

kernel_name = "hybrid_s5_stickbreaking_adaln_block"


def rmsnorm(x, gain):
    xf = x.astype(jnp.float32)
    xf = xf * lax.rsqrt(jnp.mean(xf * xf, axis=-1, keepdims=True) + EPS)
    return (xf * gain.astype(jnp.float32)).astype(x.dtype)


def s5_mixer(u, a_re, a_im, log_dt, b_re, b_im, c_re, c_im, d_skip, w_glu):
    bsz, seqlen, _ = u.shape
    f32 = jnp.float32
    uf = u.astype(f32).reshape(bsz, seqlen, SSM_GROUPS, SSM_GROUP)
    lam = lax.complex(a_re.astype(f32), a_im.astype(f32))
    dt = jnp.exp(log_dt.astype(f32))[:, None]
    lam_bar = jnp.exp(lam * dt)
    b = lax.complex(b_re.astype(f32), b_im.astype(f32))
    b_bar = ((lam_bar - 1.0) / lam)[..., None] * b
    bu = jnp.einsum('blgh,gph->blgp', uf.astype(jnp.complex64), b_bar)
    a = jnp.broadcast_to(lam_bar, bu.shape)

    def combine(left, right):
        a_l, b_l = left
        a_r, b_r = right
        return a_r * a_l, a_r * b_l + b_r

    _, states = lax.associative_scan(combine, (a, bu), axis=1)
    c = lax.complex(c_re.astype(f32), c_im.astype(f32))
    y = jnp.einsum('ghp,blgp->blgh', c, states).real + d_skip.astype(f32) * uf
    y = jax.nn.gelu(y.reshape(bsz, seqlen, SSM_WIDTH))
    y = y * jax.nn.sigmoid(y @ w_glu.astype(f32))
    return y.astype(u.dtype)


def stick_breaking_attention(q, k, v):
    bsz, seqlen, nh, dh = q.shape
    nblk = seqlen // Q_BLOCK
    f32 = jnp.float32
    qf = q.astype(f32) * (1.0 / math.sqrt(dh))
    kf = k.astype(f32)
    vf = v.astype(f32)
    qb = qf.reshape(bsz, nblk, Q_BLOCK, nh, dh).transpose(1, 0, 3, 2, 4)
    key_pos = jnp.arange(seqlen)

    def one_block(args):
        q_blk, blk_idx = args
        q_pos = blk_idx * Q_BLOCK + jnp.arange(Q_BLOCK)
        z = jnp.einsum('bhqd,bkhd->bhqk', q_blk, kf)
        past = key_pos[None, :] < q_pos[:, None]
        log_keep = jnp.where(past, jax.nn.log_sigmoid(-z), 0.0)
        after = lax.cumsum(log_keep, axis=3, reverse=True) - log_keep
        log_w = jax.nn.log_sigmoid(z) + after
        w = jnp.where(past, jnp.exp(log_w), 0.0)
        return jnp.einsum('bhqk,bkhd->bhqd', w, vf)

    out = lax.map(one_block, (qb, jnp.arange(nblk)))
    out = out.transpose(1, 0, 3, 2, 4).reshape(bsz, seqlen, nh * dh)
    return out.astype(q.dtype)


def setup_inputs(seed: int = 0) -> dict:
    key = jax.random.key(seed)
    ks = jax.random.split(key, 24)
    f32 = jnp.float32
    nrm = lambda k, shape, s: (jax.random.normal(k, shape, f32) * s)
    x = jax.random.normal(ks[0], (BATCH, SEQ, D_MODEL), f32)
    c = jax.random.normal(ks[1], (BATCH, D_MODEL), f32)
    w_ada = nrm(ks[2], (DEPTH, D_MODEL, N_MOD * D_MODEL), D_MODEL ** -0.5)
    b_ada = nrm(ks[3], (DEPTH, N_MOD * D_MODEL), 0.01)
    g_mix = 1.0 + nrm(ks[4], (DEPTH, D_MODEL), 0.02)
    w_in = nrm(ks[5], (DEPTH, D_MODEL, IN_COLS), D_MODEL ** -0.5)
    a_re = -0.5 + nrm(ks[6], (DEPTH, SSM_GROUPS, SSM_STATE), 0.01)
    a_im = (math.pi * jnp.arange(SSM_STATE, dtype=f32))[None, None, :] + nrm(ks[7], (DEPTH, SSM_GROUPS, SSM_STATE), 0.01)
    log_dt = jax.random.uniform(ks[8], (DEPTH, SSM_GROUPS), f32, math.log(DT_MIN), math.log(DT_MAX))
    b_re = nrm(ks[9], (DEPTH, SSM_GROUPS, SSM_STATE, SSM_GROUP), (2 * SSM_GROUP) ** -0.5)
    b_im = nrm(ks[10], (DEPTH, SSM_GROUPS, SSM_STATE, SSM_GROUP), (2 * SSM_GROUP) ** -0.5)
    c_re = nrm(ks[11], (DEPTH, SSM_GROUPS, SSM_GROUP, SSM_STATE), SSM_STATE ** -0.5)
    c_im = nrm(ks[12], (DEPTH, SSM_GROUPS, SSM_GROUP, SSM_STATE), SSM_STATE ** -0.5)
    d_skip = nrm(ks[13], (DEPTH, SSM_GROUPS, SSM_GROUP), 1.0)
    w_glu = nrm(ks[14], (DEPTH, SSM_WIDTH, SSM_WIDTH), SSM_WIDTH ** -0.5)
    q_gain = 1.0 + nrm(ks[15], (DEPTH, HEAD_DIM), 0.02)
    k_gain = 1.0 + nrm(ks[16], (DEPTH, HEAD_DIM), 0.02)
    g_ssm_out = 1.0 + nrm(ks[17], (DEPTH, SSM_WIDTH), 0.02)
    g_attn_out = 1.0 + nrm(ks[18], (DEPTH, ATTN_WIDTH), 0.02)
    w_out = nrm(ks[19], (DEPTH, MIX_WIDTH, D_MODEL), MIX_WIDTH ** -0.5)
    g_ffn = 1.0 + nrm(ks[20], (DEPTH, D_MODEL), 0.02)
    w_gate = nrm(ks[21], (DEPTH, D_MODEL, D_FF), D_MODEL ** -0.5)
    w_up = nrm(ks[22], (DEPTH, D_MODEL, D_FF), D_MODEL ** -0.5)
    w_down = nrm(ks[23], (DEPTH, D_FF, D_MODEL), D_FF ** -0.5)
    return {"x": x, "c": c, "w_ada": w_ada, "b_ada": b_ada, "g_mix": g_mix, "w_in": w_in,
            "a_re": a_re, "a_im": a_im, "log_dt": log_dt, "b_re": b_re, "b_im": b_im,
            "c_re": c_re, "c_im": c_im, "d_skip": d_skip, "w_glu": w_glu,
            "q_gain": q_gain, "k_gain": k_gain, "g_ssm_out": g_ssm_out, "g_attn_out": g_attn_out,
            "w_out": w_out, "g_ffn": g_ffn, "w_gate": w_gate, "w_up": w_up, "w_down": w_down}


def reference(x, c, w_ada, b_ada, g_mix, w_in, a_re, a_im, log_dt, b_re, b_im, c_re, c_im,
              d_skip, w_glu, q_gain, k_gain, g_ssm_out, g_attn_out, w_out, g_ffn, w_gate,
              w_up, w_down):
    bsz, seqlen, _ = x.shape
    h = x
    cond = jax.nn.silu(c.astype(jnp.float32))
    for layer in range(DEPTH):
        mod = (cond @ w_ada[layer].astype(jnp.float32) + b_ada[layer].astype(jnp.float32)).astype(x.dtype)
        shift_m, scale_m, gate_m, shift_f, scale_f, gate_f = jnp.split(mod[:, None, :], N_MOD, axis=-1)

        xm = rmsnorm(h, g_mix[layer]) * (1.0 + scale_m) + shift_m
        proj = xm @ w_in[layer]
        u, q, k, v = jnp.split(proj, [SSM_WIDTH, SSM_WIDTH + ATTN_WIDTH, SSM_WIDTH + 2 * ATTN_WIDTH], axis=-1)
        y_ssm = s5_mixer(u, a_re[layer], a_im[layer], log_dt[layer], b_re[layer], b_im[layer],
                         c_re[layer], c_im[layer], d_skip[layer], w_glu[layer])
        q = rmsnorm(q.reshape(bsz, seqlen, N_HEADS, HEAD_DIM), q_gain[layer])
        k = rmsnorm(k.reshape(bsz, seqlen, N_HEADS, HEAD_DIM), k_gain[layer])
        v = v.reshape(bsz, seqlen, N_HEADS, HEAD_DIM)
        y_attn = stick_breaking_attention(q, k, v)
        mixed = jnp.concatenate([rmsnorm(y_ssm, g_ssm_out[layer]),
                                 rmsnorm(y_attn, g_attn_out[layer])], axis=-1)
        h = h + gate_m * (mixed @ w_out[layer])

        xf = rmsnorm(h, g_ffn[layer]) * (1.0 + scale_f) + shift_f
        ffn = (jax.nn.silu(xf @ w_gate[layer]) * (xf @ w_up[layer])) @ w_down[layer]
        h = h + gate_f * ffn
    return h
```

```python
import functools

import jax
import jax.numpy as jnp
from jax import lax
from jax.experimental import pallas as pl
from jax.experimental.pallas import tpu as pltpu

F32 = jnp.float32
BF16 = jnp.bfloat16
EPS = 1e-6
HEAD_DIM = 64
CHUNK = 16
LANE_TILE = 256
VMEM_LIMIT = 56 * 1024 * 1024


def _cparams(n_axes, vmem=VMEM_LIMIT):
    return pltpu.CompilerParams(dimension_semantics=("arbitrary",) * n_axes,
                                vmem_limit_bytes=vmem)


def _rms(x):
    return x * lax.rsqrt(jnp.mean(x * x, axis=-1, keepdims=True) + EPS)


def _ada_kernel(c_ref, w_ref, b_ref, o_ref):
    d, tn = w_ref.shape
    rows = 256
    for b in range(c_ref.shape[0]):
        acc = jnp.zeros((8, tn), F32)
        for kc in range(d // rows):
            cond = jax.nn.silu(c_ref[b, kc * rows:(kc + 1) * rows, :])
            blk = cond * w_ref[kc * rows:(kc + 1) * rows, :]
            acc = acc + jnp.sum(blk.reshape(rows // 8, 8, tn), axis=0)
        o_ref[b:b + 1, :] = jnp.sum(acc, axis=0, keepdims=True) + b_ref[...]


def ada_mod(c, w_ada, b_ada, tn=1024):
    bsz, d = c.shape
    n = w_ada.shape[1]
    return pl.pallas_call(
        _ada_kernel,
        grid=(n // tn,),
        in_specs=[pl.BlockSpec((bsz, d, 1), lambda j: (0, 0, 0)),
                  pl.BlockSpec((d, tn), lambda j: (0, j)),
                  pl.BlockSpec((1, tn), lambda j: (0, j))],
        out_specs=pl.BlockSpec((bsz, tn), lambda j: (0, j)),
        out_shape=jax.ShapeDtypeStruct((bsz, n), F32),
        compiler_params=_cparams(1),
        name="ada",
    )(c.reshape(bsz, d, 1), w_ada, b_ada.reshape(1, n))


def _head_rmsnorm(r, ones_ref):
    outs = []
    for s in range(r.shape[1] // LANE_TILE):
        rs = r[:, s * LANE_TILE:(s + 1) * LANE_TILE]
        ss = jnp.dot((rs * rs).astype(BF16), ones_ref[...], preferred_element_type=F32)
        outs.append(rs * lax.rsqrt(ss * (1.0 / HEAD_DIM) + EPS))
    return jnp.concatenate(outs, axis=1)


def _proj_kernel(x_ref, mod_ref, g_ref, w_ref, qg_ref, kg_ref, ones_ref, o_ref, xm_ref):
    j = pl.program_id(1)

    @pl.when(j == 0)
    def _():
        xn = _rms(x_ref[...]) * g_ref[...]
        xm_ref[...] = (xn * (1.0 + mod_ref[0, 1:2, :]) + mod_ref[0, 0:1, :]).astype(BF16)

    r = jnp.dot(xm_ref[...], w_ref[...], preferred_element_type=F32)

    @pl.when(j == 1)
    def _():
        o_ref[...] = ((_head_rmsnorm(r, ones_ref) * qg_ref[...]) * (HEAD_DIM ** -0.5)).astype(BF16)

    @pl.when(j == 2)
    def _():
        o_ref[...] = (_head_rmsnorm(r, ones_ref) * kg_ref[...]).astype(BF16)

    @pl.when((j == 0) | (j == 3))
    def _():
        o_ref[...] = r.astype(BF16)


def in_proj(x2, mod3, g_mix, w_in_bf, q_gain, k_gain, seqlen, tm=512):
    t, d = x2.shape
    n = w_in_bf.shape[1]
    tn = n // 4
    reps = tn // HEAD_DIM
    qg = jnp.tile(q_gain.reshape(1, HEAD_DIM), (1, reps))
    kg = jnp.tile(k_gain.reshape(1, HEAD_DIM), (1, reps))
    idx = jnp.arange(LANE_TILE) // HEAD_DIM
    ones = (idx[:, None] == idx[None, :]).astype(BF16)
    return pl.pallas_call(
        _proj_kernel,
        grid=(t // tm, 4),
        in_specs=[pl.BlockSpec((tm, d), lambda i, j: (i, 0)),
                  pl.BlockSpec((1, 6, d), lambda i, j: ((i * tm) // seqlen, 0, 0)),
                  pl.BlockSpec((1, d), lambda i, j: (0, 0)),
                  pl.BlockSpec((d, tn), lambda i, j: (0, j)),
                  pl.BlockSpec((1, tn), lambda i, j: (0, 0)),
                  pl.BlockSpec((1, tn), lambda i, j: (0, 0)),
                  pl.BlockSpec((LANE_TILE, LANE_TILE), lambda i, j: (0, 0))],
        out_specs=pl.BlockSpec((tm, tn), lambda i, j: (i, j)),
        out_shape=jax.ShapeDtypeStruct((t, n), BF16),
        scratch_shapes=[pltpu.VMEM((tm, d), BF16)],
        compiler_params=_cparams(2),
        name="proj",
    )(x2, mod3, g_mix.reshape(1, d), w_in_bf, qg, kg, ones)


def _s5prep_kernel(ar_ref, ai_ref, ldt_ref, btr_ref, bti_ref, cr_ref, ci_ref, d_ref,
                   blr_ref, bli_ref, clr_ref, cli_ref, kx_ref, a_ref):
    ar, ai = ar_ref[...], ai_ref[...]
    dt = jnp.exp(ldt_ref[...])
    mag = jnp.exp(ar * dt)
    th = ai * dt
    lbr, lbi = mag * jnp.cos(th), mag * jnp.sin(th)
    den = ar * ar + ai * ai
    nr, ni = lbr - 1.0, lbi
    kr = (nr * ar + ni * ai) / den
    ki = (ni * ar - nr * ai) / den
    btr, bti = btr_ref[...], bti_ref[...]
    bbr = kr * btr - ki * bti
    bbi = kr * bti + ki * btr
    c_re, c_im = cr_ref[...], ci_ref[...]
    h = c_re.shape[1]
    eye = (lax.broadcasted_iota(jnp.int32, (1, h, h), 1) ==
           lax.broadcasted_iota(jnp.int32, (1, h, h), 2))
    dn = (((2,), (2,)), ((0,), (0,)))
    pr, pi = jnp.ones_like(ar), jnp.zeros_like(ar)
    for j in range(CHUNK):
        blr = pr * bbr - pi * bbi
        bli = pr * bbi + pi * bbr
        blr_ref[j] = blr
        bli_ref[j] = bli
        kx = (lax.dot_general(blr, c_re, dn, precision=lax.Precision.HIGHEST,
                              preferred_element_type=F32)
              - lax.dot_general(bli, c_im, dn, precision=lax.Precision.HIGHEST,
                                preferred_element_type=F32))
        if j == 0:
            kx = kx + jnp.where(eye, d_ref[...], 0.0)
        kx_ref[j] = kx
        pr, pi = pr * lbr - pi * lbi, pr * lbi + pi * lbr
        clr_ref[j] = c_re * pr - c_im * pi
        cli_ref[j] = -(c_re * pi + c_im * pr)
    a_ref[0] = pr
    a_ref[1] = pi


def s5_prep(a_re, a_im, log_dt, b_re, b_im, c_re, c_im, d_skip, gb=8):
    g, p = a_re.shape
    h = c_re.shape[1]
    vec = pl.BlockSpec((gb, 1, p), lambda i: (i, 0, 0))
    mat = pl.BlockSpec((gb, h, p), lambda i: (i, 0, 0))
    lag = pl.BlockSpec((CHUNK, gb, h, p), lambda i: (0, i, 0, 0))
    lag_shape = jax.ShapeDtypeStruct((CHUNK, g, h, p), F32)
    return pl.pallas_call(
        _s5prep_kernel,
        grid=(g // gb,),
        in_specs=[vec, vec, pl.BlockSpec((gb, 1, 1), lambda i: (i, 0, 0)),
                  mat, mat, mat, mat, pl.BlockSpec((gb, 1, h), lambda i: (i, 0, 0))],
        out_specs=[lag, lag, lag, lag,
                   pl.BlockSpec((CHUNK, gb, h, h), lambda i: (0, i, 0, 0)),
                   pl.BlockSpec((2, gb, 1, p), lambda i: (0, i, 0, 0))],
        out_shape=[lag_shape, lag_shape, lag_shape, lag_shape,
                   jax.ShapeDtypeStruct((CHUNK, g, h, h), F32),
                   jax.ShapeDtypeStruct((2, g, 1, p), F32)],
        compiler_params=_cparams(1),
        name="s5prep",
    )(a_re.reshape(g, 1, p), a_im.reshape(g, 1, p), log_dt.reshape(g, 1, 1),
      b_re.transpose(0, 2, 1), b_im.transpose(0, 2, 1), c_re, c_im, d_skip.reshape(g, 1, h))


def s5_assemble(blr, bli, clr, cli, kx):
    n_lag, g, h, p = blr.shape
    gpt = LANE_TILE // h
    gps = LANE_TILE // p
    ngt = g // gpt
    nsub = gpt // gps
    k5 = kx.reshape(n_lag, ngt, gpt, h, h)
    eye_g = jnp.eye(gpt, dtype=bool)[None, None, :, None, :, None]
    tt = jnp.where(eye_g, k5[:, :, :, :, None, :], 0.0).reshape(n_lag, ngt, LANE_TILE, LANE_TILE)
    gi = jnp.arange(gpt)
    m_in = (gi[None, :, None] == (gps * jnp.arange(nsub)[:, None, None] + jnp.arange(gps)[None, None, :]))
    bl = jnp.stack([blr, bli], axis=0).reshape(2, n_lag, ngt, gpt, h, p).transpose(1, 2, 0, 3, 4, 5)
    wb = jnp.where(m_in[None, None, None, :, :, None, :, None],
                   bl[:, :, :, None, :, :, None, :], 0.0)
    wb = wb.reshape(n_lag, ngt, 2, nsub, LANE_TILE, LANE_TILE)
    cl = jnp.stack([clr, cli], axis=0).reshape(2, n_lag, ngt, nsub, gps, h, p)
    cl = cl.transpose(1, 2, 0, 3, 4, 6, 5)
    m_out = (gi[None, None, :] == (gps * jnp.arange(nsub)[:, None, None] + jnp.arange(gps)[None, :, None]))
    wc = jnp.where(m_out[None, None, None, :, :, None, :, None],
                   cl[:, :, :, :, :, :, None, :], 0.0)
    wc = wc.reshape(n_lag, ngt, 2, nsub, LANE_TILE, LANE_TILE)
    return tt.astype(BF16), wb.astype(BF16), wc.astype(BF16)


def _s5in_kernel(u_ref, w_ref, o_ref):
    nsub = w_ref.shape[3]
    for sub in range(nsub):
        acc = None
        for s in range(CHUNK):
            d = jnp.dot(u_ref[0, s, 0], w_ref[CHUNK - 1 - s, 0, 0, sub], preferred_element_type=F32)
            acc = d if acc is None else acc + d
        o_ref[0, 0, :, sub * LANE_TILE:(sub + 1) * LANE_TILE] = acc


def s5_chunk_inputs(u_tm, wb):
    bsz, _, ngt, nc, _ = u_tm.shape
    nsub = wb.shape[3]
    wid = nsub * LANE_TILE
    return pl.pallas_call(
        _s5in_kernel,
        grid=(ngt, 2, bsz),
        in_specs=[pl.BlockSpec((1, CHUNK, 1, nc, LANE_TILE), lambda gt, ri, b: (b, 0, gt, 0, 0)),
                  pl.BlockSpec((CHUNK, 1, 1, nsub, LANE_TILE, LANE_TILE),
                               lambda gt, ri, b: (0, gt, ri, 0, 0, 0))],
        out_specs=pl.BlockSpec((1, 1, nc, wid), lambda gt, ri, b: (b, ri, 0, gt)),
        out_shape=jax.ShapeDtypeStruct((bsz, 2, nc, ngt * wid), F32),
        compiler_params=_cparams(3),
        name="s5in",
    )(u_tm, wb)


def _s5scan_kernel(s_ref, a_ref, x_ref):
    nc, w = s_ref.shape[2], s_ref.shape[3]
    a1r, a1i = a_ref[0], a_ref[1]

    def cmul(xr, xi, yr, yi):
        return xr * yr - xi * yi, xr * yi + xi * yr

    row = lax.broadcasted_iota(jnp.int32, (8, w), 0)
    pows = [(a1r, a1i)]
    for _ in range(7):
        pows.append(cmul(pows[-1][0], pows[-1][1], a1r, a1i))
    steps = []
    for k in (1, 2, 4):
        steps.append((k, jnp.where(row >= k, pows[k - 1][0], 0.0), jnp.where(row >= k, pows[k - 1][1], 0.0)))
    pcr = jnp.zeros((8, w), F32)
    pci = jnp.zeros((8, w), F32)
    for r in range(8):
        pcr = jnp.where(row == r, pows[r][0], pcr)
        pci = jnp.where(row == r, pows[r][1], pci)

    def body(blk, carry):
        cr, ci = carry
        off = pl.multiple_of(blk * 8, 8)
        xr = s_ref[0, 0, pl.ds(off, 8), :]
        xi = s_ref[0, 1, pl.ds(off, 8), :]
        for k, kr, ki in steps:
            sr, si = pltpu.roll(xr, k, 0), pltpu.roll(xi, k, 0)
            dr, di = cmul(sr, si, kr, ki)
            xr, xi = xr + dr, xi + di
        dr, di = cmul(jnp.broadcast_to(cr, (8, w)), jnp.broadcast_to(ci, (8, w)), pcr, pci)
        xr, xi = xr + dr, xi + di
        x_ref[0, 0, pl.ds(off, 8), :] = jnp.where(row == 0, cr, pltpu.roll(xr, 1, 0))
        x_ref[0, 1, pl.ds(off, 8), :] = jnp.where(row == 0, ci, pltpu.roll(xi, 1, 0))
        return xr[7:8, :], xi[7:8, :]

    zero = jnp.zeros((1, w), F32)
    lax.fori_loop(0, nc // 8, body, (zero, zero))


def s5_chunk_scan(s, a, w=1024):
    bsz, _, nc, tot = s.shape
    blk = pl.BlockSpec((1, 2, nc, w), lambda b, l: (b, 0, 0, l))
    return pl.pallas_call(
        _s5scan_kernel,
        grid=(bsz, tot // w),
        in_specs=[blk, pl.BlockSpec((2, 1, w), lambda b, l: (0, 0, l))],
        out_specs=blk,
        out_shape=jax.ShapeDtypeStruct(s.shape, F32),
        compiler_params=_cparams(2),
        name="s5scan",
    )(s, a)


def _s5out_kernel(u_ref, tt_ref, wc_ref, x_ref, o_ref):
    t = pl.program_id(2)
    nc = u_ref.shape[3]
    nsub = wc_ref.shape[3]

    def body(s, acc):
        return acc + jnp.dot(u_ref[0, s, 0], tt_ref[t - s, 0], preferred_element_type=F32)

    acc = lax.fori_loop(0, t + 1, body, jnp.zeros((nc, LANE_TILE), F32))
    for ri in range(2):
        for sub in range(nsub):
            xs = x_ref[0, ri, :, sub * LANE_TILE:(sub + 1) * LANE_TILE].astype(BF16)
            acc = acc + jnp.dot(xs, wc_ref[0, 0, ri, sub], preferred_element_type=F32)
    o_ref[0, 0, 0] = acc.astype(BF16)


def s5_outputs(u_tm, tt, wc, x):
    bsz, _, ngt, nc, _ = u_tm.shape
    nsub = wc.shape[3]
    wid = nsub * LANE_TILE
    return pl.pallas_call(
        _s5out_kernel,
        grid=(bsz, ngt, CHUNK),
        in_specs=[pl.BlockSpec((1, CHUNK, 1, nc, LANE_TILE), lambda b, gt, t: (b, 0, gt, 0, 0)),
                  pl.BlockSpec((CHUNK, 1, LANE_TILE, LANE_TILE), lambda b, gt, t: (0, gt, 0, 0)),
                  pl.BlockSpec((1, 1, 2, nsub, LANE_TILE, LANE_TILE), lambda b, gt, t: (t, gt, 0, 0, 0, 0)),
                  pl.BlockSpec((1, 2, nc, wid), lambda b, gt, t: (b, 0, 0, gt))],
        out_specs=pl.BlockSpec((1, 1, 1, nc, LANE_TILE), lambda b, gt, t: (b, t, gt, 0, 0)),
        out_shape=jax.ShapeDtypeStruct(u_tm.shape, BF16),
        compiler_params=_cparams(3),
        name="s5out",
    )(u_tm, tt, wc, x)


def _attn_kernel(q_ref, k_ref, v_ref, tri_ref, o_ref, *, blk):
    qi = pl.program_id(2)
    q = q_ref[0]
    lane = lax.broadcasted_iota(jnp.int32, q.shape, 1)
    row = lax.broadcasted_iota(jnp.int32, (blk, blk), 0)
    col = lax.broadcasted_iota(jnp.int32, (blk, blk), 1)
    past = col < row
    nt = (((1,), (1,)), ((), ()))

    def one_head(qh):
        def block(kj, carry, acc, diag):
            off = pl.multiple_of(kj * blk, blk)
            kb = k_ref[0, pl.ds(off, blk), :]
            vb = v_ref[0, pl.ds(off, blk), :]
            z = lax.dot_general(qh, kb, nt, preferred_element_type=F32)
            sp = jnp.maximum(z, 0.0) + jnp.log(1.0 + jnp.exp(-jnp.abs(z)))
            lk = -sp
            if diag:
                lk = jnp.where(past, lk, 0.0)
            after = jnp.dot(lk.astype(BF16), tri_ref[...], preferred_element_type=F32) + carry
            w = jnp.exp((z - sp) + after)
            if diag:
                w = jnp.where(past, w, 0.0)
            acc = acc + jnp.dot(w.astype(BF16), vb, preferred_element_type=F32)
            carry = carry + jnp.sum(lk, axis=1, keepdims=True)
            return carry, acc

        carry, acc = block(qi, jnp.zeros((blk, 1), F32), jnp.zeros(q.shape, F32), True)

        def body(i, c):
            return block(qi - 1 - i, c[0], c[1], False)

        _, acc = lax.fori_loop(0, qi, body, (carry, acc))
        return acc

    zero = jnp.zeros_like(q)
    acc0 = one_head(jnp.where(lane < HEAD_DIM, q, zero))
    acc1 = one_head(jnp.where(lane >= HEAD_DIM, q, zero))
    o_ref[0] = jnp.where(lane < HEAD_DIM, acc0, acc1).astype(BF16)


def stick_attention(proj3, width, blk=256):
    bsz, seqlen, _ = proj3.shape
    npair = width // 128
    ii = jnp.arange(blk)
    tri = (ii[:, None] > ii[None, :]).astype(BF16)
    return pl.pallas_call(
        functools.partial(_attn_kernel, blk=blk),
        grid=(bsz, npair, seqlen // blk),
        in_specs=[pl.BlockSpec((1, blk, 128), lambda b, h, i: (b, i, npair + h)),
                  pl.BlockSpec((1, seqlen, 128), lambda b, h, i: (b, 0, 2 * npair + h)),
                  pl.BlockSpec((1, seqlen, 128), lambda b, h, i: (b, 0, 3 * npair + h)),
                  pl.BlockSpec((blk, blk), lambda b, h, i: (0, 0))],
        out_specs=pl.BlockSpec((1, blk, 128), lambda b, h, i: (b, i, h)),
        out_shape=jax.ShapeDtypeStruct((bsz, seqlen, width), BF16),
        compiler_params=_cparams(3),
        name="attn",
    )(proj3, proj3, proj3, tri)


def _out_kernel(y_ref, a_ref, x_ref, mod_ref, wglu_ref, gs_ref, ga_ref, w0_ref, w1_ref, o_ref):
    y = jax.nn.gelu(y_ref[...].astype(F32))
    z = jnp.dot(y.astype(BF16), wglu_ref[...], preferred_element_type=F32)
    ys = y * jax.nn.sigmoid(z)
    ysn = (_rms(ys) * gs_ref[...]).astype(BF16)
    an = (_rms(a_ref[...].astype(F32)) * ga_ref[...]).astype(BF16)
    o = (jnp.dot(ysn, w0_ref[...], preferred_element_type=F32)
         + jnp.dot(an, w1_ref[...], preferred_element_type=F32))
    o_ref[...] = x_ref[...] + mod_ref[0, 2:3, :] * o


def out_proj(y2, a2, x2, mod3, w_glu_bf, g_ssm, g_attn, w_out_bf, seqlen, tm=512):
    t, d = x2.shape
    ws = y2.shape[1]
    wa = a2.shape[1]
    return pl.pallas_call(
        _out_kernel,
        grid=(t // tm,),
        in_specs=[pl.BlockSpec((tm, ws), lambda i: (i, 0)),
                  pl.BlockSpec((tm, wa), lambda i: (i, 0)),
                  pl.BlockSpec((tm, d), lambda i: (i, 0)),
                  pl.BlockSpec((1, 6, d), lambda i: ((i * tm) // seqlen, 0, 0)),
                  pl.BlockSpec((ws, ws), lambda i: (0, 0)),
                  pl.BlockSpec((1, ws), lambda i: (0, 0)),
                  pl.BlockSpec((1, wa), lambda i: (0, 0)),
                  pl.BlockSpec((ws, d), lambda i: (0, 0)),
                  pl.BlockSpec((wa, d), lambda i: (1, 0))],
        out_specs=pl.BlockSpec((tm, d), lambda i: (i, 0)),
        out_shape=jax.ShapeDtypeStruct((t, d), F32),
        compiler_params=_cparams(1),
        name="outproj",
    )(y2, a2, x2, mod3, w_glu_bf, g_ssm.reshape(1, ws), g_attn.reshape(1, wa), w_out_bf, w_out_bf)


def _ffn_kernel(h_ref, mod_ref, g_ref, wg_ref, wu_ref, wd_ref, o_ref, xf_ref, acc_ref):
    j = pl.program_id(1)

    @pl.when(j == 0)
    def _():
        xn = _rms(h_ref[...]) * g_ref[...]
        xf_ref[...] = (xn * (1.0 + mod_ref[0, 4:5, :]) + mod_ref[0, 3:4, :]).astype(BF16)
        acc_ref[...] = jnp.zeros_like(acc_ref)

    xf = xf_ref[...]
    g = jnp.dot(xf, wg_ref[...], preferred_element_type=F32)
    u = jnp.dot(xf, wu_ref[...], preferred_element_type=F32)
    hm = (jax.nn.silu(g) * u).astype(BF16)
    acc_ref[...] += jnp.dot(hm, wd_ref[...], preferred_element_type=F32)

    @pl.when(j == pl.num_programs(1) - 1)
    def _():
        o_ref[...] = h_ref[...] + mod_ref[0, 5:6, :] * acc_ref[...]


def ffn(h2, mod3, g_ffn, wg_bf, wu_bf, wd_bf, seqlen, tm=512, tf=512):
    t, d = h2.shape
    dff = wg_bf.shape[1]
    return pl.pallas_call(
        _ffn_kernel,
        grid=(t // tm, dff // tf),
        in_specs=[pl.BlockSpec((tm, d), lambda i, j: (i, 0)),
                  pl.BlockSpec((1, 6, d), lambda i, j: ((i * tm) // seqlen, 0, 0)),
                  pl.BlockSpec((1, d), lambda i, j: (0, 0)),
                  pl.BlockSpec((d, tf), lambda i, j: (0, j)),
                  pl.BlockSpec((d, tf), lambda i, j: (0, j)),
                  pl.BlockSpec((tf, d), lambda i, j: (j, 0))],
        out_specs=pl.BlockSpec((tm, d), lambda i, j: (i, 0)),
        out_shape=jax.ShapeDtypeStruct((t, d), F32),
        scratch_shapes=[pltpu.VMEM((tm, d), BF16), pltpu.VMEM((tm, d), F32)],
        compiler_params=_cparams(2),
        name="ffn",
    )(h2, mod3, g_ffn.reshape(1, d), wg_bf, wu_bf, wd_bf)


def s5_mixer_chunked(u3, a_re, a_im, log_dt, b_re, b_im, c_re, c_im, d_skip):
    bsz, seqlen, width = u3.shape
    g, p = a_re.shape
    nc = seqlen // CHUNK
    ngt = width // LANE_TILE
    blr, bli, clr, cli, kx, a = s5_prep(a_re, a_im, log_dt, b_re, b_im, c_re, c_im, d_skip)
    tt, wb, wc = s5_assemble(blr, bli, clr, cli, kx)
    u_tm = u3.reshape(bsz, nc, CHUNK, ngt, LANE_TILE).transpose(0, 2, 3, 1, 4)
    s = s5_chunk_inputs(u_tm, wb)
    x = s5_chunk_scan(s, a.reshape(2, 1, g * p))
    y_tm = s5_outputs(u_tm, tt, wc, x)
    return y_tm.transpose(0, 3, 1, 2, 4).reshape(bsz, seqlen, width)


def kernel(x, c, w_ada, b_ada, g_mix, w_in, a_re, a_im, log_dt, b_re, b_im, c_re, c_im, d_skip,
           w_glu, q_gain, k_gain, g_ssm_out, g_attn_out, w_out, g_ffn, w_gate, w_up, w_down):
    bsz, seqlen, d = x.shape
    depth = w_ada.shape[0]
    width = w_glu.shape[1]
    h2 = x.reshape(bsz * seqlen, d)
    for l in range(depth):
        mod3 = ada_mod(c, w_ada[l], b_ada[l]).reshape(bsz, 6, d)
        proj = in_proj(h2, mod3, g_mix[l], w_in[l].astype(BF16), q_gain[l], k_gain[l], seqlen)
        proj3 = proj.reshape(bsz, seqlen, 4 * width)
        y = s5_mixer_chunked(proj3[:, :, :width], a_re[l], a_im[l], log_dt[l], b_re[l], b_im[l],
                             c_re[l], c_im[l], d_skip[l])
        att = stick_attention(proj3, width)
        h2 = out_proj(y.reshape(bsz * seqlen, width), att.reshape(bsz * seqlen, width), h2, mod3,
                      w_glu[l].astype(BF16), g_ssm_out[l], g_attn_out[l], w_out[l].astype(BF16), seqlen)
        h2 = ffn(h2, mod3, g_ffn[l], w_gate[l].astype(BF16), w_up[l].astype(BF16),
                 w_down[l].astype(BF16), seqlen)
    return h2.reshape(bsz, seqlen, d)
```

```python
import functools

import jax
import jax.numpy as jnp
from jax import lax
from jax.experimental import pallas as pl
from jax.experimental.pallas import tpu as pltpu

F32 = jnp.float32
BF16 = jnp.bfloat16
EPS = 1e-6
HEAD_DIM = 64
CHUNK = 16
LANE_TILE = 256
VMEM_LIMIT = 56 * 1024 * 1024
Q_SCALE = HEAD_DIM ** -0.5 * 1.4426950408889634


def _cparams(n_axes, vmem=VMEM_LIMIT):
    return pltpu.CompilerParams(dimension_semantics=("arbitrary",) * n_axes,
                                vmem_limit_bytes=vmem)


def _rms(x):
    return x * lax.rsqrt(jnp.mean(x * x, axis=-1, keepdims=True) + EPS)


def _ada_kernel(c_ref, w_ref, b_ref, o_ref):
    d, tn = w_ref.shape
    rows = 256
    for b in range(c_ref.shape[0]):
        acc = jnp.zeros((8, tn), F32)
        for kc in range(d // rows):
            cond = jax.nn.silu(c_ref[b, kc * rows:(kc + 1) * rows, :])
            blk = cond * w_ref[kc * rows:(kc + 1) * rows, :]
            acc = acc + jnp.sum(blk.reshape(rows // 8, 8, tn), axis=0)
        o_ref[b:b + 1, :] = jnp.sum(acc, axis=0, keepdims=True) + b_ref[...]


def ada_mod(c, w_ada, b_ada, tn=1024):
    bsz, d = c.shape
    n = w_ada.shape[1]
    return pl.pallas_call(
        _ada_kernel,
        grid=(n // tn,),
        in_specs=[pl.BlockSpec((bsz, d, 1), lambda j: (0, 0, 0)),
                  pl.BlockSpec((d, tn), lambda j: (0, j)),
                  pl.BlockSpec((1, tn), lambda j: (0, j))],
        out_specs=pl.BlockSpec((bsz, tn), lambda j: (0, j)),
        out_shape=jax.ShapeDtypeStruct((bsz, n), F32),
        compiler_params=_cparams(1),
        name="ada",
    )(c.reshape(bsz, d, 1), w_ada, b_ada.reshape(1, n))


def _head_rmsnorm(r, ones_ref):
    outs = []
    for s in range(r.shape[1] // LANE_TILE):
        rs = r[:, s * LANE_TILE:(s + 1) * LANE_TILE]
        ss = jnp.dot((rs * rs).astype(BF16), ones_ref[...], preferred_element_type=F32)
        outs.append(rs * lax.rsqrt(ss * (1.0 / HEAD_DIM) + EPS))
    return jnp.concatenate(outs, axis=1)


def _proj_kernel(x_ref, mod_ref, g_ref, w_ref, qg_ref, kg_ref, ones_ref, o_ref, xm_ref):
    j = pl.program_id(1)

    @pl.when(j == 0)
    def _():
        xn = _rms(x_ref[...]) * g_ref[...]
        xm_ref[...] = (xn * (1.0 + mod_ref[0, 1:2, :]) + mod_ref[0, 0:1, :]).astype(BF16)

    r = jnp.dot(xm_ref[...], w_ref[...], preferred_element_type=F32)

    @pl.when(j == 1)
    def _():
        o_ref[...] = ((_head_rmsnorm(r, ones_ref) * qg_ref[...]) * Q_SCALE).astype(BF16)

    @pl.when(j == 2)
    def _():
        o_ref[...] = (_head_rmsnorm(r, ones_ref) * kg_ref[...]).astype(BF16)

    @pl.when((j == 0) | (j == 3))
    def _():
        o_ref[...] = r.astype(BF16)


def in_proj(x2, mod3, g_mix, w_in_bf, q_gain, k_gain, seqlen, tm=512):
    t, d = x2.shape
    n = w_in_bf.shape[1]
    tn = n // 4
    reps = tn // HEAD_DIM
    qg = jnp.tile(q_gain.reshape(1, HEAD_DIM), (1, reps))
    kg = jnp.tile(k_gain.reshape(1, HEAD_DIM), (1, reps))
    idx = jnp.arange(LANE_TILE) // HEAD_DIM
    ones = (idx[:, None] == idx[None, :]).astype(BF16)
    return pl.pallas_call(
        _proj_kernel,
        grid=(t // tm, 4),
        in_specs=[pl.BlockSpec((tm, d), lambda i, j: (i, 0)),
                  pl.BlockSpec((1, 6, d), lambda i, j: ((i * tm) // seqlen, 0, 0)),
                  pl.BlockSpec((1, d), lambda i, j: (0, 0)),
                  pl.BlockSpec((d, tn), lambda i, j: (0, j)),
                  pl.BlockSpec((1, tn), lambda i, j: (0, 0)),
                  pl.BlockSpec((1, tn), lambda i, j: (0, 0)),
                  pl.BlockSpec((LANE_TILE, LANE_TILE), lambda i, j: (0, 0))],
        out_specs=pl.BlockSpec((tm, tn), lambda i, j: (i, j)),
        out_shape=jax.ShapeDtypeStruct((t, n), BF16),
        scratch_shapes=[pltpu.VMEM((tm, d), BF16)],
        compiler_params=_cparams(2),
        name="proj",
    )(x2, mod3, g_mix.reshape(1, d), w_in_bf, qg, kg, ones)


def _s5prep_kernel(ar_ref, ai_ref, ldt_ref, btr_ref, bti_ref, cr_ref, ci_ref, d_ref,
                   blr_ref, bli_ref, clr_ref, cli_ref, kx_ref, a_ref):
    ar, ai = ar_ref[...], ai_ref[...]
    dt = jnp.exp(ldt_ref[...])
    mag = jnp.exp(ar * dt)
    th = ai * dt
    lbr, lbi = mag * jnp.cos(th), mag * jnp.sin(th)
    den = ar * ar + ai * ai
    nr, ni = lbr - 1.0, lbi
    kr = (nr * ar + ni * ai) / den
    ki = (ni * ar - nr * ai) / den
    btr, bti = btr_ref[...], bti_ref[...]
    bbr = kr * btr - ki * bti
    bbi = kr * bti + ki * btr
    c_re, c_im = cr_ref[...], ci_ref[...]
    h = c_re.shape[1]
    eye = (lax.broadcasted_iota(jnp.int32, (1, h, h), 1) ==
           lax.broadcasted_iota(jnp.int32, (1, h, h), 2))
    dn = (((2,), (2,)), ((0,), (0,)))
    pr, pi = jnp.ones_like(ar), jnp.zeros_like(ar)
    for j in range(CHUNK):
        blr = pr * bbr - pi * bbi
        bli = pr * bbi + pi * bbr
        blr_ref[j] = blr
        bli_ref[j] = bli
        kx = (lax.dot_general(blr, c_re, dn, precision=lax.Precision.HIGHEST,
                              preferred_element_type=F32)
              - lax.dot_general(bli, c_im, dn, precision=lax.Precision.HIGHEST,
                                preferred_element_type=F32))
        if j == 0:
            kx = kx + jnp.where(eye, d_ref[...], 0.0)
        kx_ref[j] = kx
        pr, pi = pr * lbr - pi * lbi, pr * lbi + pi * lbr
        clr_ref[j] = c_re * pr - c_im * pi
        cli_ref[j] = -(c_re * pi + c_im * pr)
    a_ref[0] = pr
    a_ref[1] = pi


def s5_prep(a_re, a_im, log_dt, b_re, b_im, c_re, c_im, d_skip, gb=8):
    g, p = a_re.shape
    h = c_re.shape[1]
    vec = pl.BlockSpec((gb, 1, p), lambda i: (i, 0, 0))
    mat = pl.BlockSpec((gb, h, p), lambda i: (i, 0, 0))
    lag = pl.BlockSpec((CHUNK, gb, h, p), lambda i: (0, i, 0, 0))
    lag_shape = jax.ShapeDtypeStruct((CHUNK, g, h, p), F32)
    return pl.pallas_call(
        _s5prep_kernel,
        grid=(g // gb,),
        in_specs=[vec, vec, pl.BlockSpec((gb, 1, 1), lambda i: (i, 0, 0)),
                  mat, mat, mat, mat, pl.BlockSpec((gb, 1, h), lambda i: (i, 0, 0))],
        out_specs=[lag, lag, lag, lag,
                   pl.BlockSpec((CHUNK, gb, h, h), lambda i: (0, i, 0, 0)),
                   pl.BlockSpec((2, gb, 1, p), lambda i: (0, i, 0, 0))],
        out_shape=[lag_shape, lag_shape, lag_shape, lag_shape,
                   jax.ShapeDtypeStruct((CHUNK, g, h, h), F32),
                   jax.ShapeDtypeStruct((2, g, 1, p), F32)],
        compiler_params=_cparams(1),
        name="s5prep",
    )(a_re.reshape(g, 1, p), a_im.reshape(g, 1, p), log_dt.reshape(g, 1, 1),
      b_re.transpose(0, 2, 1), b_im.transpose(0, 2, 1), c_re, c_im, d_skip.reshape(g, 1, h))


def s5_assemble(blr, bli, clr, cli, kx):
    n_lag, g, h, p = blr.shape
    gpt = LANE_TILE // h
    gps = LANE_TILE // p
    ngt = g // gpt
    nsub = gpt // gps
    k5 = kx.reshape(n_lag, ngt, gpt, h, h)
    eye_g = jnp.eye(gpt, dtype=bool)[None, None, :, None, :, None]
    tt = jnp.where(eye_g, k5[:, :, :, :, None, :], 0.0).reshape(n_lag, ngt, LANE_TILE, LANE_TILE)
    gi = jnp.arange(gpt)
    m_in = (gi[None, :, None] == (gps * jnp.arange(nsub)[:, None, None] + jnp.arange(gps)[None, None, :]))
    bl = jnp.stack([blr, bli], axis=0).reshape(2, n_lag, ngt, gpt, h, p).transpose(1, 2, 0, 3, 4, 5)
    wb = jnp.where(m_in[None, None, None, :, :, None, :, None],
                   bl[:, :, :, None, :, :, None, :], 0.0)
    wb = wb.reshape(n_lag, ngt, 2, nsub, LANE_TILE, LANE_TILE)
    cl = jnp.stack([clr, cli], axis=0).reshape(2, n_lag, ngt, nsub, gps, h, p)
    cl = cl.transpose(1, 2, 0, 3, 4, 6, 5)
    m_out = (gi[None, None, :] == (gps * jnp.arange(nsub)[:, None, None] + jnp.arange(gps)[None, :, None]))
    wc = jnp.where(m_out[None, None, None, :, :, None, :, None],
                   cl[:, :, :, :, :, :, None, :], 0.0)
    wc = wc.reshape(n_lag, ngt, 2, nsub, LANE_TILE, LANE_TILE)
    return tt.astype(BF16), wb.astype(BF16), wc.astype(BF16)


def _s5in_kernel(u_ref, w_ref, o_ref):
    nsub = w_ref.shape[3]
    for sub in range(nsub):
        acc = None
        for s in range(CHUNK):
            d = jnp.dot(u_ref[0, s, 0], w_ref[CHUNK - 1 - s, 0, 0, sub], preferred_element_type=F32)
            acc = d if acc is None else acc + d
        o_ref[0, 0, :, sub * LANE_TILE:(sub + 1) * LANE_TILE] = acc


def s5_chunk_inputs(u_tm, wb):
    bsz, _, ngt, nc, _ = u_tm.shape
    nsub = wb.shape[3]
    wid = nsub * LANE_TILE
    return pl.pallas_call(
        _s5in_kernel,
        grid=(ngt, 2, bsz),
        in_specs=[pl.BlockSpec((1, CHUNK, 1, nc, LANE_TILE), lambda gt, ri, b: (b, 0, gt, 0, 0)),
                  pl.BlockSpec((CHUNK, 1, 1, nsub, LANE_TILE, LANE_TILE),
                               lambda gt, ri, b: (0, gt, ri, 0, 0, 0))],
        out_specs=pl.BlockSpec((1, 1, nc, wid), lambda gt, ri, b: (b, ri, 0, gt)),
        out_shape=jax.ShapeDtypeStruct((bsz, 2, nc, ngt * wid), F32),
        compiler_params=_cparams(3),
        name="s5in",
    )(u_tm, wb)


def _s5scan_kernel(s_ref, a_ref, x_ref):
    nc, w = s_ref.shape[2], s_ref.shape[3]
    a1r, a1i = a_ref[0], a_ref[1]

    def cmul(xr, xi, yr, yi):
        return xr * yr - xi * yi, xr * yi + xi * yr

    row = lax.broadcasted_iota(jnp.int32, (8, w), 0)
    pows = [(a1r, a1i)]
    for _ in range(7):
        pows.append(cmul(pows[-1][0], pows[-1][1], a1r, a1i))
    steps = []
    for k in (1, 2, 4):
        steps.append((k, jnp.where(row >= k, pows[k - 1][0], 0.0), jnp.where(row >= k, pows[k - 1][1], 0.0)))
    pcr = jnp.zeros((8, w), F32)
    pci = jnp.zeros((8, w), F32)
    for r in range(8):
        pcr = jnp.where(row == r, pows[r][0], pcr)
        pci = jnp.where(row == r, pows[r][1], pci)

    def body(blk, carry):
        cr, ci = carry
        off = pl.multiple_of(blk * 8, 8)
        xr = s_ref[0, 0, pl.ds(off, 8), :]
        xi = s_ref[0, 1, pl.ds(off, 8), :]
        for k, kr, ki in steps:
            sr, si = pltpu.roll(xr, k, 0), pltpu.roll(xi, k, 0)
            dr, di = cmul(sr, si, kr, ki)
            xr, xi = xr + dr, xi + di
        dr, di = cmul(jnp.broadcast_to(cr, (8, w)), jnp.broadcast_to(ci, (8, w)), pcr, pci)
        xr, xi = xr + dr, xi + di
        x_ref[0, 0, pl.ds(off, 8), :] = jnp.where(row == 0, cr, pltpu.roll(xr, 1, 0))
        x_ref[0, 1, pl.ds(off, 8), :] = jnp.where(row == 0, ci, pltpu.roll(xi, 1, 0))
        return xr[7:8, :], xi[7:8, :]

    zero = jnp.zeros((1, w), F32)
    lax.fori_loop(0, nc // 8, body, (zero, zero))


def s5_chunk_scan(s, a, w=1024):
    bsz, _, nc, tot = s.shape
    blk = pl.BlockSpec((1, 2, nc, w), lambda b, l: (b, 0, 0, l))
    return pl.pallas_call(
        _s5scan_kernel,
        grid=(bsz, tot // w),
        in_specs=[blk, pl.BlockSpec((2, 1, w), lambda b, l: (0, 0, l))],
        out_specs=blk,
        out_shape=jax.ShapeDtypeStruct(s.shape, F32),
        compiler_params=_cparams(2),
        name="s5scan",
    )(s, a)


def _s5out_kernel(u_ref, tt_ref, wc_ref, x_ref, o_ref):
    t = pl.program_id(2)
    nc = u_ref.shape[3]
    nsub = wc_ref.shape[3]

    def body(s, acc):
        return acc + jnp.dot(u_ref[0, s, 0], tt_ref[t - s, 0], preferred_element_type=F32)

    acc = lax.fori_loop(0, t + 1, body, jnp.zeros((nc, LANE_TILE), F32))
    for ri in range(2):
        for sub in range(nsub):
            xs = x_ref[0, ri, :, sub * LANE_TILE:(sub + 1) * LANE_TILE].astype(BF16)
            acc = acc + jnp.dot(xs, wc_ref[0, 0, ri, sub], preferred_element_type=F32)
    o_ref[0, 0, 0] = acc.astype(BF16)


def s5_outputs(u_tm, tt, wc, x):
    bsz, _, ngt, nc, _ = u_tm.shape
    nsub = wc.shape[3]
    wid = nsub * LANE_TILE
    return pl.pallas_call(
        _s5out_kernel,
        grid=(bsz, ngt, CHUNK),
        in_specs=[pl.BlockSpec((1, CHUNK, 1, nc, LANE_TILE), lambda b, gt, t: (b, 0, gt, 0, 0)),
                  pl.BlockSpec((CHUNK, 1, LANE_TILE, LANE_TILE), lambda b, gt, t: (0, gt, 0, 0)),
                  pl.BlockSpec((1, 1, 2, nsub, LANE_TILE, LANE_TILE), lambda b, gt, t: (t, gt, 0, 0, 0, 0)),
                  pl.BlockSpec((1, 2, nc, wid), lambda b, gt, t: (b, 0, 0, gt))],
        out_specs=pl.BlockSpec((1, 1, 1, nc, LANE_TILE), lambda b, gt, t: (b, t, gt, 0, 0)),
        out_shape=jax.ShapeDtypeStruct(u_tm.shape, BF16),
        compiler_params=_cparams(3),
        name="s5out",
    )(u_tm, tt, wc, x)


def _attn_kernel(q_ref, k_ref, v_ref, tri_ref, o_ref, *, blk, nh):
    qi = pl.program_id(2)
    q = q_ref[0]
    lane = lax.broadcasted_iota(jnp.int32, q.shape, 1)
    heads = [(lane >= HEAD_DIM * h) & (lane < HEAD_DIM * (h + 1)) for h in range(nh)]
    zero = jnp.zeros_like(q)
    qs = jnp.concatenate([jnp.where(m, q, zero) for m in heads], axis=0)
    row = lax.broadcasted_iota(jnp.int32, (blk, blk), 0)
    col = lax.broadcasted_iota(jnp.int32, (blk, blk), 1)
    past = jnp.concatenate([col < row] * nh, axis=0)
    nt = (((1,), (1,)), ((), ()))

    def block(kj, carry, acc, diag):
        off = pl.multiple_of(kj * blk, blk)
        kb = k_ref[0, pl.ds(off, blk), :]
        vb = v_ref[0, pl.ds(off, blk), :]
        z = lax.dot_general(qs, kb, nt, preferred_element_type=F32)
        sp = jnp.maximum(z, 0.0) + jnp.log2(1.0 + jnp.exp2(-jnp.abs(z)))
        if diag:
            sp = jnp.where(past, sp, 0.0)
        cs = jnp.dot(sp.astype(BF16), tri_ref[...], preferred_element_type=F32)
        w = jnp.exp2(((z - sp) - cs) - carry)
        if diag:
            w = jnp.where(past, w, 0.0)
        wb = w.astype(BF16)
        wcat = jnp.concatenate([wb[h * blk:(h + 1) * blk] for h in range(nh)], axis=1)
        vcat = jnp.concatenate([jnp.where(m, vb, zero) for m in heads], axis=0)
        acc = acc + jnp.dot(wcat, vcat, preferred_element_type=F32)
        carry = carry + (cs[:, 0:1] + sp[:, 0:1])
        return carry, acc

    carry, acc = block(qi, jnp.zeros((nh * blk, 1), F32), jnp.zeros(q.shape, F32), True)

    def body(i, c):
        return block(qi - 1 - i, c[0], c[1], False)

    _, acc = lax.fori_loop(0, qi, body, (carry, acc))
    o_ref[0] = acc.astype(BF16)


def stick_attention(proj3, width, blk=256, nh=4):
    bsz, seqlen, _ = proj3.shape
    wl = nh * HEAD_DIM
    ngrp = width // wl
    ii = jnp.arange(blk)
    tri = (ii[:, None] > ii[None, :]).astype(BF16)
    return pl.pallas_call(
        functools.partial(_attn_kernel, blk=blk, nh=nh),
        grid=(bsz, ngrp, seqlen // blk),
        in_specs=[pl.BlockSpec((1, blk, wl), lambda b, h, i: (b, i, ngrp + h)),
                  pl.BlockSpec((1, seqlen, wl), lambda b, h, i: (b, 0, 2 * ngrp + h)),
                  pl.BlockSpec((1, seqlen, wl), lambda b, h, i: (b, 0, 3 * ngrp + h)),
                  pl.BlockSpec((blk, blk), lambda b, h, i: (0, 0))],
        out_specs=pl.BlockSpec((1, blk, wl), lambda b, h, i: (b, i, h)),
        out_shape=jax.ShapeDtypeStruct((bsz, seqlen, width), BF16),
        compiler_params=_cparams(3),
        name="attn",
    )(proj3, proj3, proj3, tri)


def _out_kernel(y_ref, a_ref, x_ref, mod_ref, wglu_ref, gs_ref, ga_ref, w0_ref, w1_ref, o_ref):
    y = jax.nn.gelu(y_ref[...].astype(F32))
    z = jnp.dot(y.astype(BF16), wglu_ref[...], preferred_element_type=F32)
    ys = y * jax.nn.sigmoid(z)
    ysn = (_rms(ys) * gs_ref[...]).astype(BF16)
    an = (_rms(a_ref[...].astype(F32)) * ga_ref[...]).astype(BF16)
    o = (jnp.dot(ysn, w0_ref[...], preferred_element_type=F32)
         + jnp.dot(an, w1_ref[...], preferred_element_type=F32))
    o_ref[...] = x_ref[...] + mod_ref[0, 2:3, :] * o


def out_proj(y2, a2, x2, mod3, w_glu_bf, g_ssm, g_attn, w_out_bf, seqlen, tm=512):
    t, d = x2.shape
    ws = y2.shape[1]
    wa = a2.shape[1]
    return pl.pallas_call(
        _out_kernel,
        grid=(t // tm,),
        in_specs=[pl.BlockSpec((tm, ws), lambda i: (i, 0)),
                  pl.BlockSpec((tm, wa), lambda i: (i, 0)),
                  pl.BlockSpec((tm, d), lambda i: (i, 0)),
                  pl.BlockSpec((1, 6, d), lambda i: ((i * tm) // seqlen, 0, 0)),
                  pl.BlockSpec((ws, ws), lambda i: (0, 0)),
                  pl.BlockSpec((1, ws), lambda i: (0, 0)),
                  pl.BlockSpec((1, wa), lambda i: (0, 0)),
                  pl.BlockSpec((ws, d), lambda i: (0, 0)),
                  pl.BlockSpec((wa, d), lambda i: (1, 0))],
        out_specs=pl.BlockSpec((tm, d), lambda i: (i, 0)),
        out_shape=jax.ShapeDtypeStruct((t, d), F32),
        compiler_params=_cparams(1),
        name="outproj",
    )(y2, a2, x2, mod3, w_glu_bf, g_ssm.reshape(1, ws), g_attn.reshape(1, wa), w_out_bf, w_out_bf)


def _ffn_kernel(h_ref, mod_ref, g_ref, wg_ref, wu_ref, wd_ref, o_ref, xf_ref, acc_ref):
    j = pl.program_id(1)

    @pl.when(j == 0)
    def _():
        xn = _rms(h_ref[...]) * g_ref[...]
        xf_ref[...] = (xn * (1.0 + mod_ref[0, 4:5, :]) + mod_ref[0, 3:4, :]).astype(BF16)
        acc_ref[...] = jnp.zeros_like(acc_ref)

    xf = xf_ref[...]
    g = jnp.dot(xf, wg_ref[...], preferred_element_type=F32)
    u = jnp.dot(xf, wu_ref[...], preferred_element_type=F32)
    hm = (jax.nn.silu(g) * u).astype(BF16)
    acc_ref[...] += jnp.dot(hm, wd_ref[...], preferred_element_type=F32)

    @pl.when(j == pl.num_programs(1) - 1)
    def _():
        o_ref[...] = h_ref[...] + mod_ref[0, 5:6, :] * acc_ref[...]


def ffn(h2, mod3, g_ffn, wg_bf, wu_bf, wd_bf, seqlen, tm=512, tf=512):
    t, d = h2.shape
    dff = wg_bf.shape[1]
    return pl.pallas_call(
        _ffn_kernel,
        grid=(t // tm, dff // tf),
        in_specs=[pl.BlockSpec((tm, d), lambda i, j: (i, 0)),
                  pl.BlockSpec((1, 6, d), lambda i, j: ((i * tm) // seqlen, 0, 0)),
                  pl.BlockSpec((1, d), lambda i, j: (0, 0)),
                  pl.BlockSpec((d, tf), lambda i, j: (0, j)),
                  pl.BlockSpec((d, tf), lambda i, j: (0, j)),
                  pl.BlockSpec((tf, d), lambda i, j: (j, 0))],
        out_specs=pl.BlockSpec((tm, d), lambda i, j: (i, 0)),
        out_shape=jax.ShapeDtypeStruct((t, d), F32),
        scratch_shapes=[pltpu.VMEM((tm, d), BF16), pltpu.VMEM((tm, d), F32)],
        compiler_params=_cparams(2),
        name="ffn",
    )(h2, mod3, g_ffn.reshape(1, d), wg_bf, wu_bf, wd_bf)


def s5_mixer_chunked(u3, a_re, a_im, log_dt, b_re, b_im, c_re, c_im, d_skip):
    bsz, seqlen, width = u3.shape
    g, p = a_re.shape
    nc = seqlen // CHUNK
    ngt = width // LANE_TILE
    blr, bli, clr, cli, kx, a = s5_prep(a_re, a_im, log_dt, b_re, b_im, c_re, c_im, d_skip)
    tt, wb, wc = s5_assemble(blr, bli, clr, cli, kx)
    u_tm = u3.reshape(bsz, nc, CHUNK, ngt, LANE_TILE).transpose(0, 2, 3, 1, 4)
    s = s5_chunk_inputs(u_tm, wb)
    x = s5_chunk_scan(s, a.reshape(2, 1, g * p))
    y_tm = s5_outputs(u_tm, tt, wc, x)
    return y_tm.transpose(0, 3, 1, 2, 4).reshape(bsz, seqlen, width)


def kernel(x, c, w_ada, b_ada, g_mix, w_in, a_re, a_im, log_dt, b_re, b_im, c_re, c_im, d_skip,
           w_glu, q_gain, k_gain, g_ssm_out, g_attn_out, w_out, g_ffn, w_gate, w_up, w_down):
    bsz, seqlen, d = x.shape
    depth = w_ada.shape[0]
    width = w_glu.shape[1]
    h2 = x.reshape(bsz * seqlen, d)
    for l in range(depth):
        mod3 = ada_mod(c, w_ada[l], b_ada[l]).reshape(bsz, 6, d)
        proj = in_proj(h2, mod3, g_mix[l], w_in[l].astype(BF16), q_gain[l], k_gain[l], seqlen)
        proj3 = proj.reshape(bsz, seqlen, 4 * width)
        y = s5_mixer_chunked(proj3[:, :, :width], a_re[l], a_im[l], log_dt[l], b_re[l], b_im[l],
                             c_re[l], c_im[l], d_skip[l])
        att = stick_attention(proj3, width)
        h2 = out_proj(y.reshape(bsz * seqlen, width), att.reshape(bsz * seqlen, width), h2, mod3,
                      w_glu[l].astype(BF16), g_ssm_out[l], g_attn_out[l], w_out[l].astype(BF16), seqlen)
        h2 = ffn(h2, mod3, g_ffn[l], w_gate[l].astype(BF16), w_up[l].astype(BF16),
                 w_down[l].astype(BF16), seqlen)
    return h2.reshape(bsz, seqlen, d)
```

```python
import functools

import jax
import jax.numpy as jnp
from jax import lax
from jax.experimental import pallas as pl
from jax.experimental.pallas import tpu as pltpu

F32 = jnp.float32
BF16 = jnp.bfloat16
EPS = 1e-6
HEAD_DIM = 64
CHUNK = 16
LANE_TILE = 256
VMEM_LIMIT = 56 * 1024 * 1024
Q_SCALE = HEAD_DIM ** -0.5 * 1.4426950408889634


def _cparams(n_axes, vmem=VMEM_LIMIT):
    return pltpu.CompilerParams(dimension_semantics=("arbitrary",) * n_axes,
                                vmem_limit_bytes=vmem)


def _rms(x):
    return x * lax.rsqrt(jnp.mean(x * x, axis=-1, keepdims=True) + EPS)


def _ada_kernel(c_ref, w_ref, b_ref, o_ref):
    d, tn = w_ref.shape
    rows = 256
    for b in range(c_ref.shape[0]):
        acc = jnp.zeros((8, tn), F32)
        for kc in range(d // rows):
            cond = jax.nn.silu(c_ref[b, kc * rows:(kc + 1) * rows, :])
            blk = cond * w_ref[kc * rows:(kc + 1) * rows, :]
            acc = acc + jnp.sum(blk.reshape(rows // 8, 8, tn), axis=0)
        o_ref[b:b + 1, :] = jnp.sum(acc, axis=0, keepdims=True) + b_ref[...]


def ada_mod(c, w_ada, b_ada, tn=1024):
    bsz, d = c.shape
    n = w_ada.shape[1]
    return pl.pallas_call(
        _ada_kernel,
        grid=(n // tn,),
        in_specs=[pl.BlockSpec((bsz, d, 1), lambda j: (0, 0, 0)),
                  pl.BlockSpec((d, tn), lambda j: (0, j)),
                  pl.BlockSpec((1, tn), lambda j: (0, j))],
        out_specs=pl.BlockSpec((bsz, tn), lambda j: (0, j)),
        out_shape=jax.ShapeDtypeStruct((bsz, n), F32),
        compiler_params=_cparams(1),
        name="ada",
    )(c.reshape(bsz, d, 1), w_ada, b_ada.reshape(1, n))


def _head_rmsnorm(r, ones_ref):
    outs = []
    for s in range(r.shape[1] // LANE_TILE):
        rs = r[:, s * LANE_TILE:(s + 1) * LANE_TILE]
        ss = jnp.dot((rs * rs).astype(BF16), ones_ref[...], preferred_element_type=F32)
        outs.append(rs * lax.rsqrt(ss * (1.0 / HEAD_DIM) + EPS))
    return jnp.concatenate(outs, axis=1)


def _proj_kernel(x_ref, mod_ref, g_ref, w_ref, qg_ref, kg_ref, ones_ref, o_ref, xm_ref):
    j = pl.program_id(1)

    @pl.when(j == 0)
    def _():
        xn = _rms(x_ref[...]) * g_ref[...]
        xm_ref[...] = (xn * (1.0 + mod_ref[0, 1:2, :]) + mod_ref[0, 0:1, :]).astype(BF16)

    r = jnp.dot(xm_ref[...], w_ref[...], preferred_element_type=F32)

    @pl.when(j == 1)
    def _():
        o_ref[...] = ((_head_rmsnorm(r, ones_ref) * qg_ref[...]) * Q_SCALE).astype(BF16)

    @pl.when(j == 2)
    def _():
        o_ref[...] = (_head_rmsnorm(r, ones_ref) * kg_ref[...]).astype(BF16)

    @pl.when((j == 0) | (j == 3))
    def _():
        o_ref[...] = r.astype(BF16)


def in_proj(x2, mod3, g_mix, w_in_bf, q_gain, k_gain, seqlen, tm=512):
    t, d = x2.shape
    n = w_in_bf.shape[1]
    tn = n // 4
    reps = tn // HEAD_DIM
    qg = jnp.tile(q_gain.reshape(1, HEAD_DIM), (1, reps))
    kg = jnp.tile(k_gain.reshape(1, HEAD_DIM), (1, reps))
    idx = jnp.arange(LANE_TILE) // HEAD_DIM
    ones = (idx[:, None] == idx[None, :]).astype(BF16)
    return pl.pallas_call(
        _proj_kernel,
        grid=(t // tm, 4),
        in_specs=[pl.BlockSpec((tm, d), lambda i, j: (i, 0)),
                  pl.BlockSpec((1, 6, d), lambda i, j: ((i * tm) // seqlen, 0, 0)),
                  pl.BlockSpec((1, d), lambda i, j: (0, 0)),
                  pl.BlockSpec((d, tn), lambda i, j: (0, j)),
                  pl.BlockSpec((1, tn), lambda i, j: (0, 0)),
                  pl.BlockSpec((1, tn), lambda i, j: (0, 0)),
                  pl.BlockSpec((LANE_TILE, LANE_TILE), lambda i, j: (0, 0))],
        out_specs=pl.BlockSpec((tm, tn), lambda i, j: (i, j)),
        out_shape=jax.ShapeDtypeStruct((t, n), BF16),
        scratch_shapes=[pltpu.VMEM((tm, d), BF16)],
        compiler_params=_cparams(2),
        name="proj",
    )(x2, mod3, g_mix.reshape(1, d), w_in_bf, qg, kg, ones)


def _zoh(ar, ai, dt):
    mag = jnp.exp(ar * dt)
    th = ai * dt
    lbr, lbi = mag * jnp.cos(th), mag * jnp.sin(th)
    den = ar * ar + ai * ai
    nr, ni = lbr - 1.0, lbi
    return lbr, lbi, (nr * ar + ni * ai) / den, (ni * ar - nr * ai) / den


def _s5prep_kernel(ar_ref, ai_ref, ldt_ref, btr_ref, bti_ref, cr_ref, ci_ref, d_ref,
                   arc_ref, aic_ref, ctr_ref, cti_ref,
                   bl_ref, clt_ref, kx_ref, a_ref):
    dt = jnp.exp(ldt_ref[...])
    lbr, lbi, kr, ki = _zoh(ar_ref[...], ai_ref[...], dt)
    btr, bti = btr_ref[...], bti_ref[...]
    bbr = kr * btr - ki * bti
    bbi = kr * bti + ki * btr
    c_re, c_im = cr_ref[...], ci_ref[...]
    h = c_re.shape[1]
    eye = (lax.broadcasted_iota(jnp.int32, (1, h, h), 1) ==
           lax.broadcasted_iota(jnp.int32, (1, h, h), 2))
    dn = (((2,), (2,)), ((0,), (0,)))
    pr, pi = jnp.ones_like(lbr), jnp.zeros_like(lbr)
    for j in range(CHUNK):
        blr = pr * bbr - pi * bbi
        bli = pr * bbi + pi * bbr
        bl_ref[j, 0] = blr
        bl_ref[j, 1] = bli
        kx = (lax.dot_general(blr, c_re, dn, precision=lax.Precision.HIGHEST,
                              preferred_element_type=F32)
              - lax.dot_general(bli, c_im, dn, precision=lax.Precision.HIGHEST,
                                preferred_element_type=F32))
        if j == 0:
            kx = kx + jnp.where(eye, d_ref[...], 0.0)
        kx_ref[:, :, j * h:(j + 1) * h] = kx
        pr, pi = pr * lbr - pi * lbi, pr * lbi + pi * lbr
    a_ref[0] = pr
    a_ref[1] = pi
    lcr, lci, _, _ = _zoh(arc_ref[...], aic_ref[...], dt)
    ctr, cti = ctr_ref[...], cti_ref[...]
    qr, qi = lcr, lci
    for t in range(CHUNK):
        clt_ref[0, :, :, t * h:(t + 1) * h] = ctr * qr - cti * qi
        clt_ref[1, :, :, t * h:(t + 1) * h] = -(ctr * qi + cti * qr)
        qr, qi = qr * lcr - qi * lci, qr * lci + qi * lcr


def s5_prep(a_re, a_im, log_dt, b_re, b_im, c_re, c_im, d_skip, gb=8):
    g, p = a_re.shape
    h = c_re.shape[1]
    vec = pl.BlockSpec((gb, 1, p), lambda i: (i, 0, 0))
    col = pl.BlockSpec((gb, p, 1), lambda i: (i, 0, 0))
    mat = pl.BlockSpec((gb, h, p), lambda i: (i, 0, 0))
    matt = pl.BlockSpec((gb, p, h), lambda i: (i, 0, 0))
    return pl.pallas_call(
        _s5prep_kernel,
        grid=(g // gb,),
        in_specs=[vec, vec, pl.BlockSpec((gb, 1, 1), lambda i: (i, 0, 0)),
                  mat, mat, mat, mat, pl.BlockSpec((gb, 1, h), lambda i: (i, 0, 0)),
                  col, col, matt, matt],
        out_specs=[pl.BlockSpec((CHUNK, 2, gb, h, p), lambda i: (0, 0, i, 0, 0)),
                   pl.BlockSpec((2, gb, p, CHUNK * h), lambda i: (0, i, 0, 0)),
                   pl.BlockSpec((gb, h, CHUNK * h), lambda i: (i, 0, 0)),
                   pl.BlockSpec((2, gb, 1, p), lambda i: (0, i, 0, 0))],
        out_shape=[jax.ShapeDtypeStruct((CHUNK, 2, g, h, p), F32),
                   jax.ShapeDtypeStruct((2, g, p, CHUNK * h), F32),
                   jax.ShapeDtypeStruct((g, h, CHUNK * h), F32),
                   jax.ShapeDtypeStruct((2, g, 1, p), F32)],
        compiler_params=_cparams(1),
        name="s5prep",
    )(a_re.reshape(g, 1, p), a_im.reshape(g, 1, p), log_dt.reshape(g, 1, 1),
      b_re.transpose(0, 2, 1), b_im.transpose(0, 2, 1), c_re, c_im, d_skip.reshape(g, 1, h),
      a_re.reshape(g, p, 1), a_im.reshape(g, p, 1), c_re.transpose(0, 2, 1), c_im.transpose(0, 2, 1))


def _group_mask(shape, row_div, col_div):
    return (lax.broadcasted_iota(jnp.int32, shape, 0) // row_div ==
            lax.broadcasted_iota(jnp.int32, shape, 1) // col_div)


def _s5in_kernel(u_ref, bl_ref, e_ref, o_ref, w_ref):
    p, wid = e_ref.shape
    h = LANE_TILE // (wid // p)
    mask = _group_mask((LANE_TILE, wid), h, p)
    for s in range(CHUNK):
        ex = jnp.dot(bl_ref[CHUNK - 1 - s, 0].astype(BF16), e_ref[...], preferred_element_type=F32)
        w_ref[s * LANE_TILE:(s + 1) * LANE_TILE, :] = jnp.where(mask, ex, 0.0).astype(BF16)
    for b in range(u_ref.shape[0]):
        ucat = jnp.concatenate([u_ref[b, s, 0] for s in range(CHUNK)], axis=1)
        o_ref[b, 0] = jnp.dot(ucat, w_ref[...], preferred_element_type=F32)


def s5_chunk_inputs(u_tm, bl2, h):
    bsz, _, ngt, nc, _ = u_tm.shape
    p = bl2.shape[3]
    wid = (LANE_TILE // h) * p
    e = (jnp.arange(wid)[None, :] % p == jnp.arange(p)[:, None]).astype(BF16)
    return pl.pallas_call(
        _s5in_kernel,
        grid=(ngt, 2),
        in_specs=[pl.BlockSpec((bsz, CHUNK, 1, nc, LANE_TILE), lambda gt, ri: (0, 0, gt, 0, 0)),
                  pl.BlockSpec((CHUNK, 1, LANE_TILE, p), lambda gt, ri: (0, ri, gt, 0)),
                  pl.BlockSpec((p, wid), lambda gt, ri: (0, 0))],
        out_specs=pl.BlockSpec((bsz, 1, nc, wid), lambda gt, ri: (0, ri, 0, gt)),
        out_shape=jax.ShapeDtypeStruct((bsz, 2, nc, ngt * wid), F32),
        scratch_shapes=[pltpu.VMEM((CHUNK * LANE_TILE, wid), BF16)],
        compiler_params=_cparams(2),
        name="s5in",
    )(u_tm, bl2, e)


def _s5scan_kernel(s_ref, a_ref, x_ref):
    nc, w = s_ref.shape[2], s_ref.shape[3]
    a1r, a1i = a_ref[0], a_ref[1]

    def cmul(xr, xi, yr, yi):
        return xr * yr - xi * yi, xr * yi + xi * yr

    row = lax.broadcasted_iota(jnp.int32, (8, w), 0)
    pows = [(a1r, a1i)]
    for _ in range(7):
        pows.append(cmul(pows[-1][0], pows[-1][1], a1r, a1i))
    steps = []
    for k in (1, 2, 4):
        steps.append((k, jnp.where(row >= k, pows[k - 1][0], 0.0), jnp.where(row >= k, pows[k - 1][1], 0.0)))
    pcr = jnp.zeros((8, w), F32)
    pci = jnp.zeros((8, w), F32)
    for r in range(8):
        pcr = jnp.where(row == r, pows[r][0], pcr)
        pci = jnp.where(row == r, pows[r][1], pci)

    def body(blk, carry):
        cr, ci = carry
        off = pl.multiple_of(blk * 8, 8)
        xr = s_ref[0, 0, pl.ds(off, 8), :]
        xi = s_ref[0, 1, pl.ds(off, 8), :]
        for k, kr, ki in steps:
            sr, si = pltpu.roll(xr, k, 0), pltpu.roll(xi, k, 0)
            dr, di = cmul(sr, si, kr, ki)
            xr, xi = xr + dr, xi + di
        dr, di = cmul(jnp.broadcast_to(cr, (8, w)), jnp.broadcast_to(ci, (8, w)), pcr, pci)
        xr, xi = xr + dr, xi + di
        x_ref[0, 0, pl.ds(off, 8), :] = jnp.where(row == 0, cr, pltpu.roll(xr, 1, 0))
        x_ref[0, 1, pl.ds(off, 8), :] = jnp.where(row == 0, ci, pltpu.roll(xi, 1, 0))
        return xr[7:8, :], xi[7:8, :]

    zero = jnp.zeros((1, w), F32)
    lax.fori_loop(0, nc // 8, body, (zero, zero))


def s5_chunk_scan(s, a, w=1024):
    bsz, _, nc, tot = s.shape
    blk = pl.BlockSpec((1, 2, nc, w), lambda b, l: (b, 0, 0, l))
    return pl.pallas_call(
        _s5scan_kernel,
        grid=(bsz, tot // w),
        in_specs=[blk, pl.BlockSpec((2, 1, w), lambda b, l: (0, 0, l))],
        out_specs=blk,
        out_shape=jax.ShapeDtypeStruct(s.shape, F32),
        compiler_params=_cparams(2),
        name="s5scan",
    )(s, a)


def _s5out_kernel(u_ref, kx_ref, clt_ref, x_ref, e_ref, o_ref, acc_ref):
    bsz, _, _, nc, _ = u_ref.shape
    wid = x_ref.shape[3]
    h = LANE_TILE // CHUNK
    p = wid // (LANE_TILE // h)
    mask_t = _group_mask((LANE_TILE, LANE_TILE), h, h)
    mask_c = _group_mask((wid, LANE_TILE), p, h)
    kx = kx_ref[...].astype(BF16)
    for j in range(CHUNK):
        tj = jnp.where(mask_t, jnp.dot(kx, e_ref[j], preferred_element_type=F32), 0.0).astype(BF16)
        for b in range(bsz):
            lhs = u_ref[b, 0:CHUNK - j, 0].reshape((CHUNK - j) * nc, LANE_TILE)
            r = jnp.dot(lhs, tj, preferred_element_type=F32).reshape(CHUNK - j, nc, LANE_TILE)
            if j == 0:
                acc_ref[b] = r
            else:
                acc_ref[b, j:CHUNK] += r
    clr, cli = clt_ref[0].astype(BF16), clt_ref[1].astype(BF16)
    xr = x_ref[:, 0].reshape(bsz * nc, wid).astype(BF16)
    xi = x_ref[:, 1].reshape(bsz * nc, wid).astype(BF16)
    for t in range(CHUNK):
        wr = jnp.where(mask_c, jnp.dot(clr, e_ref[t], preferred_element_type=F32), 0.0).astype(BF16)
        wi = jnp.where(mask_c, jnp.dot(cli, e_ref[t], preferred_element_type=F32), 0.0).astype(BF16)
        y = (jnp.dot(xr, wr, preferred_element_type=F32) + jnp.dot(xi, wi, preferred_element_type=F32))
        for b in range(bsz):
            o_ref[b, t, 0] = (acc_ref[b, t] + y[b * nc:(b + 1) * nc]).astype(BF16)


def s5_outputs(u_tm, kx2, clt2, x):
    bsz, _, ngt, nc, _ = u_tm.shape
    wid = x.shape[3] // ngt
    h = LANE_TILE // CHUNK
    r = jnp.arange(LANE_TILE)
    e = ((r[None, :, None] // h == jnp.arange(CHUNK)[:, None, None])
         & (r[None, :, None] % h == r[None, None, :] % h)).astype(BF16)
    tile = pl.BlockSpec((bsz, CHUNK, 1, nc, LANE_TILE), lambda gt: (0, 0, gt, 0, 0))
    return pl.pallas_call(
        _s5out_kernel,
        grid=(ngt,),
        in_specs=[tile,
                  pl.BlockSpec((LANE_TILE, LANE_TILE), lambda gt: (gt, 0)),
                  pl.BlockSpec((2, wid, LANE_TILE), lambda gt: (0, gt, 0)),
                  pl.BlockSpec((bsz, 2, nc, wid), lambda gt: (0, 0, 0, gt)),
                  pl.BlockSpec((CHUNK, LANE_TILE, LANE_TILE), lambda gt: (0, 0, 0))],
        out_specs=tile,
        out_shape=jax.ShapeDtypeStruct(u_tm.shape, BF16),
        scratch_shapes=[pltpu.VMEM((bsz, CHUNK, nc, LANE_TILE), F32)],
        compiler_params=_cparams(1),
        name="s5out",
    )(u_tm, kx2, clt2, x, e)


def _attn_kernel(q_ref, k_ref, v_ref, tri_ref, o_ref, *, blk, nh):
    qi = pl.program_id(2)
    q = q_ref[0]
    lane = lax.broadcasted_iota(jnp.int32, q.shape, 1)
    heads = [(lane >= HEAD_DIM * h) & (lane < HEAD_DIM * (h + 1)) for h in range(nh)]
    zero = jnp.zeros_like(q)
    qs = jnp.concatenate([jnp.where(m, q, zero) for m in heads], axis=0)
    row = lax.broadcasted_iota(jnp.int32, (blk, blk), 0)
    col = lax.broadcasted_iota(jnp.int32, (blk, blk), 1)
    past = jnp.concatenate([col < row] * nh, axis=0)
    nt = (((1,), (1,)), ((), ()))

    def block(kj, carry, acc, diag):
        off = pl.multiple_of(kj * blk, blk)
        kb = k_ref[0, pl.ds(off, blk), :]
        vb = v_ref[0, pl.ds(off, blk), :]
        z = lax.dot_general(qs, kb, nt, preferred_element_type=F32)
        sp = jnp.maximum(z, 0.0) + jnp.log2(1.0 + jnp.exp2(-jnp.abs(z)))
        if diag:
            sp = jnp.where(past, sp, 0.0)
        cs = jnp.dot(sp.astype(BF16), tri_ref[...], preferred_element_type=F32)
        w = jnp.exp2(((z - sp) - cs) - carry)
        if diag:
            w = jnp.where(past, w, 0.0)
        wb = w.astype(BF16)
        wcat = jnp.concatenate([wb[h * blk:(h + 1) * blk] for h in range(nh)], axis=1)
        vcat = jnp.concatenate([jnp.where(m, vb, zero) for m in heads], axis=0)
        acc = acc + jnp.dot(wcat, vcat, preferred_element_type=F32)
        carry = carry + (cs[:, 0:1] + sp[:, 0:1])
        return carry, acc

    carry, acc = block(qi, jnp.zeros((nh * blk, 1), F32), jnp.zeros(q.shape, F32), True)

    def body(i, c):
        return block(qi - 1 - i, c[0], c[1], False)

    _, acc = lax.fori_loop(0, qi, body, (carry, acc))
    o_ref[0] = acc.astype(BF16)


def stick_attention(proj3, width, blk=256, nh=4):
    bsz, seqlen, _ = proj3.shape
    wl = nh * HEAD_DIM
    ngrp = width // wl
    ii = jnp.arange(blk)
    tri = (ii[:, None] > ii[None, :]).astype(BF16)
    return pl.pallas_call(
        functools.partial(_attn_kernel, blk=blk, nh=nh),
        grid=(bsz, ngrp, seqlen // blk),
        in_specs=[pl.BlockSpec((1, blk, wl), lambda b, h, i: (b, i, ngrp + h)),
                  pl.BlockSpec((1, seqlen, wl), lambda b, h, i: (b, 0, 2 * ngrp + h)),
                  pl.BlockSpec((1, seqlen, wl), lambda b, h, i: (b, 0, 3 * ngrp + h)),
                  pl.BlockSpec((blk, blk), lambda b, h, i: (0, 0))],
        out_specs=pl.BlockSpec((1, blk, wl), lambda b, h, i: (b, i, h)),
        out_shape=jax.ShapeDtypeStruct((bsz, seqlen, width), BF16),
        compiler_params=_cparams(3),
        name="attn",
    )(proj3, proj3, proj3, tri)


def _out_kernel(y_ref, a_ref, x_ref, mod_ref, wglu_ref, gs_ref, ga_ref, w0_ref, w1_ref, o_ref):
    y = jax.nn.gelu(y_ref[...].astype(F32))
    z = jnp.dot(y.astype(BF16), wglu_ref[...], preferred_element_type=F32)
    ys = y * jax.nn.sigmoid(z)
    ysn = (_rms(ys) * gs_ref[...]).astype(BF16)
    an = (_rms(a_ref[...].astype(F32)) * ga_ref[...]).astype(BF16)
    o = (jnp.dot(ysn, w0_ref[...], preferred_element_type=F32)
         + jnp.dot(an, w1_ref[...], preferred_element_type=F32))
    o_ref[...] = x_ref[...] + mod_ref[0, 2:3, :] * o


def out_proj(y2, a2, x2, mod3, w_glu_bf, g_ssm, g_attn, w_out_bf, seqlen, tm=512):
    t, d = x2.shape
    ws = y2.shape[1]
    wa = a2.shape[1]
    return pl.pallas_call(
        _out_kernel,
        grid=(t // tm,),
        in_specs=[pl.BlockSpec((tm, ws), lambda i: (i, 0)),
                  pl.BlockSpec((tm, wa), lambda i: (i, 0)),
                  pl.BlockSpec((tm, d), lambda i: (i, 0)),
                  pl.BlockSpec((1, 6, d), lambda i: ((i * tm) // seqlen, 0, 0)),
                  pl.BlockSpec((ws, ws), lambda i: (0, 0)),
                  pl.BlockSpec((1, ws), lambda i: (0, 0)),
                  pl.BlockSpec((1, wa), lambda i: (0, 0)),
                  pl.BlockSpec((ws, d), lambda i: (0, 0)),
                  pl.BlockSpec((wa, d), lambda i: (1, 0))],
        out_specs=pl.BlockSpec((tm, d), lambda i: (i, 0)),
        out_shape=jax.ShapeDtypeStruct((t, d), F32),
        compiler_params=_cparams(1),
        name="outproj",
    )(y2, a2, x2, mod3, w_glu_bf, g_ssm.reshape(1, ws), g_attn.reshape(1, wa), w_out_bf, w_out_bf)


def _ffn_kernel(h_ref, mod_ref, g_ref, wg_ref, wu_ref, wd_ref, o_ref, xf_ref, acc_ref):
    j = pl.program_id(1)

    @pl.when(j == 0)
    def _():
        xn = _rms(h_ref[...]) * g_ref[...]
        xf_ref[...] = (xn * (1.0 + mod_ref[0, 4:5, :]) + mod_ref[0, 3:4, :]).astype(BF16)
        acc_ref[...] = jnp.zeros_like(acc_ref)

    xf = xf_ref[...]
    g = jnp.dot(xf, wg_ref[...], preferred_element_type=F32)
    u = jnp.dot(xf, wu_ref[...], preferred_element_type=F32)
    hm = (jax.nn.silu(g) * u).astype(BF16)
    acc_ref[...] += jnp.dot(hm, wd_ref[...], preferred_element_type=F32)

    @pl.when(j == pl.num_programs(1) - 1)
    def _():
        o_ref[...] = h_ref[...] + mod_ref[0, 5:6, :] * acc_ref[...]


def ffn(h2, mod3, g_ffn, wg_bf, wu_bf, wd_bf, seqlen, tm=512, tf=512):
    t, d = h2.shape
    dff = wg_bf.shape[1]
    return pl.pallas_call(
        _ffn_kernel,
        grid=(t // tm, dff // tf),
        in_specs=[pl.BlockSpec((tm, d), lambda i, j: (i, 0)),
                  pl.BlockSpec((1, 6, d), lambda i, j: ((i * tm) // seqlen, 0, 0)),
                  pl.BlockSpec((1, d), lambda i, j: (0, 0)),
                  pl.BlockSpec((d, tf), lambda i, j: (0, j)),
                  pl.BlockSpec((d, tf), lambda i, j: (0, j)),
                  pl.BlockSpec((tf, d), lambda i, j: (j, 0))],
        out_specs=pl.BlockSpec((tm, d), lambda i, j: (i, 0)),
        out_shape=jax.ShapeDtypeStruct((t, d), F32),
        scratch_shapes=[pltpu.VMEM((tm, d), BF16), pltpu.VMEM((tm, d), F32)],
        compiler_params=_cparams(2),
        name="ffn",
    )(h2, mod3, g_ffn.reshape(1, d), wg_bf, wu_bf, wd_bf)


def s5_mixer_chunked(u3, a_re, a_im, log_dt, b_re, b_im, c_re, c_im, d_skip):
    bsz, seqlen, width = u3.shape
    g, p = a_re.shape
    h = c_re.shape[1]
    nc = seqlen // CHUNK
    ngt = width // LANE_TILE
    assert h * CHUNK == LANE_TILE and g * h == width
    bl, clt, kx, a = s5_prep(a_re, a_im, log_dt, b_re, b_im, c_re, c_im, d_skip)
    u_tm = u3.reshape(bsz, nc, CHUNK, ngt, LANE_TILE).transpose(0, 2, 3, 1, 4)
    s = s5_chunk_inputs(u_tm, bl.reshape(CHUNK, 2, g * h, p), h)
    x = s5_chunk_scan(s, a.reshape(2, 1, g * p))
    y_tm = s5_outputs(u_tm, kx.reshape(g * h, CHUNK * h), clt.reshape(2, g * p, CHUNK * h), x)
    return y_tm.transpose(0, 3, 1, 2, 4).reshape(bsz, seqlen, width)


def kernel(x, c, w_ada, b_ada, g_mix, w_in, a_re, a_im, log_dt, b_re, b_im, c_re, c_im, d_skip,
           w_glu, q_gain, k_gain, g_ssm_out, g_attn_out, w_out, g_ffn, w_gate, w_up, w_down):
    bsz, seqlen, d = x.shape
    depth = w_ada.shape[0]
    width = w_glu.shape[1]
    h2 = x.reshape(bsz * seqlen, d)
    for l in range(depth):
        mod3 = ada_mod(c, w_ada[l], b_ada[l]).reshape(bsz, 6, d)
        proj = in_proj(h2, mod3, g_mix[l], w_in[l].astype(BF16), q_gain[l], k_gain[l], seqlen)
        proj3 = proj.reshape(bsz, seqlen, 4 * width)
        y = s5_mixer_chunked(proj3[:, :, :width], a_re[l], a_im[l], log_dt[l], b_re[l], b_im[l],
                             c_re[l], c_im[l], d_skip[l])
        att = stick_attention(proj3, width)
        h2 = out_proj(y.reshape(bsz * seqlen, width), att.reshape(bsz * seqlen, width), h2, mod3,
                      w_glu[l].astype(BF16), g_ssm_out[l], g_attn_out[l], w_out[l].astype(BF16), seqlen)
        h2 = ffn(h2, mod3, g_ffn[l], w_gate[l].astype(BF16), w_up[l].astype(BF16),
                 w_down[l].astype(BF16), seqlen)
    return h2.reshape(bsz, seqlen, d)
```

```python
import functools

import jax
import jax.numpy as jnp
from jax import lax
from jax.experimental import pallas as pl
from jax.experimental.pallas import tpu as pltpu

F32 = jnp.float32
BF16 = jnp.bfloat16
EPS = 1e-6
HEAD_DIM = 64
CHUNK = 16
LANE_TILE = 256
VMEM_LIMIT = 56 * 1024 * 1024
Q_SCALE = HEAD_DIM ** -0.5 * 1.4426950408889634


def _cparams(n_axes, vmem=VMEM_LIMIT):
    return pltpu.CompilerParams(dimension_semantics=("arbitrary",) * n_axes,
                                vmem_limit_bytes=vmem)


def _rms(x):
    return x * lax.rsqrt(jnp.mean(x * x, axis=-1, keepdims=True) + EPS)


def _ada_kernel(c_ref, w_ref, b_ref, o_ref):
    d, tn = w_ref.shape
    rows = 256
    for b in range(c_ref.shape[0]):
        acc = jnp.zeros((8, tn), F32)
        for kc in range(d // rows):
            cond = jax.nn.silu(c_ref[b, kc * rows:(kc + 1) * rows, :])
            blk = cond * w_ref[kc * rows:(kc + 1) * rows, :]
            acc = acc + jnp.sum(blk.reshape(rows // 8, 8, tn), axis=0)
        o_ref[b:b + 1, :] = jnp.sum(acc, axis=0, keepdims=True) + b_ref[...]


def ada_mod(c, w_ada, b_ada, tn=1024):
    bsz, d = c.shape
    n = w_ada.shape[1]
    return pl.pallas_call(
        _ada_kernel,
        grid=(n // tn,),
        in_specs=[pl.BlockSpec((bsz, d, 1), lambda j: (0, 0, 0)),
                  pl.BlockSpec((d, tn), lambda j: (0, j)),
                  pl.BlockSpec((1, tn), lambda j: (0, j))],
        out_specs=pl.BlockSpec((bsz, tn), lambda j: (0, j)),
        out_shape=jax.ShapeDtypeStruct((bsz, n), F32),
        compiler_params=_cparams(1),
        name="ada",
    )(c.reshape(bsz, d, 1), w_ada, b_ada.reshape(1, n))


def _head_rmsnorm(r, ones_ref):
    outs = []
    for s in range(r.shape[1] // LANE_TILE):
        rs = r[:, s * LANE_TILE:(s + 1) * LANE_TILE]
        ss = jnp.dot((rs * rs).astype(BF16), ones_ref[...], preferred_element_type=F32)
        outs.append(rs * lax.rsqrt(ss * (1.0 / HEAD_DIM) + EPS))
    return jnp.concatenate(outs, axis=1)


def _proj_kernel(x_ref, mod_ref, g_ref, w_ref, qg_ref, kg_ref, ones_ref, o_ref, xm_ref):
    j = pl.program_id(1)

    @pl.when(j == 0)
    def _():
        xn = _rms(x_ref[...]) * g_ref[...]
        xm_ref[...] = (xn * (1.0 + mod_ref[0, 1:2, :]) + mod_ref[0, 0:1, :]).astype(BF16)

    r = jnp.dot(xm_ref[...], w_ref[...], preferred_element_type=F32)

    @pl.when(j == 1)
    def _():
        o_ref[...] = ((_head_rmsnorm(r, ones_ref) * qg_ref[...]) * Q_SCALE).astype(BF16)

    @pl.when(j == 2)
    def _():
        o_ref[...] = (_head_rmsnorm(r, ones_ref) * kg_ref[...]).astype(BF16)

    @pl.when((j == 0) | (j == 3))
    def _():
        o_ref[...] = r.astype(BF16)


def in_proj(x2, mod3, g_mix, w_in_bf, q_gain, k_gain, seqlen, tm=512):
    t, d = x2.shape
    n = w_in_bf.shape[1]
    tn = n // 4
    reps = tn // HEAD_DIM
    qg = jnp.tile(q_gain.reshape(1, HEAD_DIM), (1, reps))
    kg = jnp.tile(k_gain.reshape(1, HEAD_DIM), (1, reps))
    idx = jnp.arange(LANE_TILE) // HEAD_DIM
    ones = (idx[:, None] == idx[None, :]).astype(BF16)
    return pl.pallas_call(
        _proj_kernel,
        grid=(t // tm, 4),
        in_specs=[pl.BlockSpec((tm, d), lambda i, j: (i, 0)),
                  pl.BlockSpec((1, 6, d), lambda i, j: ((i * tm) // seqlen, 0, 0)),
                  pl.BlockSpec((1, d), lambda i, j: (0, 0)),
                  pl.BlockSpec((d, tn), lambda i, j: (0, j)),
                  pl.BlockSpec((1, tn), lambda i, j: (0, 0)),
                  pl.BlockSpec((1, tn), lambda i, j: (0, 0)),
                  pl.BlockSpec((LANE_TILE, LANE_TILE), lambda i, j: (0, 0))],
        out_specs=pl.BlockSpec((tm, tn), lambda i, j: (i, j)),
        out_shape=jax.ShapeDtypeStruct((t, n), BF16),
        scratch_shapes=[pltpu.VMEM((tm, d), BF16)],
        compiler_params=_cparams(2),
        name="proj",
    )(x2, mod3, g_mix.reshape(1, d), w_in_bf, qg, kg, ones)


def _zoh(ar, ai, dt):
    mag = jnp.exp(ar * dt)
    th = ai * dt
    lbr, lbi = mag * jnp.cos(th), mag * jnp.sin(th)
    den = ar * ar + ai * ai
    nr, ni = lbr - 1.0, lbi
    return lbr, lbi, (nr * ar + ni * ai) / den, (ni * ar - nr * ai) / den


def _s5prep_kernel(ar_ref, ai_ref, ldt_ref, btr_ref, bti_ref, cr_ref, ci_ref, d_ref,
                   arc_ref, aic_ref, ctr_ref, cti_ref,
                   bl_ref, clt_ref, kx_ref, a_ref):
    dt = jnp.exp(ldt_ref[...])
    lbr, lbi, kr, ki = _zoh(ar_ref[...], ai_ref[...], dt)
    btr, bti = btr_ref[...], bti_ref[...]
    bbr = kr * btr - ki * bti
    bbi = kr * bti + ki * btr
    c_re, c_im = cr_ref[...], ci_ref[...]
    h = c_re.shape[1]
    eye = (lax.broadcasted_iota(jnp.int32, (1, h, h), 1) ==
           lax.broadcasted_iota(jnp.int32, (1, h, h), 2))
    dn = (((2,), (2,)), ((0,), (0,)))
    pr, pi = jnp.ones_like(lbr), jnp.zeros_like(lbr)
    for j in range(CHUNK):
        blr = pr * bbr - pi * bbi
        bli = pr * bbi + pi * bbr
        bl_ref[j, 0] = blr
        bl_ref[j, 1] = bli
        kx = (lax.dot_general(blr, c_re, dn, precision=lax.Precision.HIGHEST,
                              preferred_element_type=F32)
              - lax.dot_general(bli, c_im, dn, precision=lax.Precision.HIGHEST,
                                preferred_element_type=F32))
        if j == 0:
            kx = kx + jnp.where(eye, d_ref[...], 0.0)
        kx_ref[:, :, j * h:(j + 1) * h] = kx
        pr, pi = pr * lbr - pi * lbi, pr * lbi + pi * lbr
    a_ref[0] = pr
    a_ref[1] = pi
    lcr, lci, _, _ = _zoh(arc_ref[...], aic_ref[...], dt)
    ctr, cti = ctr_ref[...], cti_ref[...]
    qr, qi = lcr, lci
    for t in range(CHUNK):
        clt_ref[0, :, :, t * h:(t + 1) * h] = ctr * qr - cti * qi
        clt_ref[1, :, :, t * h:(t + 1) * h] = -(ctr * qi + cti * qr)
        qr, qi = qr * lcr - qi * lci, qr * lci + qi * lcr


def s5_prep(a_re, a_im, log_dt, b_re, b_im, c_re, c_im, d_skip, gb=8):
    g, p = a_re.shape
    h = c_re.shape[1]
    vec = pl.BlockSpec((gb, 1, p), lambda i: (i, 0, 0))
    col = pl.BlockSpec((gb, p, 1), lambda i: (i, 0, 0))
    mat = pl.BlockSpec((gb, h, p), lambda i: (i, 0, 0))
    matt = pl.BlockSpec((gb, p, h), lambda i: (i, 0, 0))
    return pl.pallas_call(
        _s5prep_kernel,
        grid=(g // gb,),
        in_specs=[vec, vec, pl.BlockSpec((gb, 1, 1), lambda i: (i, 0, 0)),
                  mat, mat, mat, mat, pl.BlockSpec((gb, 1, h), lambda i: (i, 0, 0)),
                  col, col, matt, matt],
        out_specs=[pl.BlockSpec((CHUNK, 2, gb, h, p), lambda i: (0, 0, i, 0, 0)),
                   pl.BlockSpec((2, gb, p, CHUNK * h), lambda i: (0, i, 0, 0)),
                   pl.BlockSpec((gb, h, CHUNK * h), lambda i: (i, 0, 0)),
                   pl.BlockSpec((2, gb, 1, p), lambda i: (0, i, 0, 0))],
        out_shape=[jax.ShapeDtypeStruct((CHUNK, 2, g, h, p), F32),
                   jax.ShapeDtypeStruct((2, g, p, CHUNK * h), F32),
                   jax.ShapeDtypeStruct((g, h, CHUNK * h), F32),
                   jax.ShapeDtypeStruct((2, g, 1, p), F32)],
        compiler_params=_cparams(1),
        name="s5prep",
    )(a_re.reshape(g, 1, p), a_im.reshape(g, 1, p), log_dt.reshape(g, 1, 1),
      b_re.transpose(0, 2, 1), b_im.transpose(0, 2, 1), c_re, c_im, d_skip.reshape(g, 1, h),
      a_re.reshape(g, p, 1), a_im.reshape(g, p, 1), c_re.transpose(0, 2, 1), c_im.transpose(0, 2, 1))


def _group_mask(shape, row_div, col_div):
    return (lax.broadcasted_iota(jnp.int32, shape, 0) // row_div ==
            lax.broadcasted_iota(jnp.int32, shape, 1) // col_div)


def _s5in_kernel(u_ref, bl_ref, e_ref, o_ref, w_ref):
    p, wid = e_ref.shape
    h = LANE_TILE // (wid // p)
    mask = _group_mask((LANE_TILE, wid), h, p)
    for s in range(CHUNK):
        ex = jnp.dot(bl_ref[CHUNK - 1 - s, 0].astype(BF16), e_ref[...], preferred_element_type=F32)
        w_ref[s * LANE_TILE:(s + 1) * LANE_TILE, :] = jnp.where(mask, ex, 0.0).astype(BF16)
    for b in range(u_ref.shape[0]):
        ucat = jnp.concatenate([u_ref[b, s, 0] for s in range(CHUNK)], axis=1)
        o_ref[b, 0] = jnp.dot(ucat, w_ref[...], preferred_element_type=F32)


def s5_chunk_inputs(u_tm, bl2, h):
    bsz, _, ngt, nc, _ = u_tm.shape
    p = bl2.shape[3]
    wid = (LANE_TILE // h) * p
    e = (jnp.arange(wid)[None, :] % p == jnp.arange(p)[:, None]).astype(BF16)
    return pl.pallas_call(
        _s5in_kernel,
        grid=(ngt, 2),
        in_specs=[pl.BlockSpec((bsz, CHUNK, 1, nc, LANE_TILE), lambda gt, ri: (0, 0, gt, 0, 0)),
                  pl.BlockSpec((CHUNK, 1, LANE_TILE, p), lambda gt, ri: (0, ri, gt, 0)),
                  pl.BlockSpec((p, wid), lambda gt, ri: (0, 0))],
        out_specs=pl.BlockSpec((bsz, 1, nc, wid), lambda gt, ri: (0, ri, 0, gt)),
        out_shape=jax.ShapeDtypeStruct((bsz, 2, nc, ngt * wid), F32),
        scratch_shapes=[pltpu.VMEM((CHUNK * LANE_TILE, wid), BF16)],
        compiler_params=_cparams(2),
        name="s5in",
    )(u_tm, bl2, e)


def _s5scan_kernel(s_ref, a_ref, x_ref):
    nc, w = s_ref.shape[2], s_ref.shape[3]
    a1r, a1i = a_ref[0], a_ref[1]

    def cmul(xr, xi, yr, yi):
        return xr * yr - xi * yi, xr * yi + xi * yr

    row = lax.broadcasted_iota(jnp.int32, (8, w), 0)
    pows = [(a1r, a1i)]
    for _ in range(7):
        pows.append(cmul(pows[-1][0], pows[-1][1], a1r, a1i))
    steps = []
    for k in (1, 2, 4):
        steps.append((k, jnp.where(row >= k, pows[k - 1][0], 0.0), jnp.where(row >= k, pows[k - 1][1], 0.0)))
    pcr = jnp.zeros((8, w), F32)
    pci = jnp.zeros((8, w), F32)
    for r in range(8):
        pcr = jnp.where(row == r, pows[r][0], pcr)
        pci = jnp.where(row == r, pows[r][1], pci)

    def body(blk, carry):
        cr, ci = carry
        off = pl.multiple_of(blk * 8, 8)
        xr = s_ref[0, 0, pl.ds(off, 8), :]
        xi = s_ref[0, 1, pl.ds(off, 8), :]
        for k, kr, ki in steps:
            sr, si = pltpu.roll(xr, k, 0), pltpu.roll(xi, k, 0)
            dr, di = cmul(sr, si, kr, ki)
            xr, xi = xr + dr, xi + di
        dr, di = cmul(jnp.broadcast_to(cr, (8, w)), jnp.broadcast_to(ci, (8, w)), pcr, pci)
        xr, xi = xr + dr, xi + di
        x_ref[0, 0, pl.ds(off, 8), :] = jnp.where(row == 0, cr, pltpu.roll(xr, 1, 0))
        x_ref[0, 1, pl.ds(off, 8), :] = jnp.where(row == 0, ci, pltpu.roll(xi, 1, 0))
        return xr[7:8, :], xi[7:8, :]

    zero = jnp.zeros((1, w), F32)
    lax.fori_loop(0, nc // 8, body, (zero, zero))


def s5_chunk_scan(s, a, w=1024):
    bsz, _, nc, tot = s.shape
    blk = pl.BlockSpec((1, 2, nc, w), lambda b, l: (b, 0, 0, l))
    return pl.pallas_call(
        _s5scan_kernel,
        grid=(bsz, tot // w),
        in_specs=[blk, pl.BlockSpec((2, 1, w), lambda b, l: (0, 0, l))],
        out_specs=blk,
        out_shape=jax.ShapeDtypeStruct(s.shape, F32),
        compiler_params=_cparams(2),
        name="s5scan",
    )(s, a)


def _s5out_kernel(u_ref, kx_ref, clt_ref, x_ref, e_ref, o_ref, acc_ref):
    bsz, _, _, nc, _ = u_ref.shape
    wid = x_ref.shape[3]
    h = LANE_TILE // CHUNK
    p = wid // (LANE_TILE // h)
    mask_t = _group_mask((LANE_TILE, LANE_TILE), h, h)
    mask_c = _group_mask((wid, LANE_TILE), p, h)
    kx = kx_ref[...].astype(BF16)
    for j in range(CHUNK):
        tj = jnp.where(mask_t, jnp.dot(kx, e_ref[j], preferred_element_type=F32), 0.0).astype(BF16)
        for b in range(bsz):
            lhs = u_ref[b, 0:CHUNK - j, 0].reshape((CHUNK - j) * nc, LANE_TILE)
            r = jnp.dot(lhs, tj, preferred_element_type=F32).reshape(CHUNK - j, nc, LANE_TILE)
            if j == 0:
                acc_ref[b] = r
            else:
                acc_ref[b, j:CHUNK] += r
    clr, cli = clt_ref[0].astype(BF16), clt_ref[1].astype(BF16)
    xr = x_ref[:, 0].reshape(bsz * nc, wid).astype(BF16)
    xi = x_ref[:, 1].reshape(bsz * nc, wid).astype(BF16)
    for t in range(CHUNK):
        wr = jnp.where(mask_c, jnp.dot(clr, e_ref[t], preferred_element_type=F32), 0.0).astype(BF16)
        wi = jnp.where(mask_c, jnp.dot(cli, e_ref[t], preferred_element_type=F32), 0.0).astype(BF16)
        y = (jnp.dot(xr, wr, preferred_element_type=F32) + jnp.dot(xi, wi, preferred_element_type=F32))
        for b in range(bsz):
            o_ref[b, t, 0] = (acc_ref[b, t] + y[b * nc:(b + 1) * nc]).astype(BF16)


def s5_outputs(u_tm, kx2, clt2, x):
    bsz, _, ngt, nc, _ = u_tm.shape
    wid = x.shape[3] // ngt
    h = LANE_TILE // CHUNK
    r = jnp.arange(LANE_TILE)
    e = ((r[None, :, None] // h == jnp.arange(CHUNK)[:, None, None])
         & (r[None, :, None] % h == r[None, None, :] % h)).astype(BF16)
    tile = pl.BlockSpec((bsz, CHUNK, 1, nc, LANE_TILE), lambda gt: (0, 0, gt, 0, 0))
    return pl.pallas_call(
        _s5out_kernel,
        grid=(ngt,),
        in_specs=[tile,
                  pl.BlockSpec((LANE_TILE, LANE_TILE), lambda gt: (gt, 0)),
                  pl.BlockSpec((2, wid, LANE_TILE), lambda gt: (0, gt, 0)),
                  pl.BlockSpec((bsz, 2, nc, wid), lambda gt: (0, 0, 0, gt)),
                  pl.BlockSpec((CHUNK, LANE_TILE, LANE_TILE), lambda gt: (0, 0, 0))],
        out_specs=tile,
        out_shape=jax.ShapeDtypeStruct(u_tm.shape, BF16),
        scratch_shapes=[pltpu.VMEM((bsz, CHUNK, nc, LANE_TILE), F32)],
        compiler_params=_cparams(1),
        name="s5out",
    )(u_tm, kx2, clt2, x, e)


def _attn_kernel(q_ref, k_ref, v_ref, tri_ref, o_ref, vcat_ref, *, blk, nh, unroll):
    qi = pl.program_id(2)
    q = q_ref[0]
    lane = lax.broadcasted_iota(jnp.int32, q.shape, 1)
    heads = [(lane >= HEAD_DIM * h) & (lane < HEAD_DIM * (h + 1)) for h in range(nh)]
    zero = jnp.zeros_like(q)

    @pl.when(qi == 0)
    def _():
        def fill(j, _):
            vb = v_ref[0, pl.ds(pl.multiple_of(j * blk, blk), blk), :]
            vcat_ref[j] = jnp.concatenate([jnp.where(m, vb, zero) for m in heads], axis=0)
            return 0

        lax.fori_loop(0, vcat_ref.shape[0], fill, 0)

    qs = jnp.concatenate([jnp.where(m, q, zero) for m in heads], axis=0)
    row = lax.broadcasted_iota(jnp.int32, (blk, blk), 0)
    col = lax.broadcasted_iota(jnp.int32, (blk, blk), 1)
    past = jnp.concatenate([col < row] * nh, axis=0)
    nt = (((1,), (1,)), ((), ()))

    def block(kj, carry, acc, diag):
        kb = k_ref[0, pl.ds(pl.multiple_of(kj * blk, blk), blk), :]
        z = lax.dot_general(qs, kb, nt, preferred_element_type=F32)
        sp = jnp.maximum(z, 0.0) + jnp.log2(1.0 + jnp.exp2(-jnp.abs(z)))
        if diag:
            sp = jnp.where(past, sp, 0.0)
        cs = jnp.dot(sp.astype(BF16), tri_ref[...], preferred_element_type=F32)
        w = jnp.exp2(((z - sp) - cs) - carry)
        if diag:
            w = jnp.where(past, w, 0.0)
        wb = w.astype(BF16)
        wcat = jnp.concatenate([wb[h * blk:(h + 1) * blk] for h in range(nh)], axis=1)
        acc = acc + jnp.dot(wcat, vcat_ref[kj], preferred_element_type=F32)
        return carry + (cs[:, 0:1] + sp[:, 0:1]), acc

    carry, acc = block(qi, jnp.zeros((nh * blk, 1), F32), jnp.zeros(q.shape, F32), True)

    c = (carry, acc)
    left = qi
    n = unroll
    while n >= 1:
        def group(i, c, n=n, left=left):
            for r in range(n):
                c = block(left - 1 - r - n * i, c[0], c[1], False)
            return c

        c = lax.fori_loop(0, left // n, group, c)
        left = left % n
        n //= 2
    o_ref[0] = c[1].astype(BF16)


def stick_attention(proj3, width, blk=256, nh=4, unroll=4):
    bsz, seqlen, _ = proj3.shape
    wl = nh * HEAD_DIM
    ngrp = width // wl
    ii = jnp.arange(blk)
    tri = (ii[:, None] > ii[None, :]).astype(BF16)
    return pl.pallas_call(
        functools.partial(_attn_kernel, blk=blk, nh=nh, unroll=unroll),
        grid=(bsz, ngrp, seqlen // blk),
        in_specs=[pl.BlockSpec((1, blk, wl), lambda b, h, i: (b, i, ngrp + h)),
                  pl.BlockSpec((1, seqlen, wl), lambda b, h, i: (b, 0, 2 * ngrp + h)),
                  pl.BlockSpec((1, seqlen, wl), lambda b, h, i: (b, 0, 3 * ngrp + h)),
                  pl.BlockSpec((blk, blk), lambda b, h, i: (0, 0))],
        out_specs=pl.BlockSpec((1, blk, wl), lambda b, h, i: (b, i, h)),
        out_shape=jax.ShapeDtypeStruct((bsz, seqlen, width), BF16),
        scratch_shapes=[pltpu.VMEM((seqlen // blk, nh * blk, wl), BF16)],
        compiler_params=_cparams(3),
        name="attn",
    )(proj3, proj3, proj3, tri)


def _out_kernel(y_ref, a_ref, x_ref, mod_ref, wglu_ref, gs_ref, ga_ref, w0_ref, w1_ref, o_ref):
    y = jax.nn.gelu(y_ref[...].astype(F32))
    z = jnp.dot(y.astype(BF16), wglu_ref[...], preferred_element_type=F32)
    ys = y * jax.nn.sigmoid(z)
    ysn = (_rms(ys) * gs_ref[...]).astype(BF16)
    an = (_rms(a_ref[...].astype(F32)) * ga_ref[...]).astype(BF16)
    o = (jnp.dot(ysn, w0_ref[...], preferred_element_type=F32)
         + jnp.dot(an, w1_ref[...], preferred_element_type=F32))
    o_ref[...] = x_ref[...] + mod_ref[0, 2:3, :] * o


def out_proj(y2, a2, x2, mod3, w_glu_bf, g_ssm, g_attn, w_out_bf, seqlen, tm=512):
    t, d = x2.shape
    ws = y2.shape[1]
    wa = a2.shape[1]
    return pl.pallas_call(
        _out_kernel,
        grid=(t // tm,),
        in_specs=[pl.BlockSpec((tm, ws), lambda i: (i, 0)),
                  pl.BlockSpec((tm, wa), lambda i: (i, 0)),
                  pl.BlockSpec((tm, d), lambda i: (i, 0)),
                  pl.BlockSpec((1, 6, d), lambda i: ((i * tm) // seqlen, 0, 0)),
                  pl.BlockSpec((ws, ws), lambda i: (0, 0)),
                  pl.BlockSpec((1, ws), lambda i: (0, 0)),
                  pl.BlockSpec((1, wa), lambda i: (0, 0)),
                  pl.BlockSpec((ws, d), lambda i: (0, 0)),
                  pl.BlockSpec((wa, d), lambda i: (1, 0))],
        out_specs=pl.BlockSpec((tm, d), lambda i: (i, 0)),
        out_shape=jax.ShapeDtypeStruct((t, d), F32),
        compiler_params=_cparams(1),
        name="outproj",
    )(y2, a2, x2, mod3, w_glu_bf, g_ssm.reshape(1, ws), g_attn.reshape(1, wa), w_out_bf, w_out_bf)


def _ffn_kernel(h_ref, mod_ref, g_ref, wg_ref, wu_ref, wd_ref, o_ref, xf_ref, acc_ref):
    j = pl.program_id(1)

    @pl.when(j == 0)
    def _():
        xn = _rms(h_ref[...]) * g_ref[...]
        xf_ref[...] = (xn * (1.0 + mod_ref[0, 4:5, :]) + mod_ref[0, 3:4, :]).astype(BF16)
        acc_ref[...] = jnp.zeros_like(acc_ref)

    xf = xf_ref[...]
    g = jnp.dot(xf, wg_ref[...], preferred_element_type=F32)
    u = jnp.dot(xf, wu_ref[...], preferred_element_type=F32)
    hm = (jax.nn.silu(g) * u).astype(BF16)
    acc_ref[...] += jnp.dot(hm, wd_ref[...], preferred_element_type=F32)

    @pl.when(j == pl.num_programs(1) - 1)
    def _():
        o_ref[...] = h_ref[...] + mod_ref[0, 5:6, :] * acc_ref[...]


def ffn(h2, mod3, g_ffn, wg_bf, wu_bf, wd_bf, seqlen, tm=512, tf=512):
    t, d = h2.shape
    dff = wg_bf.shape[1]
    return pl.pallas_call(
        _ffn_kernel,
        grid=(t // tm, dff // tf),
        in_specs=[pl.BlockSpec((tm, d), lambda i, j: (i, 0)),
                  pl.BlockSpec((1, 6, d), lambda i, j: ((i * tm) // seqlen, 0, 0)),
                  pl.BlockSpec((1, d), lambda i, j: (0, 0)),
                  pl.BlockSpec((d, tf), lambda i, j: (0, j)),
                  pl.BlockSpec((d, tf), lambda i, j: (0, j)),
                  pl.BlockSpec((tf, d), lambda i, j: (j, 0))],
        out_specs=pl.BlockSpec((tm, d), lambda i, j: (i, 0)),
        out_shape=jax.ShapeDtypeStruct((t, d), F32),
        scratch_shapes=[pltpu.VMEM((tm, d), BF16), pltpu.VMEM((tm, d), F32)],
        compiler_params=_cparams(2),
        name="ffn",
    )(h2, mod3, g_ffn.reshape(1, d), wg_bf, wu_bf, wd_bf)


def s5_mixer_chunked(u3, a_re, a_im, log_dt, b_re, b_im, c_re, c_im, d_skip):
    bsz, seqlen, width = u3.shape
    g, p = a_re.shape
    h = c_re.shape[1]
    nc = seqlen // CHUNK
    ngt = width // LANE_TILE
    assert h * CHUNK == LANE_TILE and g * h == width
    bl, clt, kx, a = s5_prep(a_re, a_im, log_dt, b_re, b_im, c_re, c_im, d_skip)
    u_tm = u3.reshape(bsz, nc, CHUNK, ngt, LANE_TILE).transpose(0, 2, 3, 1, 4)
    s = s5_chunk_inputs(u_tm, bl.reshape(CHUNK, 2, g * h, p), h)
    x = s5_chunk_scan(s, a.reshape(2, 1, g * p))
    y_tm = s5_outputs(u_tm, kx.reshape(g * h, CHUNK * h), clt.reshape(2, g * p, CHUNK * h), x)
    return y_tm.transpose(0, 3, 1, 2, 4).reshape(bsz, seqlen, width)


def kernel(x, c, w_ada, b_ada, g_mix, w_in, a_re, a_im, log_dt, b_re, b_im, c_re, c_im, d_skip,
           w_glu, q_gain, k_gain, g_ssm_out, g_attn_out, w_out, g_ffn, w_gate, w_up, w_down):
    bsz, seqlen, d = x.shape
    depth = w_ada.shape[0]
    width = w_glu.shape[1]
    h2 = x.reshape(bsz * seqlen, d)
    for l in range(depth):
        mod3 = ada_mod(c, w_ada[l], b_ada[l]).reshape(bsz, 6, d)
        proj = in_proj(h2, mod3, g_mix[l], w_in[l].astype(BF16), q_gain[l], k_gain[l], seqlen)
        proj3 = proj.reshape(bsz, seqlen, 4 * width)
        y = s5_mixer_chunked(proj3[:, :, :width], a_re[l], a_im[l], log_dt[l], b_re[l], b_im[l],
                             c_re[l], c_im[l], d_skip[l])
        att = stick_attention(proj3, width)
        h2 = out_proj(y.reshape(bsz * seqlen, width), att.reshape(bsz * seqlen, width), h2, mod3,
                      w_glu[l].astype(BF16), g_ssm_out[l], g_attn_out[l], w_out[l].astype(BF16), seqlen)
        h2 = ffn(h2, mod3, g_ffn[l], w_gate[l].astype(BF16), w_up[l].astype(BF16),
                 w_down[l].astype(BF16), seqlen)
    return h2.reshape(bsz, seqlen, d)
```

```python
import functools

import jax
import jax.numpy as jnp
from jax import lax
from jax.experimental import pallas as pl
from jax.experimental.pallas import tpu as pltpu

F32 = jnp.float32
BF16 = jnp.bfloat16
EPS = 1e-6
HEAD_DIM = 64
CHUNK = 16
LANE_TILE = 256
VMEM_LIMIT = 56 * 1024 * 1024
Q_SCALE = HEAD_DIM ** -0.5 * 1.4426950408889634


def _cparams(n_axes, vmem=VMEM_LIMIT):
    return pltpu.CompilerParams(dimension_semantics=("arbitrary",) * n_axes,
                                vmem_limit_bytes=vmem)


def _rms(x):
    return x * lax.rsqrt(jnp.mean(x * x, axis=-1, keepdims=True) + EPS)


def _ada_kernel(c_ref, w_ref, b_ref, o_ref):
    d, tn = w_ref.shape
    rows = 256
    for b in range(c_ref.shape[0]):
        acc = jnp.zeros((8, tn), F32)
        for kc in range(d // rows):
            cond = jax.nn.silu(c_ref[b, kc * rows:(kc + 1) * rows, :])
            blk = cond * w_ref[kc * rows:(kc + 1) * rows, :]
            acc = acc + jnp.sum(blk.reshape(rows // 8, 8, tn), axis=0)
        o_ref[b:b + 1, :] = jnp.sum(acc, axis=0, keepdims=True) + b_ref[...]


def ada_mod(c, w_ada, b_ada, tn=1024):
    bsz, d = c.shape
    n = w_ada.shape[1]
    return pl.pallas_call(
        _ada_kernel,
        grid=(n // tn,),
        in_specs=[pl.BlockSpec((bsz, d, 1), lambda j: (0, 0, 0)),
                  pl.BlockSpec((d, tn), lambda j: (0, j)),
                  pl.BlockSpec((1, tn), lambda j: (0, j))],
        out_specs=pl.BlockSpec((bsz, tn), lambda j: (0, j)),
        out_shape=jax.ShapeDtypeStruct((bsz, n), F32),
        compiler_params=_cparams(1),
        name="ada",
    )(c.reshape(bsz, d, 1), w_ada, b_ada.reshape(1, n))


def _head_rmsnorm(r, ones_ref):
    outs = []
    for s in range(r.shape[1] // LANE_TILE):
        rs = r[:, s * LANE_TILE:(s + 1) * LANE_TILE]
        ss = jnp.dot((rs * rs).astype(BF16), ones_ref[...], preferred_element_type=F32)
        outs.append(rs * lax.rsqrt(ss * (1.0 / HEAD_DIM) + EPS))
    return jnp.concatenate(outs, axis=1)


def _proj_kernel(x_ref, mod_ref, g_ref, w_ref, qg_ref, kg_ref, ones_ref, o_ref, *, rows):
    tm = x_ref.shape[0]
    tn = w_ref.shape[1] // 4
    for r0 in range(0, tm, rows):
        xn = _rms(x_ref[r0:r0 + rows, :]) * g_ref[...]
        xm = (xn * (1.0 + mod_ref[0, 1:2, :]) + mod_ref[0, 0:1, :]).astype(BF16)
        for j in range(4):
            r = jnp.dot(xm, w_ref[:, j * tn:(j + 1) * tn], preferred_element_type=F32)
            if j == 1:
                r = (_head_rmsnorm(r, ones_ref) * qg_ref[...]) * Q_SCALE
            elif j == 2:
                r = _head_rmsnorm(r, ones_ref) * kg_ref[...]
            o_ref[r0:r0 + rows, j * tn:(j + 1) * tn] = r.astype(BF16)


def in_proj(x2, mod3, g_mix, w_in_bf, q_gain, k_gain, seqlen, tm=512, rows=256):
    t, d = x2.shape
    n = w_in_bf.shape[1]
    tn = n // 4
    reps = tn // HEAD_DIM
    qg = jnp.tile(q_gain.reshape(1, HEAD_DIM), (1, reps))
    kg = jnp.tile(k_gain.reshape(1, HEAD_DIM), (1, reps))
    idx = jnp.arange(LANE_TILE) // HEAD_DIM
    ones = (idx[:, None] == idx[None, :]).astype(BF16)
    const = lambda shape: pl.BlockSpec(shape, lambda i: (0, 0), pipeline_mode=pl.Buffered(1))
    return pl.pallas_call(
        functools.partial(_proj_kernel, rows=rows),
        grid=(t // tm,),
        in_specs=[pl.BlockSpec((tm, d), lambda i: (i, 0)),
                  pl.BlockSpec((1, 6, d), lambda i: ((i * tm) // seqlen, 0, 0)),
                  const((1, d)), const((d, n)), const((1, tn)), const((1, tn)),
                  const((LANE_TILE, LANE_TILE))],
        out_specs=pl.BlockSpec((tm, n), lambda i: (i, 0)),
        out_shape=jax.ShapeDtypeStruct((t, n), BF16),
        compiler_params=_cparams(1),
        name="proj",
    )(x2, mod3, g_mix.reshape(1, d), w_in_bf, qg, kg, ones)


def _zoh(ar, ai, dt):
    mag = jnp.exp(ar * dt)
    th = ai * dt
    lbr, lbi = mag * jnp.cos(th), mag * jnp.sin(th)
    den = ar * ar + ai * ai
    nr, ni = lbr - 1.0, lbi
    return lbr, lbi, (nr * ar + ni * ai) / den, (ni * ar - nr * ai) / den


def _s5prep_kernel(ar_ref, ai_ref, ldt_ref, btr_ref, bti_ref, cr_ref, ci_ref, d_ref,
                   bl_ref, cl_ref, kx_ref, a_ref):
    dt = jnp.exp(ldt_ref[...])
    lbr, lbi, kr, ki = _zoh(ar_ref[...], ai_ref[...], dt)
    btr, bti = btr_ref[...], bti_ref[...]
    bbr = kr * btr - ki * bti
    bbi = kr * bti + ki * btr
    c_re, c_im = cr_ref[...], ci_ref[...]
    h = c_re.shape[1]
    eye = (lax.broadcasted_iota(jnp.int32, (1, h, h), 1) ==
           lax.broadcasted_iota(jnp.int32, (1, h, h), 2))
    dn = (((2,), (2,)), ((0,), (0,)))
    pr, pi = jnp.ones_like(lbr), jnp.zeros_like(lbr)
    for j in range(CHUNK):
        blr = pr * bbr - pi * bbi
        bli = pr * bbi + pi * bbr
        bl_ref[j, 0] = blr
        bl_ref[j, 1] = bli
        kx = (lax.dot_general(blr, c_re, dn, precision=lax.Precision.HIGHEST,
                              preferred_element_type=F32)
              - lax.dot_general(bli, c_im, dn, precision=lax.Precision.HIGHEST,
                                preferred_element_type=F32))
        if j == 0:
            kx = kx + jnp.where(eye, d_ref[...], 0.0)
        kx_ref[:, :, j * h:(j + 1) * h] = kx
        pr, pi = pr * lbr - pi * lbi, pr * lbi + pi * lbr
        cl_ref[j, 0] = c_re * pr - c_im * pi
        cl_ref[j, 1] = -(c_re * pi + c_im * pr)
    a_ref[0] = pr
    a_ref[1] = pi


def s5_prep(a_re, a_im, log_dt, b_re, b_im, c_re, c_im, d_skip, gb=8):
    g, p = a_re.shape
    h = c_re.shape[1]
    vec = pl.BlockSpec((gb, 1, p), lambda i: (i, 0, 0))
    mat = pl.BlockSpec((gb, h, p), lambda i: (i, 0, 0))
    lag = pl.BlockSpec((CHUNK, 2, gb, h, p), lambda i: (0, 0, i, 0, 0))
    lag_shape = jax.ShapeDtypeStruct((CHUNK, 2, g, h, p), F32)
    return pl.pallas_call(
        _s5prep_kernel,
        grid=(g // gb,),
        in_specs=[vec, vec, pl.BlockSpec((gb, 1, 1), lambda i: (i, 0, 0)),
                  mat, mat, mat, mat, pl.BlockSpec((gb, 1, h), lambda i: (i, 0, 0))],
        out_specs=[lag, lag,
                   pl.BlockSpec((gb, h, CHUNK * h), lambda i: (i, 0, 0)),
                   pl.BlockSpec((2, gb, 1, p), lambda i: (0, i, 0, 0))],
        out_shape=[lag_shape, lag_shape,
                   jax.ShapeDtypeStruct((g, h, CHUNK * h), F32),
                   jax.ShapeDtypeStruct((2, g, 1, p), F32)],
        compiler_params=_cparams(1),
        name="s5prep",
    )(a_re.reshape(g, 1, p), a_im.reshape(g, 1, p), log_dt.reshape(g, 1, 1),
      b_re.transpose(0, 2, 1), b_im.transpose(0, 2, 1), c_re, c_im, d_skip.reshape(g, 1, h))


def _group_mask(shape, row_div, col_div):
    return (lax.broadcasted_iota(jnp.int32, shape, 0) // row_div ==
            lax.broadcasted_iota(jnp.int32, shape, 1) // col_div)


def _s5in_kernel(u_ref, bl_ref, e_ref, o_ref, w_ref):
    p, wid = e_ref.shape
    h = LANE_TILE // (wid // p)
    mask = _group_mask((LANE_TILE, wid), h, p)
    for s in range(CHUNK):
        ex = jnp.dot(bl_ref[CHUNK - 1 - s, 0].astype(BF16), e_ref[...], preferred_element_type=F32)
        w_ref[s * LANE_TILE:(s + 1) * LANE_TILE, :] = jnp.where(mask, ex, 0.0).astype(BF16)
    for b in range(u_ref.shape[0]):
        ucat = jnp.concatenate([u_ref[b, s, 0] for s in range(CHUNK)], axis=1)
        o_ref[b, 0] = jnp.dot(ucat, w_ref[...], preferred_element_type=F32)


def s5_chunk_inputs(u_tm, bl2, h):
    bsz, _, ngt, nc, _ = u_tm.shape
    p = bl2.shape[3]
    wid = (LANE_TILE // h) * p
    e = _lane_replicator(p, wid)
    return pl.pallas_call(
        _s5in_kernel,
        grid=(ngt, 2),
        in_specs=[pl.BlockSpec((bsz, CHUNK, 1, nc, LANE_TILE), lambda gt, ri: (0, 0, gt, 0, 0)),
                  pl.BlockSpec((CHUNK, 1, LANE_TILE, p), lambda gt, ri: (0, ri, gt, 0)),
                  pl.BlockSpec((p, wid), lambda gt, ri: (0, 0))],
        out_specs=pl.BlockSpec((bsz, 1, nc, wid), lambda gt, ri: (0, ri, 0, gt)),
        out_shape=jax.ShapeDtypeStruct((bsz, 2, nc, ngt * wid), F32),
        scratch_shapes=[pltpu.VMEM((CHUNK * LANE_TILE, wid), BF16)],
        compiler_params=_cparams(2),
        name="s5in",
    )(u_tm, bl2, e)


def _s5scan_kernel(s_ref, a_ref, x_ref):
    nc, w = s_ref.shape[2], s_ref.shape[3]
    a1r, a1i = a_ref[0], a_ref[1]

    def cmul(xr, xi, yr, yi):
        return xr * yr - xi * yi, xr * yi + xi * yr

    row = lax.broadcasted_iota(jnp.int32, (8, w), 0)
    pows = [(a1r, a1i)]
    for _ in range(7):
        pows.append(cmul(pows[-1][0], pows[-1][1], a1r, a1i))
    steps = []
    for k in (1, 2, 4):
        steps.append((k, jnp.where(row >= k, pows[k - 1][0], 0.0), jnp.where(row >= k, pows[k - 1][1], 0.0)))
    pcr = jnp.zeros((8, w), F32)
    pci = jnp.zeros((8, w), F32)
    for r in range(8):
        pcr = jnp.where(row == r, pows[r][0], pcr)
        pci = jnp.where(row == r, pows[r][1], pci)

    def body(blk, carry):
        cr, ci = carry
        off = pl.multiple_of(blk * 8, 8)
        xr = s_ref[0, 0, pl.ds(off, 8), :]
        xi = s_ref[0, 1, pl.ds(off, 8), :]
        for k, kr, ki in steps:
            sr, si = pltpu.roll(xr, k, 0), pltpu.roll(xi, k, 0)
            dr, di = cmul(sr, si, kr, ki)
            xr, xi = xr + dr, xi + di
        dr, di = cmul(jnp.broadcast_to(cr, (8, w)), jnp.broadcast_to(ci, (8, w)), pcr, pci)
        xr, xi = xr + dr, xi + di
        x_ref[0, 0, pl.ds(off, 8), :] = jnp.where(row == 0, cr, pltpu.roll(xr, 1, 0))
        x_ref[0, 1, pl.ds(off, 8), :] = jnp.where(row == 0, ci, pltpu.roll(xi, 1, 0))
        return xr[7:8, :], xi[7:8, :]

    zero = jnp.zeros((1, w), F32)
    lax.fori_loop(0, nc // 8, body, (zero, zero))


def s5_chunk_scan(s, a, w=1024):
    bsz, _, nc, tot = s.shape
    blk = pl.BlockSpec((1, 2, nc, w), lambda b, l: (b, 0, 0, l))
    return pl.pallas_call(
        _s5scan_kernel,
        grid=(bsz, tot // w),
        in_specs=[blk, pl.BlockSpec((2, 1, w), lambda b, l: (0, 0, l))],
        out_specs=blk,
        out_shape=jax.ShapeDtypeStruct(s.shape, F32),
        compiler_params=_cparams(2),
        name="s5scan",
    )(s, a)


def _s5out_kernel(u_ref, kx_ref, cl_ref, x_ref, e_ref, ep_ref, o_ref, acc_ref):
    bsz, _, _, nc, _ = u_ref.shape
    p, wid = ep_ref.shape
    h = LANE_TILE // CHUNK
    mask_t = _group_mask((LANE_TILE, LANE_TILE), h, h)
    mask_c = _group_mask((LANE_TILE, wid), h, p)
    nt = (((1,), (1,)), ((), ()))
    kx = kx_ref[...].astype(BF16)
    for j in range(CHUNK):
        tj = jnp.where(mask_t, jnp.dot(kx, e_ref[j], preferred_element_type=F32), 0.0).astype(BF16)
        for b in range(bsz):
            lhs = u_ref[b, 0:CHUNK - j, 0].reshape((CHUNK - j) * nc, LANE_TILE)
            r = jnp.dot(lhs, tj, preferred_element_type=F32).reshape(CHUNK - j, nc, LANE_TILE)
            if j == 0:
                acc_ref[b] = r
            else:
                acc_ref[b, j:CHUNK] += r
    xr = x_ref[:, 0].reshape(bsz * nc, wid).astype(BF16)
    xi = x_ref[:, 1].reshape(bsz * nc, wid).astype(BF16)
    for t in range(CHUNK):
        wr = jnp.dot(cl_ref[t, 0].astype(BF16), ep_ref[...], preferred_element_type=F32)
        wi = jnp.dot(cl_ref[t, 1].astype(BF16), ep_ref[...], preferred_element_type=F32)
        wr = jnp.where(mask_c, wr, 0.0).astype(BF16)
        wi = jnp.where(mask_c, wi, 0.0).astype(BF16)
        y = (lax.dot_general(xr, wr, nt, preferred_element_type=F32)
             + lax.dot_general(xi, wi, nt, preferred_element_type=F32))
        for b in range(bsz):
            o_ref[b, t, 0] = (acc_ref[b, t] + y[b * nc:(b + 1) * nc]).astype(BF16)


def _lane_replicator(p, wid):
    return (jnp.arange(wid)[None, :] % p == jnp.arange(p)[:, None]).astype(BF16)


def s5_outputs(u_tm, kx2, cl2, x):
    bsz, _, ngt, nc, _ = u_tm.shape
    wid = x.shape[3] // ngt
    p = cl2.shape[3]
    h = LANE_TILE // CHUNK
    r = jnp.arange(LANE_TILE)
    e = ((r[None, :, None] // h == jnp.arange(CHUNK)[:, None, None])
         & (r[None, :, None] % h == r[None, None, :] % h)).astype(BF16)
    tile = pl.BlockSpec((bsz, CHUNK, 1, nc, LANE_TILE), lambda gt: (0, 0, gt, 0, 0))
    return pl.pallas_call(
        _s5out_kernel,
        grid=(ngt,),
        in_specs=[tile,
                  pl.BlockSpec((LANE_TILE, LANE_TILE), lambda gt: (gt, 0)),
                  pl.BlockSpec((CHUNK, 2, LANE_TILE, p), lambda gt: (0, 0, gt, 0)),
                  pl.BlockSpec((bsz, 2, nc, wid), lambda gt: (0, 0, 0, gt)),
                  pl.BlockSpec((CHUNK, LANE_TILE, LANE_TILE), lambda gt: (0, 0, 0)),
                  pl.BlockSpec((p, wid), lambda gt: (0, 0))],
        out_specs=tile,
        out_shape=jax.ShapeDtypeStruct(u_tm.shape, BF16),
        scratch_shapes=[pltpu.VMEM((bsz, CHUNK, nc, LANE_TILE), F32)],
        compiler_params=_cparams(1),
        name="s5out",
    )(u_tm, kx2, cl2, x, e, _lane_replicator(p, wid))


def _attn_kernel(q_ref, k_ref, v_ref, tri_ref, o_ref, vcat_ref, *, blk, nh, unroll):
    qi = pl.program_id(2)
    q = q_ref[0]
    lane = lax.broadcasted_iota(jnp.int32, q.shape, 1)
    heads = [(lane >= HEAD_DIM * h) & (lane < HEAD_DIM * (h + 1)) for h in range(nh)]
    zero = jnp.zeros_like(q)

    @pl.when(qi == 0)
    def _():
        def fill(j, _):
            vb = v_ref[0, pl.ds(pl.multiple_of(j * blk, blk), blk), :]
            vcat_ref[j] = jnp.concatenate([jnp.where(m, vb, zero) for m in heads], axis=0)
            return 0

        lax.fori_loop(0, vcat_ref.shape[0], fill, 0)

    qs = jnp.concatenate([jnp.where(m, q, zero) for m in heads], axis=0)
    row = lax.broadcasted_iota(jnp.int32, (blk, blk), 0)
    col = lax.broadcasted_iota(jnp.int32, (blk, blk), 1)
    past = jnp.concatenate([col < row] * nh, axis=0)
    nt = (((1,), (1,)), ((), ()))

    def block(kj, carry, acc, diag):
        kb = k_ref[0, pl.ds(pl.multiple_of(kj * blk, blk), blk), :]
        z = lax.dot_general(qs, kb, nt, preferred_element_type=F32)
        sp = jnp.maximum(z, 0.0) + jnp.log2(1.0 + jnp.exp2(-jnp.abs(z)))
        if diag:
            sp = jnp.where(past, sp, 0.0)
        cs = jnp.dot(sp.astype(BF16), tri_ref[...], preferred_element_type=F32)
        w = jnp.exp2(((z - sp) - cs) - carry)
        if diag:
            w = jnp.where(past, w, 0.0)
        wb = w.astype(BF16)
        wcat = jnp.concatenate([wb[h * blk:(h + 1) * blk] for h in range(nh)], axis=1)
        acc = acc + jnp.dot(wcat, vcat_ref[kj], preferred_element_type=F32)
        return carry + (cs[:, 0:1] + sp[:, 0:1]), acc

    def head(r):
        def run():
            c = block(qi, jnp.zeros((nh * blk, 1), F32), jnp.zeros(q.shape, F32), True)
            for i in range(r):
                c = block(qi - 1 - i, c[0], c[1], False)
            return c
        return run

    rem = qi % unroll
    c = lax.switch(rem, [head(r) for r in range(unroll)])

    def group(i, c):
        for r in range(unroll):
            c = block(qi - rem - 1 - r - unroll * i, c[0], c[1], False)
        return c

    c = lax.fori_loop(0, qi // unroll, group, c)
    o_ref[0] = c[1].astype(BF16)


def stick_attention(proj3, width, blk=256, nh=4, unroll=4):
    bsz, seqlen, _ = proj3.shape
    wl = nh * HEAD_DIM
    ngrp = width // wl
    ii = jnp.arange(blk)
    tri = (ii[:, None] > ii[None, :]).astype(BF16)
    return pl.pallas_call(
        functools.partial(_attn_kernel, blk=blk, nh=nh, unroll=unroll),
        grid=(bsz, ngrp, seqlen // blk),
        in_specs=[pl.BlockSpec((1, blk, wl), lambda b, h, i: (b, i, ngrp + h)),
                  pl.BlockSpec((1, seqlen, wl), lambda b, h, i: (b, 0, 2 * ngrp + h)),
                  pl.BlockSpec((1, seqlen, wl), lambda b, h, i: (b, 0, 3 * ngrp + h)),
                  pl.BlockSpec((blk, blk), lambda b, h, i: (0, 0))],
        out_specs=pl.BlockSpec((1, blk, wl), lambda b, h, i: (b, i, h)),
        out_shape=jax.ShapeDtypeStruct((bsz, seqlen, width), BF16),
        scratch_shapes=[pltpu.VMEM((seqlen // blk, nh * blk, wl), BF16)],
        compiler_params=_cparams(3),
        name="attn",
    )(proj3, proj3, proj3, tri)


def _out_kernel(y_ref, a_ref, x_ref, mod_ref, wglu_ref, gs_ref, ga_ref, w0_ref, w1_ref, o_ref):
    y = jax.nn.gelu(y_ref[...].astype(F32))
    z = jnp.dot(y.astype(BF16), wglu_ref[...], preferred_element_type=F32)
    ys = y * jax.nn.sigmoid(z)
    ysn = (_rms(ys) * gs_ref[...]).astype(BF16)
    an = (_rms(a_ref[...].astype(F32)) * ga_ref[...]).astype(BF16)
    o = (jnp.dot(ysn, w0_ref[...], preferred_element_type=F32)
         + jnp.dot(an, w1_ref[...], preferred_element_type=F32))
    o_ref[...] = x_ref[...] + mod_ref[0, 2:3, :] * o


def out_proj(y2, a2, x2, mod3, w_glu_bf, g_ssm, g_attn, w_out_bf, seqlen, tm=512):
    t, d = x2.shape
    ws = y2.shape[1]
    wa = a2.shape[1]
    return pl.pallas_call(
        _out_kernel,
        grid=(t // tm,),
        in_specs=[pl.BlockSpec((tm, ws), lambda i: (i, 0)),
                  pl.BlockSpec((tm, wa), lambda i: (i, 0)),
                  pl.BlockSpec((tm, d), lambda i: (i, 0)),
                  pl.BlockSpec((1, 6, d), lambda i: ((i * tm) // seqlen, 0, 0)),
                  pl.BlockSpec((ws, ws), lambda i: (0, 0)),
                  pl.BlockSpec((1, ws), lambda i: (0, 0)),
                  pl.BlockSpec((1, wa), lambda i: (0, 0)),
                  pl.BlockSpec((ws, d), lambda i: (0, 0)),
                  pl.BlockSpec((wa, d), lambda i: (1, 0))],
        out_specs=pl.BlockSpec((tm, d), lambda i: (i, 0)),
        out_shape=jax.ShapeDtypeStruct((t, d), F32),
        compiler_params=_cparams(1),
        name="outproj",
    )(y2, a2, x2, mod3, w_glu_bf, g_ssm.reshape(1, ws), g_attn.reshape(1, wa), w_out_bf, w_out_bf)


def _ffn_kernel(h_ref, mod_ref, g_ref, wg_ref, wu_ref, wd_ref, o_ref, xf_ref, acc_ref):
    j = pl.program_id(1)

    @pl.when(j == 0)
    def _():
        xn = _rms(h_ref[...]) * g_ref[...]
        xf_ref[...] = (xn * (1.0 + mod_ref[0, 4:5, :]) + mod_ref[0, 3:4, :]).astype(BF16)
        acc_ref[...] = jnp.zeros_like(acc_ref)

    xf = xf_ref[...]
    g = jnp.dot(xf, wg_ref[...], preferred_element_type=F32)
    u = jnp.dot(xf, wu_ref[...], preferred_element_type=F32)
    hm = (jax.nn.silu(g) * u).astype(BF16)
    acc_ref[...] += jnp.dot(hm, wd_ref[...], preferred_element_type=F32)

    @pl.when(j == pl.num_programs(1) - 1)
    def _():
        o_ref[...] = h_ref[...] + mod_ref[0, 5:6, :] * acc_ref[...]


def ffn(h2, mod3, g_ffn, wg_bf, wu_bf, wd_bf, seqlen, tm=512, tf=512):
    t, d = h2.shape
    dff = wg_bf.shape[1]
    return pl.pallas_call(
        _ffn_kernel,
        grid=(t // tm, dff // tf),
        in_specs=[pl.BlockSpec((tm, d), lambda i, j: (i, 0)),
                  pl.BlockSpec((1, 6, d), lambda i, j: ((i * tm) // seqlen, 0, 0)),
                  pl.BlockSpec((1, d), lambda i, j: (0, 0)),
                  pl.BlockSpec((d, tf), lambda i, j: (0, j)),
                  pl.BlockSpec((d, tf), lambda i, j: (0, j)),
                  pl.BlockSpec((tf, d), lambda i, j: (j, 0))],
        out_specs=pl.BlockSpec((tm, d), lambda i, j: (i, 0)),
        out_shape=jax.ShapeDtypeStruct((t, d), F32),
        scratch_shapes=[pltpu.VMEM((tm, d), BF16), pltpu.VMEM((tm, d), F32)],
        compiler_params=_cparams(2),
        name="ffn",
    )(h2, mod3, g_ffn.reshape(1, d), wg_bf, wu_bf, wd_bf)


def s5_mixer_chunked(u3, a_re, a_im, log_dt, b_re, b_im, c_re, c_im, d_skip):
    bsz, seqlen, width = u3.shape
    g, p = a_re.shape
    h = c_re.shape[1]
    nc = seqlen // CHUNK
    ngt = width // LANE_TILE
    assert h * CHUNK == LANE_TILE and g * h == width
    bl, cl, kx, a = s5_prep(a_re, a_im, log_dt, b_re, b_im, c_re, c_im, d_skip)
    u_tm = u3.reshape(bsz, nc, CHUNK, ngt, LANE_TILE).transpose(0, 2, 3, 1, 4)
    s = s5_chunk_inputs(u_tm, bl.reshape(CHUNK, 2, g * h, p), h)
    x = s5_chunk_scan(s, a.reshape(2, 1, g * p))
    y_tm = s5_outputs(u_tm, kx.reshape(g * h, CHUNK * h), cl.reshape(CHUNK, 2, g * h, p), x)
    return y_tm.transpose(0, 3, 1, 2, 4).reshape(bsz, seqlen, width)


def kernel(x, c, w_ada, b_ada, g_mix, w_in, a_re, a_im, log_dt, b_re, b_im, c_re, c_im, d_skip,
           w_glu, q_gain, k_gain, g_ssm_out, g_attn_out, w_out, g_ffn, w_gate, w_up, w_down):
    bsz, seqlen, d = x.shape
    depth = w_ada.shape[0]
    width = w_glu.shape[1]
    h2 = x.reshape(bsz * seqlen, d)
    for l in range(depth):
        mod3 = ada_mod(c, w_ada[l], b_ada[l]).reshape(bsz, 6, d)
        proj = in_proj(h2, mod3, g_mix[l], w_in[l].astype(BF16), q_gain[l], k_gain[l], seqlen)
        proj3 = proj.reshape(bsz, seqlen, 4 * width)
        y = s5_mixer_chunked(proj3[:, :, :width], a_re[l], a_im[l], log_dt[l], b_re[l], b_im[l],
                             c_re[l], c_im[l], d_skip[l])
        att = stick_attention(proj3, width)
        h2 = out_proj(y.reshape(bsz * seqlen, width), att.reshape(bsz * seqlen, width), h2, mod3,
                      w_glu[l].astype(BF16), g_ssm_out[l], g_attn_out[l], w_out[l].astype(BF16), seqlen)
        h2 = ffn(h2, mod3, g_ffn[l], w_gate[l].astype(BF16), w_up[l].astype(BF16),
                 w_down[l].astype(BF16), seqlen)
    return h2.reshape(bsz, seqlen, d)
```

```python
import functools

import jax
import jax.numpy as jnp
from jax import lax
from jax.experimental import pallas as pl
from jax.experimental.pallas import tpu as pltpu

F32 = jnp.float32
BF16 = jnp.bfloat16
EPS = 1e-6
HEAD_DIM = 64
CHUNK = 16
LANE_TILE = 256
VMEM_LIMIT = 56 * 1024 * 1024
Q_SCALE = HEAD_DIM ** -0.5 * 1.4426950408889634


def _cparams(n_axes, vmem=VMEM_LIMIT):
    return pltpu.CompilerParams(dimension_semantics=("arbitrary",) * n_axes,
                                vmem_limit_bytes=vmem)


def _rms(x):
    return x * lax.rsqrt(jnp.mean(x * x, axis=-1, keepdims=True) + EPS)


def _ada_kernel(c_ref, w_ref, b_ref, o_ref):
    d, tn = w_ref.shape
    rows = 256
    for b in range(c_ref.shape[0]):
        acc = jnp.zeros((8, tn), F32)
        for kc in range(d // rows):
            cond = jax.nn.silu(c_ref[b, kc * rows:(kc + 1) * rows, :])
            blk = cond * w_ref[kc * rows:(kc + 1) * rows, :]
            acc = acc + jnp.sum(blk.reshape(rows // 8, 8, tn), axis=0)
        o_ref[b:b + 1, :] = jnp.sum(acc, axis=0, keepdims=True) + b_ref[...]


def ada_mod(c, w_ada, b_ada, tn=1024):
    bsz, d = c.shape
    n = w_ada.shape[1]
    return pl.pallas_call(
        _ada_kernel,
        grid=(n // tn,),
        in_specs=[pl.BlockSpec((bsz, d, 1), lambda j: (0, 0, 0)),
                  pl.BlockSpec((d, tn), lambda j: (0, j)),
                  pl.BlockSpec((1, tn), lambda j: (0, j))],
        out_specs=pl.BlockSpec((bsz, tn), lambda j: (0, j)),
        out_shape=jax.ShapeDtypeStruct((bsz, n), F32),
        compiler_params=_cparams(1),
        name="ada",
    )(c.reshape(bsz, d, 1), w_ada, b_ada.reshape(1, n))


def _head_rmsnorm(r, ones_ref):
    outs = []
    for s in range(r.shape[1] // LANE_TILE):
        rs = r[:, s * LANE_TILE:(s + 1) * LANE_TILE]
        ss = jnp.dot((rs * rs).astype(BF16), ones_ref[...], preferred_element_type=F32)
        outs.append(rs * lax.rsqrt(ss * (1.0 / HEAD_DIM) + EPS))
    return jnp.concatenate(outs, axis=1)


def _proj_kernel(x_ref, mod_ref, g_ref, w_ref, qg_ref, kg_ref, ones_ref, o_ref, *, rows):
    tm = x_ref.shape[0]
    tn = w_ref.shape[1] // 4
    for r0 in range(0, tm, rows):
        xn = _rms(x_ref[r0:r0 + rows, :]) * g_ref[...]
        xm = (xn * (1.0 + mod_ref[0, 1:2, :]) + mod_ref[0, 0:1, :]).astype(BF16)
        for j in range(4):
            r = jnp.dot(xm, w_ref[:, j * tn:(j + 1) * tn], preferred_element_type=F32)
            if j == 1:
                r = (_head_rmsnorm(r, ones_ref) * qg_ref[...]) * Q_SCALE
            elif j == 2:
                r = _head_rmsnorm(r, ones_ref) * kg_ref[...]
            o_ref[r0:r0 + rows, j * tn:(j + 1) * tn] = r.astype(BF16)


def in_proj(x2, mod3, g_mix, w_in_bf, q_gain, k_gain, seqlen, tm=512, rows=256):
    t, d = x2.shape
    n = w_in_bf.shape[1]
    tn = n // 4
    reps = tn // HEAD_DIM
    qg = jnp.tile(q_gain.reshape(1, HEAD_DIM), (1, reps))
    kg = jnp.tile(k_gain.reshape(1, HEAD_DIM), (1, reps))
    idx = jnp.arange(LANE_TILE) // HEAD_DIM
    ones = (idx[:, None] == idx[None, :]).astype(BF16)
    const = lambda shape: pl.BlockSpec(shape, lambda i: (0, 0), pipeline_mode=pl.Buffered(1))
    return pl.pallas_call(
        functools.partial(_proj_kernel, rows=rows),
        grid=(t // tm,),
        in_specs=[pl.BlockSpec((tm, d), lambda i: (i, 0)),
                  pl.BlockSpec((1, 6, d), lambda i: ((i * tm) // seqlen, 0, 0)),
                  const((1, d)), const((d, n)), const((1, tn)), const((1, tn)),
                  const((LANE_TILE, LANE_TILE))],
        out_specs=pl.BlockSpec((tm, n), lambda i: (i, 0)),
        out_shape=jax.ShapeDtypeStruct((t, n), BF16),
        compiler_params=_cparams(1),
        name="proj",
    )(x2, mod3, g_mix.reshape(1, d), w_in_bf, qg, kg, ones)


def _zoh(ar, ai, dt):
    mag = jnp.exp(ar * dt)
    th = ai * dt
    lbr, lbi = mag * jnp.cos(th), mag * jnp.sin(th)
    den = ar * ar + ai * ai
    nr, ni = lbr - 1.0, lbi
    return lbr, lbi, (nr * ar + ni * ai) / den, (ni * ar - nr * ai) / den


def _s5prep_kernel(ar_ref, ai_ref, ldt_ref, btr_ref, bti_ref, cr_ref, ci_ref, d_ref,
                   bl_ref, cl_ref, kx_ref, a_ref):
    dt = jnp.exp(ldt_ref[...])
    lbr, lbi, kr, ki = _zoh(ar_ref[...], ai_ref[...], dt)
    btr, bti = btr_ref[...], bti_ref[...]
    bbr = kr * btr - ki * bti
    bbi = kr * bti + ki * btr
    c_re, c_im = cr_ref[...], ci_ref[...]
    h = c_re.shape[1]
    eye = (lax.broadcasted_iota(jnp.int32, (1, h, h), 1) ==
           lax.broadcasted_iota(jnp.int32, (1, h, h), 2))
    dn = (((2,), (2,)), ((0,), (0,)))
    pr, pi = jnp.ones_like(lbr), jnp.zeros_like(lbr)
    for j in range(CHUNK):
        blr = pr * bbr - pi * bbi
        bli = pr * bbi + pi * bbr
        bl_ref[j, 0] = blr
        bl_ref[j, 1] = bli
        kx = (lax.dot_general(blr, c_re, dn, precision=lax.Precision.HIGHEST,
                              preferred_element_type=F32)
              - lax.dot_general(bli, c_im, dn, precision=lax.Precision.HIGHEST,
                                preferred_element_type=F32))
        if j == 0:
            kx = kx + jnp.where(eye, d_ref[...], 0.0)
        kx_ref[:, :, j * h:(j + 1) * h] = kx
        pr, pi = pr * lbr - pi * lbi, pr * lbi + pi * lbr
        cl_ref[j, 0] = c_re * pr - c_im * pi
        cl_ref[j, 1] = -(c_re * pi + c_im * pr)
    a_ref[0] = pr
    a_ref[1] = pi


def s5_prep(a_re, a_im, log_dt, b_re, b_im, c_re, c_im, d_skip, gb=8):
    g, p = a_re.shape
    h = c_re.shape[1]
    vec = pl.BlockSpec((gb, 1, p), lambda i: (i, 0, 0))
    mat = pl.BlockSpec((gb, h, p), lambda i: (i, 0, 0))
    lag = pl.BlockSpec((CHUNK, 2, gb, h, p), lambda i: (0, 0, i, 0, 0))
    lag_shape = jax.ShapeDtypeStruct((CHUNK, 2, g, h, p), F32)
    return pl.pallas_call(
        _s5prep_kernel,
        grid=(g // gb,),
        in_specs=[vec, vec, pl.BlockSpec((gb, 1, 1), lambda i: (i, 0, 0)),
                  mat, mat, mat, mat, pl.BlockSpec((gb, 1, h), lambda i: (i, 0, 0))],
        out_specs=[lag, lag,
                   pl.BlockSpec((gb, h, CHUNK * h), lambda i: (i, 0, 0)),
                   pl.BlockSpec((2, gb, 1, p), lambda i: (0, i, 0, 0))],
        out_shape=[lag_shape, lag_shape,
                   jax.ShapeDtypeStruct((g, h, CHUNK * h), F32),
                   jax.ShapeDtypeStruct((2, g, 1, p), F32)],
        compiler_params=_cparams(1),
        name="s5prep",
    )(a_re.reshape(g, 1, p), a_im.reshape(g, 1, p), log_dt.reshape(g, 1, 1),
      b_re.transpose(0, 2, 1), b_im.transpose(0, 2, 1), c_re, c_im, d_skip.reshape(g, 1, h))


def _group_mask(shape, row_div, col_div):
    return (lax.broadcasted_iota(jnp.int32, shape, 0) // row_div ==
            lax.broadcasted_iota(jnp.int32, shape, 1) // col_div)


def _s5in_kernel(u_ref, bl_ref, e_ref, o_ref, w_ref):
    p, wid = e_ref.shape
    h = LANE_TILE // (wid // p)
    mask = _group_mask((LANE_TILE, wid), h, p)
    for s in range(CHUNK):
        ex = jnp.dot(bl_ref[CHUNK - 1 - s, 0].astype(BF16), e_ref[...], preferred_element_type=F32)
        w_ref[s * LANE_TILE:(s + 1) * LANE_TILE, :] = jnp.where(mask, ex, 0.0).astype(BF16)
    for b in range(u_ref.shape[0]):
        ucat = jnp.concatenate([u_ref[b, s, 0] for s in range(CHUNK)], axis=1)
        o_ref[b, 0] = jnp.dot(ucat, w_ref[...], preferred_element_type=F32)


def s5_chunk_inputs(u_tm, bl2, h):
    bsz, _, ngt, nc, _ = u_tm.shape
    p = bl2.shape[3]
    wid = (LANE_TILE // h) * p
    e = _lane_replicator(p, wid)
    return pl.pallas_call(
        _s5in_kernel,
        grid=(ngt, 2),
        in_specs=[pl.BlockSpec((bsz, CHUNK, 1, nc, LANE_TILE), lambda gt, ri: (0, 0, gt, 0, 0)),
                  pl.BlockSpec((CHUNK, 1, LANE_TILE, p), lambda gt, ri: (0, ri, gt, 0)),
                  pl.BlockSpec((p, wid), lambda gt, ri: (0, 0))],
        out_specs=pl.BlockSpec((bsz, 1, nc, wid), lambda gt, ri: (0, ri, 0, gt)),
        out_shape=jax.ShapeDtypeStruct((bsz, 2, nc, ngt * wid), F32),
        scratch_shapes=[pltpu.VMEM((CHUNK * LANE_TILE, wid), BF16)],
        compiler_params=_cparams(2),
        name="s5in",
    )(u_tm, bl2, e)


def _s5scan_kernel(s_ref, a_ref, x_ref):
    nc, w = s_ref.shape[2], s_ref.shape[3]
    a1r, a1i = a_ref[0], a_ref[1]

    def cmul(xr, xi, yr, yi):
        return xr * yr - xi * yi, xr * yi + xi * yr

    row = lax.broadcasted_iota(jnp.int32, (8, w), 0)
    pows = [(a1r, a1i)]
    for _ in range(7):
        pows.append(cmul(pows[-1][0], pows[-1][1], a1r, a1i))
    steps = []
    for k in (1, 2, 4):
        steps.append((k, jnp.where(row >= k, pows[k - 1][0], 0.0), jnp.where(row >= k, pows[k - 1][1], 0.0)))
    pcr = jnp.zeros((8, w), F32)
    pci = jnp.zeros((8, w), F32)
    for r in range(8):
        pcr = jnp.where(row == r, pows[r][0], pcr)
        pci = jnp.where(row == r, pows[r][1], pci)

    def body(blk, carry):
        cr, ci = carry
        off = pl.multiple_of(blk * 8, 8)
        xr = s_ref[0, 0, pl.ds(off, 8), :]
        xi = s_ref[0, 1, pl.ds(off, 8), :]
        for k, kr, ki in steps:
            sr, si = pltpu.roll(xr, k, 0), pltpu.roll(xi, k, 0)
            dr, di = cmul(sr, si, kr, ki)
            xr, xi = xr + dr, xi + di
        dr, di = cmul(jnp.broadcast_to(cr, (8, w)), jnp.broadcast_to(ci, (8, w)), pcr, pci)
        xr, xi = xr + dr, xi + di
        x_ref[0, 0, pl.ds(off, 8), :] = jnp.where(row == 0, cr, pltpu.roll(xr, 1, 0))
        x_ref[0, 1, pl.ds(off, 8), :] = jnp.where(row == 0, ci, pltpu.roll(xi, 1, 0))
        return xr[7:8, :], xi[7:8, :]

    zero = jnp.zeros((1, w), F32)
    lax.fori_loop(0, nc // 8, body, (zero, zero))


def s5_chunk_scan(s, a, w=1024):
    bsz, _, nc, tot = s.shape
    blk = pl.BlockSpec((1, 2, nc, w), lambda b, l: (b, 0, 0, l))
    return pl.pallas_call(
        _s5scan_kernel,
        grid=(bsz, tot // w),
        in_specs=[blk, pl.BlockSpec((2, 1, w), lambda b, l: (0, 0, l))],
        out_specs=blk,
        out_shape=jax.ShapeDtypeStruct(s.shape, F32),
        compiler_params=_cparams(2),
        name="s5scan",
    )(s, a)


def _s5out_kernel(u_ref, kx_ref, cl_ref, x_ref, e_ref, ep_ref, o_ref, acc_ref):
    bsz, _, _, nc, _ = u_ref.shape
    p, wid = ep_ref.shape
    h = LANE_TILE // CHUNK
    mask_t = _group_mask((LANE_TILE, LANE_TILE), h, h)
    mask_c = _group_mask((LANE_TILE, wid), h, p)
    nt = (((1,), (1,)), ((), ()))
    kx = kx_ref[...].astype(BF16)
    for j in range(CHUNK):
        tj = jnp.where(mask_t, jnp.dot(kx, e_ref[j], preferred_element_type=F32), 0.0).astype(BF16)
        for b in range(bsz):
            lhs = u_ref[b, 0:CHUNK - j, 0].reshape((CHUNK - j) * nc, LANE_TILE)
            r = jnp.dot(lhs, tj, preferred_element_type=F32).reshape(CHUNK - j, nc, LANE_TILE)
            if j == 0:
                acc_ref[b] = r
            else:
                acc_ref[b, j:CHUNK] += r
    xr = x_ref[:, 0].reshape(bsz * nc, wid).astype(BF16)
    xi = x_ref[:, 1].reshape(bsz * nc, wid).astype(BF16)
    for t in range(CHUNK):
        wr = jnp.dot(cl_ref[t, 0].astype(BF16), ep_ref[...], preferred_element_type=F32)
        wi = jnp.dot(cl_ref[t, 1].astype(BF16), ep_ref[...], preferred_element_type=F32)
        wr = jnp.where(mask_c, wr, 0.0).astype(BF16)
        wi = jnp.where(mask_c, wi, 0.0).astype(BF16)
        y = (lax.dot_general(xr, wr, nt, preferred_element_type=F32)
             + lax.dot_general(xi, wi, nt, preferred_element_type=F32))
        for b in range(bsz):
            o_ref[b, t, 0] = (acc_ref[b, t] + y[b * nc:(b + 1) * nc]).astype(BF16)


def _lane_replicator(p, wid):
    return (jnp.arange(wid)[None, :] % p == jnp.arange(p)[:, None]).astype(BF16)


def s5_outputs(u_tm, kx2, cl2, x):
    bsz, _, ngt, nc, _ = u_tm.shape
    wid = x.shape[3] // ngt
    p = cl2.shape[3]
    h = LANE_TILE // CHUNK
    r = jnp.arange(LANE_TILE)
    e = ((r[None, :, None] // h == jnp.arange(CHUNK)[:, None, None])
         & (r[None, :, None] % h == r[None, None, :] % h)).astype(BF16)
    tile = pl.BlockSpec((bsz, CHUNK, 1, nc, LANE_TILE), lambda gt: (0, 0, gt, 0, 0))
    return pl.pallas_call(
        _s5out_kernel,
        grid=(ngt,),
        in_specs=[tile,
                  pl.BlockSpec((LANE_TILE, LANE_TILE), lambda gt: (gt, 0)),
                  pl.BlockSpec((CHUNK, 2, LANE_TILE, p), lambda gt: (0, 0, gt, 0)),
                  pl.BlockSpec((bsz, 2, nc, wid), lambda gt: (0, 0, 0, gt)),
                  pl.BlockSpec((CHUNK, LANE_TILE, LANE_TILE), lambda gt: (0, 0, 0)),
                  pl.BlockSpec((p, wid), lambda gt: (0, 0))],
        out_specs=tile,
        out_shape=jax.ShapeDtypeStruct(u_tm.shape, BF16),
        scratch_shapes=[pltpu.VMEM((bsz, CHUNK, nc, LANE_TILE), F32)],
        compiler_params=_cparams(1),
        name="s5out",
    )(u_tm, kx2, cl2, x, e, _lane_replicator(p, wid))


def _attn_kernel(q_ref, k_ref, v_ref, tri_ref, o_ref, vcat_ref, *, blk, nh, unroll):
    qi = pl.program_id(2)
    q = q_ref[0]
    lane = lax.broadcasted_iota(jnp.int32, q.shape, 1)
    heads = [(lane >= HEAD_DIM * h) & (lane < HEAD_DIM * (h + 1)) for h in range(nh)]
    zero = jnp.zeros_like(q)

    @pl.when(qi == 0)
    def _():
        def fill(j, _):
            vb = v_ref[0, pl.ds(pl.multiple_of(j * blk, blk), blk), :]
            vcat_ref[j] = jnp.concatenate([jnp.where(m, vb, zero) for m in heads], axis=0)
            return 0

        lax.fori_loop(0, vcat_ref.shape[0], fill, 0)

    qs = jnp.concatenate([jnp.where(m, q, zero) for m in heads], axis=0)
    row = lax.broadcasted_iota(jnp.int32, (blk, blk), 0)
    col = lax.broadcasted_iota(jnp.int32, (blk, blk), 1)
    past = jnp.concatenate([col < row] * nh, axis=0)
    nt = (((1,), (1,)), ((), ()))

    def block(kj, carry, acc, diag):
        kb = k_ref[0, pl.ds(pl.multiple_of(kj * blk, blk), blk), :]
        z = lax.dot_general(qs, kb, nt, preferred_element_type=F32)
        sp = jnp.maximum(z, 0.0) + jnp.log2(1.0 + jnp.exp2(-jnp.abs(z)))
        if diag:
            sp = jnp.where(past, sp, 0.0)
        cs = jnp.dot(sp.astype(BF16), tri_ref[...], preferred_element_type=F32)
        w = jnp.exp2(((z - sp) - cs) - carry)
        if diag:
            w = jnp.where(past, w, 0.0)
        wb = w.astype(BF16)
        wcat = jnp.concatenate([wb[h * blk:(h + 1) * blk] for h in range(nh)], axis=1)
        acc = acc + jnp.dot(wcat, vcat_ref[kj], preferred_element_type=F32)
        return carry + (cs[:, 0:1] + sp[:, 0:1]), acc

    def head(r):
        def run():
            c = block(qi, jnp.zeros((nh * blk, 1), F32), jnp.zeros(q.shape, F32), True)
            for i in range(r):
                c = block(qi - 1 - i, c[0], c[1], False)
            return c
        return run

    rem = qi % unroll
    c = lax.switch(rem, [head(r) for r in range(unroll)])

    def group(i, c):
        for r in range(unroll):
            c = block(qi - rem - 1 - r - unroll * i, c[0], c[1], False)
        return c

    c = lax.fori_loop(0, qi // unroll, group, c)
    o_ref[0] = c[1].astype(BF16)


def stick_attention(proj3, width, blk=256, nh=4, unroll=4):
    bsz, seqlen, _ = proj3.shape
    wl = nh * HEAD_DIM
    ngrp = width // wl
    ii = jnp.arange(blk)
    tri = (ii[:, None] > ii[None, :]).astype(BF16)
    return pl.pallas_call(
        functools.partial(_attn_kernel, blk=blk, nh=nh, unroll=unroll),
        grid=(bsz, ngrp, seqlen // blk),
        in_specs=[pl.BlockSpec((1, blk, wl), lambda b, h, i: (b, i, ngrp + h)),
                  pl.BlockSpec((1, seqlen, wl), lambda b, h, i: (b, 0, 2 * ngrp + h)),
                  pl.BlockSpec((1, seqlen, wl), lambda b, h, i: (b, 0, 3 * ngrp + h)),
                  pl.BlockSpec((blk, blk), lambda b, h, i: (0, 0))],
        out_specs=pl.BlockSpec((1, blk, wl), lambda b, h, i: (b, i, h)),
        out_shape=jax.ShapeDtypeStruct((bsz, seqlen, width), BF16),
        scratch_shapes=[pltpu.VMEM((seqlen // blk, nh * blk, wl), BF16)],
        compiler_params=_cparams(3),
        name="attn",
    )(proj3, proj3, proj3, tri)


def _out_kernel(y_ref, a_ref, x_ref, mod_ref, wglu_ref, gs_ref, ga_ref, w0_ref, w1_ref, gf_ref,
                h_ref, xf_ref, *, rows):
    for r0 in range(0, y_ref.shape[0], rows):
        rs = slice(r0, r0 + rows)
        y = jax.nn.gelu(y_ref[rs, :].astype(F32))
        z = jnp.dot(y.astype(BF16), wglu_ref[...], preferred_element_type=F32)
        ys = y * jax.nn.sigmoid(z)
        ysn = (_rms(ys) * gs_ref[...]).astype(BF16)
        an = (_rms(a_ref[rs, :].astype(F32)) * ga_ref[...]).astype(BF16)
        o = (jnp.dot(ysn, w0_ref[...], preferred_element_type=F32)
             + jnp.dot(an, w1_ref[...], preferred_element_type=F32))
        h = x_ref[rs, :] + mod_ref[0, 2:3, :] * o
        h_ref[rs, :] = h
        xn = _rms(h) * gf_ref[...]
        xf_ref[rs, :] = (xn * (1.0 + mod_ref[0, 4:5, :]) + mod_ref[0, 3:4, :]).astype(BF16)


def out_proj(y2, a2, x2, mod3, w_glu_bf, g_ssm, g_attn, w_out_bf, g_ffn, seqlen, tm=512, rows=256):
    t, d = x2.shape
    ws = y2.shape[1]
    wa = a2.shape[1]
    const = lambda shape, r=0: pl.BlockSpec(shape, lambda i: (r, 0), pipeline_mode=pl.Buffered(1))
    return pl.pallas_call(
        functools.partial(_out_kernel, rows=rows),
        grid=(t // tm,),
        in_specs=[pl.BlockSpec((tm, ws), lambda i: (i, 0)),
                  pl.BlockSpec((tm, wa), lambda i: (i, 0)),
                  pl.BlockSpec((tm, d), lambda i: (i, 0)),
                  pl.BlockSpec((1, 6, d), lambda i: ((i * tm) // seqlen, 0, 0)),
                  const((ws, ws)), const((1, ws)), const((1, wa)),
                  const((ws, d)), const((wa, d), 1), const((1, d))],
        out_specs=[pl.BlockSpec((tm, d), lambda i: (i, 0)),
                   pl.BlockSpec((tm, d), lambda i: (i, 0))],
        out_shape=[jax.ShapeDtypeStruct((t, d), F32), jax.ShapeDtypeStruct((t, d), BF16)],
        compiler_params=_cparams(1),
        name="outproj",
    )(y2, a2, x2, mod3, w_glu_bf, g_ssm.reshape(1, ws), g_attn.reshape(1, wa), w_out_bf, w_out_bf,
      g_ffn.reshape(1, d))


def _ffn_kernel(xf_ref, h_ref, mod_ref, wg_ref, wu_ref, wd_ref, o_ref):
    j = pl.program_id(1)

    @pl.when(j == 0)
    def _():
        o_ref[...] = jnp.zeros_like(o_ref)

    xf = xf_ref[...]
    g = jnp.dot(xf, wg_ref[...].astype(BF16), preferred_element_type=F32)
    u = jnp.dot(xf, wu_ref[...].astype(BF16), preferred_element_type=F32)
    hm = (jax.nn.silu(g) * u).astype(BF16)
    o_ref[...] += jnp.dot(hm, wd_ref[...].astype(BF16), preferred_element_type=F32)

    @pl.when(j == pl.num_programs(1) - 1)
    def _():
        o_ref[...] = h_ref[...] + mod_ref[0, 5:6, :] * o_ref[...]


def ffn(xf2, h2, mod3, w_gate, w_up, w_down, seqlen, tm=1024, tf=256):
    t, d = h2.shape
    dff = w_gate.shape[1]
    row = lambda dt: pl.BlockSpec((tm, d), lambda i, j: (i, 0), pipeline_mode=pl.Buffered(1))
    return pl.pallas_call(
        _ffn_kernel,
        grid=(t // tm, dff // tf),
        in_specs=[row(BF16), row(F32),
                  pl.BlockSpec((1, 6, d), lambda i, j: ((i * tm) // seqlen, 0, 0)),
                  pl.BlockSpec((d, tf), lambda i, j: (0, j)),
                  pl.BlockSpec((d, tf), lambda i, j: (0, j)),
                  pl.BlockSpec((tf, d), lambda i, j: (j, 0))],
        out_specs=pl.BlockSpec((tm, d), lambda i, j: (i, 0)),
        out_shape=jax.ShapeDtypeStruct((t, d), F32),
        compiler_params=_cparams(2),
        name="ffn",
    )(xf2, h2, mod3, w_gate, w_up, w_down)


def s5_mixer_chunked(u3, a_re, a_im, log_dt, b_re, b_im, c_re, c_im, d_skip):
    bsz, seqlen, width = u3.shape
    g, p = a_re.shape
    h = c_re.shape[1]
    nc = seqlen // CHUNK
    ngt = width // LANE_TILE
    assert h * CHUNK == LANE_TILE and g * h == width
    bl, cl, kx, a = s5_prep(a_re, a_im, log_dt, b_re, b_im, c_re, c_im, d_skip)
    u_tm = u3.reshape(bsz, nc, CHUNK, ngt, LANE_TILE).transpose(0, 2, 3, 1, 4)
    s = s5_chunk_inputs(u_tm, bl.reshape(CHUNK, 2, g * h, p), h)
    x = s5_chunk_scan(s, a.reshape(2, 1, g * p))
    y_tm = s5_outputs(u_tm, kx.reshape(g * h, CHUNK * h), cl.reshape(CHUNK, 2, g * h, p), x)
    return y_tm.transpose(0, 3, 1, 2, 4).reshape(bsz, seqlen, width)


def kernel(x, c, w_ada, b_ada, g_mix, w_in, a_re, a_im, log_dt, b_re, b_im, c_re, c_im, d_skip,
           w_glu, q_gain, k_gain, g_ssm_out, g_attn_out, w_out, g_ffn, w_gate, w_up, w_down):
    bsz, seqlen, d = x.shape
    depth = w_ada.shape[0]
    width = w_glu.shape[1]
    h2 = x.reshape(bsz * seqlen, d)
    for l in range(depth):
        mod3 = ada_mod(c, w_ada[l], b_ada[l]).reshape(bsz, 6, d)
        proj = in_proj(h2, mod3, g_mix[l], w_in[l].astype(BF16), q_gain[l], k_gain[l], seqlen)
        proj3 = proj.reshape(bsz, seqlen, 4 * width)
        y = s5_mixer_chunked(proj3[:, :, :width], a_re[l], a_im[l], log_dt[l], b_re[l], b_im[l],
                             c_re[l], c_im[l], d_skip[l])
        att = stick_attention(proj3, width)
        h2, xf = out_proj(y.reshape(bsz * seqlen, width), att.reshape(bsz * seqlen, width), h2, mod3,
                          w_glu[l].astype(BF16), g_ssm_out[l], g_attn_out[l], w_out[l].astype(BF16),
                          g_ffn[l], seqlen)
        h2 = ffn(xf, h2, mod3, w_gate[l], w_up[l], w_down[l], seqlen)
    return h2.reshape(bsz, seqlen, d)
```

```python
import functools

import jax
import jax.numpy as jnp
from jax import lax
from jax.experimental import pallas as pl
from jax.experimental.pallas import tpu as pltpu

F32 = jnp.float32
BF16 = jnp.bfloat16
EPS = 1e-6
HEAD_DIM = 64
CHUNK = 16
LANE_TILE = 256
VMEM_LIMIT = 56 * 1024 * 1024
Q_SCALE = HEAD_DIM ** -0.5 * 1.4426950408889634


def _cparams(n_axes, vmem=VMEM_LIMIT):
    return pltpu.CompilerParams(dimension_semantics=("arbitrary",) * n_axes,
                                vmem_limit_bytes=vmem)


def _rms(x):
    return x * lax.rsqrt(jnp.mean(x * x, axis=-1, keepdims=True) + EPS)


def _ada_kernel(c_ref, w_ref, b_ref, o_ref):
    d, tn = w_ref.shape
    rows = 256
    for b in range(c_ref.shape[0]):
        acc = jnp.zeros((8, tn), F32)
        for kc in range(d // rows):
            cond = jax.nn.silu(c_ref[b, kc * rows:(kc + 1) * rows, :])
            blk = cond * w_ref[kc * rows:(kc + 1) * rows, :]
            acc = acc + jnp.sum(blk.reshape(rows // 8, 8, tn), axis=0)
        o_ref[b:b + 1, :] = jnp.sum(acc, axis=0, keepdims=True) + b_ref[...]


def ada_mod(c, w_ada, b_ada, tn=1024):
    bsz, d = c.shape
    n = w_ada.shape[1]
    return pl.pallas_call(
        _ada_kernel,
        grid=(n // tn,),
        in_specs=[pl.BlockSpec((bsz, d, 1), lambda j: (0, 0, 0)),
                  pl.BlockSpec((d, tn), lambda j: (0, j)),
                  pl.BlockSpec((1, tn), lambda j: (0, j))],
        out_specs=pl.BlockSpec((bsz, tn), lambda j: (0, j)),
        out_shape=jax.ShapeDtypeStruct((bsz, n), F32),
        compiler_params=_cparams(1),
        name="ada",
    )(c.reshape(bsz, d, 1), w_ada, b_ada.reshape(1, n))


def _head_rmsnorm(r, ones_ref):
    outs = []
    for s in range(r.shape[1] // LANE_TILE):
        rs = r[:, s * LANE_TILE:(s + 1) * LANE_TILE]
        ss = jnp.dot((rs * rs).astype(BF16), ones_ref[...], preferred_element_type=F32)
        outs.append(rs * lax.rsqrt(ss * (1.0 / HEAD_DIM) + EPS))
    return jnp.concatenate(outs, axis=1)


def _proj_kernel(x_ref, mod_ref, g_ref, w_ref, qg_ref, kg_ref, ones_ref, u_ref, o_ref, us_ref, *, rows):
    tm = x_ref.shape[0]
    tn = w_ref.shape[1] // 4
    cpr = rows // CHUNK
    for r0 in range(0, tm, rows):
        xn = _rms(x_ref[r0:r0 + rows, :]) * g_ref[...]
        xm = (xn * (1.0 + mod_ref[0, 1:2, :]) + mod_ref[0, 0:1, :]).astype(BF16)
        for j in range(4):
            r = jnp.dot(xm, w_ref[:, j * tn:(j + 1) * tn], preferred_element_type=F32)
            if j == 0:
                c0 = r0 // CHUNK
                for k in range(tn // 128):
                    us_ref[k] = r[:, k * 128:(k + 1) * 128]
                for s in range(CHUNK):
                    for k in range(tn // 128):
                        us = us_ref[k, pl.ds(s, cpr, stride=CHUNK), :]
                        u_ref[0, s, k // 2, c0:c0 + cpr, (k % 2) * 128:(k % 2 + 1) * 128] = us.astype(BF16)
                continue
            if j == 1:
                r = (_head_rmsnorm(r, ones_ref) * qg_ref[...]) * Q_SCALE
            elif j == 2:
                r = _head_rmsnorm(r, ones_ref) * kg_ref[...]
            o_ref[r0:r0 + rows, (j - 1) * tn:j * tn] = r.astype(BF16)


def in_proj(x2, mod3, g_mix, w_in_bf, q_gain, k_gain, bsz, seqlen, tm=512, rows=256):
    t, d = x2.shape
    n = w_in_bf.shape[1]
    tn = n // 4
    reps = tn // HEAD_DIM
    qg = jnp.tile(q_gain.reshape(1, HEAD_DIM), (1, reps))
    kg = jnp.tile(k_gain.reshape(1, HEAD_DIM), (1, reps))
    idx = jnp.arange(LANE_TILE) // HEAD_DIM
    ones = (idx[:, None] == idx[None, :]).astype(BF16)
    const = lambda shape: pl.BlockSpec(shape, lambda i: (0, 0), pipeline_mode=pl.Buffered(1))
    tpb = seqlen // tm
    ngt = tn // LANE_TILE
    return pl.pallas_call(
        functools.partial(_proj_kernel, rows=rows),
        grid=(t // tm,),
        in_specs=[pl.BlockSpec((tm, d), lambda i: (i, 0)),
                  pl.BlockSpec((1, 6, d), lambda i: (i // tpb, 0, 0)),
                  const((1, d)), const((d, n)), const((1, tn)), const((1, tn)),
                  const((LANE_TILE, LANE_TILE))],
        out_specs=[pl.BlockSpec((1, CHUNK, ngt, tm // CHUNK, LANE_TILE), lambda i: (i // tpb, 0, 0, i % tpb, 0)),
                   pl.BlockSpec((tm, 3 * tn), lambda i: (i, 0))],
        out_shape=[jax.ShapeDtypeStruct((bsz, CHUNK, ngt, seqlen // CHUNK, LANE_TILE), BF16),
                   jax.ShapeDtypeStruct((t, 3 * tn), BF16)],
        scratch_shapes=[pltpu.VMEM((tn // 128, rows, 128), F32)],
        compiler_params=_cparams(1),
        name="proj",
    )(x2, mod3, g_mix.reshape(1, d), w_in_bf, qg, kg, ones)


def _zoh(ar, ai, dt):
    mag = jnp.exp(ar * dt)
    th = ai * dt
    lbr, lbi = mag * jnp.cos(th), mag * jnp.sin(th)
    den = ar * ar + ai * ai
    nr, ni = lbr - 1.0, lbi
    return lbr, lbi, (nr * ar + ni * ai) / den, (ni * ar - nr * ai) / den


def _s5prep_kernel(ar_ref, ai_ref, ldt_ref, btr_ref, bti_ref, cr_ref, ci_ref, d_ref,
                   bl_ref, cl_ref, kx_ref, a_ref):
    dt = jnp.exp(ldt_ref[...])
    lbr, lbi, kr, ki = _zoh(ar_ref[...], ai_ref[...], dt)
    btr, bti = btr_ref[...], bti_ref[...]
    bbr = kr * btr - ki * bti
    bbi = kr * bti + ki * btr
    c_re, c_im = cr_ref[...], ci_ref[...]
    h = c_re.shape[1]
    eye = (lax.broadcasted_iota(jnp.int32, (1, h, h), 1) ==
           lax.broadcasted_iota(jnp.int32, (1, h, h), 2))
    dn = (((2,), (2,)), ((0,), (0,)))
    pr, pi = jnp.ones_like(lbr), jnp.zeros_like(lbr)
    for j in range(CHUNK):
        blr = pr * bbr - pi * bbi
        bli = pr * bbi + pi * bbr
        bl_ref[j, 0] = blr
        bl_ref[j, 1] = bli
        kx = (lax.dot_general(blr, c_re, dn, precision=lax.Precision.HIGHEST,
                              preferred_element_type=F32)
              - lax.dot_general(bli, c_im, dn, precision=lax.Precision.HIGHEST,
                                preferred_element_type=F32))
        if j == 0:
            kx = kx + jnp.where(eye, d_ref[...], 0.0)
        kx_ref[:, :, j * h:(j + 1) * h] = kx
        pr, pi = pr * lbr - pi * lbi, pr * lbi + pi * lbr
        cl_ref[j, 0] = c_re * pr - c_im * pi
        cl_ref[j, 1] = -(c_re * pi + c_im * pr)
    a_ref[0] = pr
    a_ref[1] = pi


def s5_prep(a_re, a_im, log_dt, b_re, b_im, c_re, c_im, d_skip, gb=8):
    g, p = a_re.shape
    h = c_re.shape[1]
    vec = pl.BlockSpec((gb, 1, p), lambda i: (i, 0, 0))
    mat = pl.BlockSpec((gb, h, p), lambda i: (i, 0, 0))
    lag = pl.BlockSpec((CHUNK, 2, gb, h, p), lambda i: (0, 0, i, 0, 0))
    lag_shape = jax.ShapeDtypeStruct((CHUNK, 2, g, h, p), F32)
    return pl.pallas_call(
        _s5prep_kernel,
        grid=(g // gb,),
        in_specs=[vec, vec, pl.BlockSpec((gb, 1, 1), lambda i: (i, 0, 0)),
                  mat, mat, mat, mat, pl.BlockSpec((gb, 1, h), lambda i: (i, 0, 0))],
        out_specs=[lag, lag,
                   pl.BlockSpec((gb, h, CHUNK * h), lambda i: (i, 0, 0)),
                   pl.BlockSpec((2, gb, 1, p), lambda i: (0, i, 0, 0))],
        out_shape=[lag_shape, lag_shape,
                   jax.ShapeDtypeStruct((g, h, CHUNK * h), F32),
                   jax.ShapeDtypeStruct((2, g, 1, p), F32)],
        compiler_params=_cparams(1),
        name="s5prep",
    )(a_re.reshape(g, 1, p), a_im.reshape(g, 1, p), log_dt.reshape(g, 1, 1),
      b_re.transpose(0, 2, 1), b_im.transpose(0, 2, 1), c_re, c_im, d_skip.reshape(g, 1, h))


def _group_mask(shape, row_div, col_div):
    return (lax.broadcasted_iota(jnp.int32, shape, 0) // row_div ==
            lax.broadcasted_iota(jnp.int32, shape, 1) // col_div)


def _s5in_kernel(u_ref, bl_ref, e_ref, o_ref, w_ref):
    p, wid = e_ref.shape
    h = LANE_TILE // (wid // p)
    mask = _group_mask((LANE_TILE, wid), h, p)
    for s in range(CHUNK):
        ex = jnp.dot(bl_ref[CHUNK - 1 - s, 0].astype(BF16), e_ref[...], preferred_element_type=F32)
        w_ref[s * LANE_TILE:(s + 1) * LANE_TILE, :] = jnp.where(mask, ex, 0.0).astype(BF16)
    for b in range(u_ref.shape[0]):
        ucat = jnp.concatenate([u_ref[b, s, 0] for s in range(CHUNK)], axis=1)
        o_ref[b, 0] = jnp.dot(ucat, w_ref[...], preferred_element_type=F32)


def s5_chunk_inputs(u_tm, bl2, h):
    bsz, _, ngt, nc, _ = u_tm.shape
    p = bl2.shape[3]
    wid = (LANE_TILE // h) * p
    e = _lane_replicator(p, wid)
    return pl.pallas_call(
        _s5in_kernel,
        grid=(ngt, 2),
        in_specs=[pl.BlockSpec((bsz, CHUNK, 1, nc, LANE_TILE), lambda gt, ri: (0, 0, gt, 0, 0)),
                  pl.BlockSpec((CHUNK, 1, LANE_TILE, p), lambda gt, ri: (0, ri, gt, 0)),
                  pl.BlockSpec((p, wid), lambda gt, ri: (0, 0))],
        out_specs=pl.BlockSpec((bsz, 1, nc, wid), lambda gt, ri: (0, ri, 0, gt)),
        out_shape=jax.ShapeDtypeStruct((bsz, 2, nc, ngt * wid), F32),
        scratch_shapes=[pltpu.VMEM((CHUNK * LANE_TILE, wid), BF16)],
        compiler_params=_cparams(2),
        name="s5in",
    )(u_tm, bl2, e)


def _s5scan_kernel(s_ref, a_ref, x_ref):
    nc, w = s_ref.shape[2], s_ref.shape[3]
    a1r, a1i = a_ref[0], a_ref[1]

    def cmul(xr, xi, yr, yi):
        return xr * yr - xi * yi, xr * yi + xi * yr

    row = lax.broadcasted_iota(jnp.int32, (8, w), 0)
    pows = [(a1r, a1i)]
    for _ in range(7):
        pows.append(cmul(pows[-1][0], pows[-1][1], a1r, a1i))
    steps = []
    for k in (1, 2, 4):
        steps.append((k, jnp.where(row >= k, pows[k - 1][0], 0.0), jnp.where(row >= k, pows[k - 1][1], 0.0)))
    pcr = jnp.zeros((8, w), F32)
    pci = jnp.zeros((8, w), F32)
    for r in range(8):
        pcr = jnp.where(row == r, pows[r][0], pcr)
        pci = jnp.where(row == r, pows[r][1], pci)

    def body(blk, carry):
        cr, ci = carry
        off = pl.multiple_of(blk * 8, 8)
        xr = s_ref[0, 0, pl.ds(off, 8), :]
        xi = s_ref[0, 1, pl.ds(off, 8), :]
        for k, kr, ki in steps:
            sr, si = pltpu.roll(xr, k, 0), pltpu.roll(xi, k, 0)
            dr, di = cmul(sr, si, kr, ki)
            xr, xi = xr + dr, xi + di
        dr, di = cmul(jnp.broadcast_to(cr, (8, w)), jnp.broadcast_to(ci, (8, w)), pcr, pci)
        xr, xi = xr + dr, xi + di
        x_ref[0, 0, pl.ds(off, 8), :] = jnp.where(row == 0, cr, pltpu.roll(xr, 1, 0))
        x_ref[0, 1, pl.ds(off, 8), :] = jnp.where(row == 0, ci, pltpu.roll(xi, 1, 0))
        return xr[7:8, :], xi[7:8, :]

    zero = jnp.zeros((1, w), F32)
    lax.fori_loop(0, nc // 8, body, (zero, zero))


def s5_chunk_scan(s, a, w=1024):
    bsz, _, nc, tot = s.shape
    blk = pl.BlockSpec((1, 2, nc, w), lambda b, l: (b, 0, 0, l))
    return pl.pallas_call(
        _s5scan_kernel,
        grid=(bsz, tot // w),
        in_specs=[blk, pl.BlockSpec((2, 1, w), lambda b, l: (0, 0, l))],
        out_specs=blk,
        out_shape=jax.ShapeDtypeStruct(s.shape, F32),
        compiler_params=_cparams(2),
        name="s5scan",
    )(s, a)


def _s5out_kernel(u_ref, kx_ref, cl_ref, x_ref, e_ref, ep_ref, o_ref, tr_ref):
    bsz, _, _, nc, _ = u_ref.shape
    p, wid = ep_ref.shape
    h = LANE_TILE // CHUNK
    mask_t = _group_mask((LANE_TILE, LANE_TILE), h, h)
    mask_c = _group_mask((LANE_TILE, wid), h, p)
    nt = (((1,), (1,)), ((), ()))
    kx = kx_ref[...].astype(BF16)
    for j in range(CHUNK):
        tj = jnp.where(mask_t, jnp.dot(kx, e_ref[j], preferred_element_type=F32), 0.0)
        tr_ref[(CHUNK - 1 - j) * LANE_TILE:(CHUNK - j) * LANE_TILE, :] = tj.astype(BF16)
    ucat = jnp.concatenate(
        [jnp.concatenate([u_ref[b, s, 0] for s in range(CHUNK)], axis=1) for b in range(bsz)], axis=0)
    xr = x_ref[:, 0].reshape(bsz * nc, wid).astype(BF16)
    xi = x_ref[:, 1].reshape(bsz * nc, wid).astype(BF16)
    for t in range(CHUNK):
        wr = jnp.dot(cl_ref[t, 0].astype(BF16), ep_ref[...], preferred_element_type=F32)
        wi = jnp.dot(cl_ref[t, 1].astype(BF16), ep_ref[...], preferred_element_type=F32)
        wr = jnp.where(mask_c, wr, 0.0).astype(BF16)
        wi = jnp.where(mask_c, wi, 0.0).astype(BF16)
        y = (jnp.dot(ucat[:, :(t + 1) * LANE_TILE], tr_ref[(CHUNK - 1 - t) * LANE_TILE:, :],
                     preferred_element_type=F32)
             + lax.dot_general(xr, wr, nt, preferred_element_type=F32)
             + lax.dot_general(xi, wi, nt, preferred_element_type=F32))
        for b in range(bsz):
            o_ref[b, t, 0] = y[b * nc:(b + 1) * nc].astype(BF16)


def _lane_replicator(p, wid):
    return (jnp.arange(wid)[None, :] % p == jnp.arange(p)[:, None]).astype(BF16)


def s5_outputs(u_tm, kx2, cl2, x):
    bsz, _, ngt, nc, _ = u_tm.shape
    wid = x.shape[3] // ngt
    p = cl2.shape[3]
    h = LANE_TILE // CHUNK
    r = jnp.arange(LANE_TILE)
    e = ((r[None, :, None] // h == jnp.arange(CHUNK)[:, None, None])
         & (r[None, :, None] % h == r[None, None, :] % h)).astype(BF16)
    tile = pl.BlockSpec((bsz, CHUNK, 1, nc, LANE_TILE), lambda gt: (0, 0, gt, 0, 0))
    return pl.pallas_call(
        _s5out_kernel,
        grid=(ngt,),
        in_specs=[tile,
                  pl.BlockSpec((LANE_TILE, LANE_TILE), lambda gt: (gt, 0)),
                  pl.BlockSpec((CHUNK, 2, LANE_TILE, p), lambda gt: (0, 0, gt, 0)),
                  pl.BlockSpec((bsz, 2, nc, wid), lambda gt: (0, 0, 0, gt)),
                  pl.BlockSpec((CHUNK, LANE_TILE, LANE_TILE), lambda gt: (0, 0, 0)),
                  pl.BlockSpec((p, wid), lambda gt: (0, 0))],
        out_specs=tile,
        out_shape=jax.ShapeDtypeStruct(u_tm.shape, BF16),
        scratch_shapes=[pltpu.VMEM((CHUNK * LANE_TILE, LANE_TILE), BF16)],
        compiler_params=_cparams(1),
        name="s5out",
    )(u_tm, kx2, cl2, x, e, _lane_replicator(p, wid))


def _attn_kernel(q_ref, k_ref, v_ref, tri_ref, o_ref, vcat_ref, *, blk, nh, unroll):
    qi = pl.program_id(2)
    q = q_ref[0]
    lane = lax.broadcasted_iota(jnp.int32, q.shape, 1)
    heads = [(lane >= HEAD_DIM * h) & (lane < HEAD_DIM * (h + 1)) for h in range(nh)]
    zero = jnp.zeros_like(q)

    @pl.when(qi == 0)
    def _():
        def fill(j, _):
            vb = v_ref[0, pl.ds(pl.multiple_of(j * blk, blk), blk), :]
            vcat_ref[j] = jnp.concatenate([jnp.where(m, vb, zero) for m in heads], axis=0)
            return 0

        lax.fori_loop(0, vcat_ref.shape[0], fill, 0)

    qs = jnp.concatenate([jnp.where(m, q, zero) for m in heads], axis=0)
    row = lax.broadcasted_iota(jnp.int32, (blk, blk), 0)
    col = lax.broadcasted_iota(jnp.int32, (blk, blk), 1)
    past = jnp.concatenate([col < row] * nh, axis=0)
    nt = (((1,), (1,)), ((), ()))

    def block(kj, carry, acc, diag):
        kb = k_ref[0, pl.ds(pl.multiple_of(kj * blk, blk), blk), :]
        z = lax.dot_general(qs, kb, nt, preferred_element_type=F32)
        sp = jnp.maximum(z, 0.0) + jnp.log2(1.0 + jnp.exp2(-jnp.abs(z)))
        if diag:
            sp = jnp.where(past, sp, 0.0)
        cs = jnp.dot(sp.astype(BF16), tri_ref[...], preferred_element_type=F32)
        w = jnp.exp2(((z - sp) - cs) - carry)
        if diag:
            w = jnp.where(past, w, 0.0)
        wb = w.astype(BF16)
        wcat = jnp.concatenate([wb[h * blk:(h + 1) * blk] for h in range(nh)], axis=1)
        acc = acc + jnp.dot(wcat, vcat_ref[kj], preferred_element_type=F32)
        return carry + (cs[:, 0:1] + sp[:, 0:1]), acc

    def head(r):
        def run():
            c = block(qi, jnp.zeros((nh * blk, 1), F32), jnp.zeros(q.shape, F32), True)
            for i in range(r):
                c = block(qi - 1 - i, c[0], c[1], False)
            return c
        return run

    rem = qi % unroll
    c = lax.switch(rem, [head(r) for r in range(unroll)])

    def group(i, c):
        for r in range(unroll):
            c = block(qi - rem - 1 - r - unroll * i, c[0], c[1], False)
        return c

    c = lax.fori_loop(0, qi // unroll, group, c)
    o_ref[0] = c[1].astype(BF16)


def stick_attention(qkv3, width, blk=256, nh=4, unroll=4):
    bsz, seqlen, _ = qkv3.shape
    wl = nh * HEAD_DIM
    ngrp = width // wl
    ii = jnp.arange(blk)
    tri = (ii[:, None] > ii[None, :]).astype(BF16)
    return pl.pallas_call(
        functools.partial(_attn_kernel, blk=blk, nh=nh, unroll=unroll),
        grid=(bsz, ngrp, seqlen // blk),
        in_specs=[pl.BlockSpec((1, blk, wl), lambda b, h, i: (b, i, h)),
                  pl.BlockSpec((1, seqlen, wl), lambda b, h, i: (b, 0, ngrp + h)),
                  pl.BlockSpec((1, seqlen, wl), lambda b, h, i: (b, 0, 2 * ngrp + h)),
                  pl.BlockSpec((blk, blk), lambda b, h, i: (0, 0))],
        out_specs=pl.BlockSpec((1, blk, wl), lambda b, h, i: (b, i, h)),
        out_shape=jax.ShapeDtypeStruct((bsz, seqlen, width), BF16),
        scratch_shapes=[pltpu.VMEM((seqlen // blk, nh * blk, wl), BF16)],
        compiler_params=_cparams(3),
        name="attn",
    )(qkv3, qkv3, qkv3, tri)


def _out_kernel(y_ref, a_ref, x_ref, mod_ref, wglu_ref, gs_ref, ga_ref, w0_ref, w1_ref, gf_ref,
                h_ref, xf_ref, ys_ref, *, rows):
    cpr = rows // CHUNK
    for r0 in range(0, a_ref.shape[0], rows):
        rs = slice(r0, r0 + rows)
        c0 = r0 // CHUNK
        nk = ys_ref.shape[0]
        for s in range(CHUNK):
            for k in range(nk):
                ys_ref[k, pl.ds(s, cpr, stride=CHUNK), :] = (
                    y_ref[0, s, k // 2, c0:c0 + cpr, (k % 2) * 128:(k % 2 + 1) * 128].astype(F32))
        y = jax.nn.gelu(jnp.concatenate([ys_ref[k] for k in range(nk)], axis=1))
        z = jnp.dot(y.astype(BF16), wglu_ref[...], preferred_element_type=F32)
        ys = y * jax.nn.sigmoid(z)
        ysn = (_rms(ys) * gs_ref[...]).astype(BF16)
        an = (_rms(a_ref[rs, :].astype(F32)) * ga_ref[...]).astype(BF16)
        o = (jnp.dot(ysn, w0_ref[...], preferred_element_type=F32)
             + jnp.dot(an, w1_ref[...], preferred_element_type=F32))
        h = x_ref[rs, :] + mod_ref[0, 2:3, :] * o
        h_ref[rs, :] = h
        xn = _rms(h) * gf_ref[...]
        xf_ref[rs, :] = (xn * (1.0 + mod_ref[0, 4:5, :]) + mod_ref[0, 3:4, :]).astype(BF16)


def out_proj(y_tm, a2, x2, mod3, w_glu_bf, g_ssm, g_attn, w_out_bf, g_ffn, seqlen, tm=512, rows=256):
    t, d = x2.shape
    ngt = y_tm.shape[2]
    ws = ngt * LANE_TILE
    wa = a2.shape[1]
    tpb = seqlen // tm
    const = lambda shape, r=0: pl.BlockSpec(shape, lambda i: (r, 0), pipeline_mode=pl.Buffered(1))
    return pl.pallas_call(
        functools.partial(_out_kernel, rows=rows),
        grid=(t // tm,),
        in_specs=[pl.BlockSpec((1, CHUNK, ngt, tm // CHUNK, LANE_TILE), lambda i: (i // tpb, 0, 0, i % tpb, 0)),
                  pl.BlockSpec((tm, wa), lambda i: (i, 0)),
                  pl.BlockSpec((tm, d), lambda i: (i, 0)),
                  pl.BlockSpec((1, 6, d), lambda i: (i // tpb, 0, 0)),
                  const((ws, ws)), const((1, ws)), const((1, wa)),
                  const((ws, d)), const((wa, d), 1), const((1, d))],
        out_specs=[pl.BlockSpec((tm, d), lambda i: (i, 0)),
                   pl.BlockSpec((tm, d), lambda i: (i, 0))],
        out_shape=[jax.ShapeDtypeStruct((t, d), F32), jax.ShapeDtypeStruct((t, d), BF16)],
        scratch_shapes=[pltpu.VMEM((ws // 128, rows, 128), F32)],
        compiler_params=_cparams(1),
        name="outproj",
    )(y_tm, a2, x2, mod3, w_glu_bf, g_ssm.reshape(1, ws), g_attn.reshape(1, wa), w_out_bf, w_out_bf,
      g_ffn.reshape(1, d))


def _ffn_kernel(xf_ref, h_ref, mod_ref, wg_ref, wu_ref, wd_ref, o_ref):
    j = pl.program_id(1)

    @pl.when(j == 0)
    def _():
        o_ref[...] = jnp.zeros_like(o_ref)

    xf = xf_ref[...]
    g = jnp.dot(xf, wg_ref[...].astype(BF16), preferred_element_type=F32)
    u = jnp.dot(xf, wu_ref[...].astype(BF16), preferred_element_type=F32)
    hm = (jax.nn.silu(g) * u).astype(BF16)
    o_ref[...] += jnp.dot(hm, wd_ref[...].astype(BF16), preferred_element_type=F32)

    @pl.when(j == pl.num_programs(1) - 1)
    def _():
        o_ref[...] = h_ref[...] + mod_ref[0, 5:6, :] * o_ref[...]


def ffn(xf2, h2, mod3, w_gate, w_up, w_down, seqlen, tm=1024, tf=256):
    t, d = h2.shape
    dff = w_gate.shape[1]
    row = lambda dt: pl.BlockSpec((tm, d), lambda i, j: (i, 0), pipeline_mode=pl.Buffered(1))
    return pl.pallas_call(
        _ffn_kernel,
        grid=(t // tm, dff // tf),
        in_specs=[row(BF16), row(F32),
                  pl.BlockSpec((1, 6, d), lambda i, j: ((i * tm) // seqlen, 0, 0)),
                  pl.BlockSpec((d, tf), lambda i, j: (0, j)),
                  pl.BlockSpec((d, tf), lambda i, j: (0, j)),
                  pl.BlockSpec((tf, d), lambda i, j: (j, 0))],
        out_specs=pl.BlockSpec((tm, d), lambda i, j: (i, 0)),
        out_shape=jax.ShapeDtypeStruct((t, d), F32),
        compiler_params=_cparams(2),
        name="ffn",
    )(xf2, h2, mod3, w_gate, w_up, w_down)


def s5_mixer_chunked(u_tm, a_re, a_im, log_dt, b_re, b_im, c_re, c_im, d_skip):
    bsz, _, ngt, nc, _ = u_tm.shape
    g, p = a_re.shape
    h = c_re.shape[1]
    assert h * CHUNK == LANE_TILE and g * h == ngt * LANE_TILE
    bl, cl, kx, a = s5_prep(a_re, a_im, log_dt, b_re, b_im, c_re, c_im, d_skip)
    s = s5_chunk_inputs(u_tm, bl.reshape(CHUNK, 2, g * h, p), h)
    x = s5_chunk_scan(s, a.reshape(2, 1, g * p))
    return s5_outputs(u_tm, kx.reshape(g * h, CHUNK * h), cl.reshape(CHUNK, 2, g * h, p), x)


def kernel(x, c, w_ada, b_ada, g_mix, w_in, a_re, a_im, log_dt, b_re, b_im, c_re, c_im, d_skip,
           w_glu, q_gain, k_gain, g_ssm_out, g_attn_out, w_out, g_ffn, w_gate, w_up, w_down):
    bsz, seqlen, d = x.shape
    depth = w_ada.shape[0]
    width = w_glu.shape[1]
    h2 = x.reshape(bsz * seqlen, d)
    for l in range(depth):
        mod3 = ada_mod(c, w_ada[l], b_ada[l]).reshape(bsz, 6, d)
        u_tm, qkv = in_proj(h2, mod3, g_mix[l], w_in[l].astype(BF16), q_gain[l], k_gain[l], bsz, seqlen)
        y_tm = s5_mixer_chunked(u_tm, a_re[l], a_im[l], log_dt[l], b_re[l], b_im[l],
                                c_re[l], c_im[l], d_skip[l])
        att = stick_attention(qkv.reshape(bsz, seqlen, 3 * width), width)
        h2, xf = out_proj(y_tm, att.reshape(bsz * seqlen, width), h2, mod3,
                          w_glu[l].astype(BF16), g_ssm_out[l], g_attn_out[l], w_out[l].astype(BF16),
                          g_ffn[l], seqlen)
        h2 = ffn(xf, h2, mod3, w_gate[l], w_up[l], w_down[l], seqlen)
    return h2.reshape(bsz, seqlen, d)
```

```python
import functools

import jax
import jax.numpy as jnp
from jax import lax
from jax.experimental import pallas as pl
from jax.experimental.pallas import tpu as pltpu

F32 = jnp.float32
BF16 = jnp.bfloat16
EPS = 1e-6
HEAD_DIM = 64
CHUNK = 16
LANE_TILE = 256
VMEM_LIMIT = 56 * 1024 * 1024
Q_SCALE = HEAD_DIM ** -0.5 * 1.4426950408889634


def _cparams(n_axes, vmem=VMEM_LIMIT):
    return pltpu.CompilerParams(dimension_semantics=("arbitrary",) * n_axes,
                                vmem_limit_bytes=vmem)


def _rms(x):
    return x * lax.rsqrt(jnp.mean(x * x, axis=-1, keepdims=True) + EPS)


def _ada_kernel(c_ref, w_ref, b_ref, o_ref):
    d, tn = w_ref.shape
    rows = 256
    for b in range(c_ref.shape[0]):
        acc = jnp.zeros((8, tn), F32)
        for kc in range(d // rows):
            cond = jax.nn.silu(c_ref[b, kc * rows:(kc + 1) * rows, :])
            blk = cond * w_ref[kc * rows:(kc + 1) * rows, :]
            acc = acc + jnp.sum(blk.reshape(rows // 8, 8, tn), axis=0)
        o_ref[b:b + 1, :] = jnp.sum(acc, axis=0, keepdims=True) + b_ref[...]


def ada_mod(c, w_ada, b_ada, tn=1024):
    bsz, d = c.shape
    n = w_ada.shape[1]
    return pl.pallas_call(
        _ada_kernel,
        grid=(n // tn,),
        in_specs=[pl.BlockSpec((bsz, d, 1), lambda j: (0, 0, 0)),
                  pl.BlockSpec((d, tn), lambda j: (0, j)),
                  pl.BlockSpec((1, tn), lambda j: (0, j))],
        out_specs=pl.BlockSpec((bsz, tn), lambda j: (0, j)),
        out_shape=jax.ShapeDtypeStruct((bsz, n), F32),
        compiler_params=_cparams(1),
        name="ada",
    )(c.reshape(bsz, d, 1), w_ada, b_ada.reshape(1, n))


def _head_rmsnorm(r, ones_ref):
    outs = []
    for s in range(r.shape[1] // LANE_TILE):
        rs = r[:, s * LANE_TILE:(s + 1) * LANE_TILE]
        ss = jnp.dot((rs * rs).astype(BF16), ones_ref[...], preferred_element_type=F32)
        outs.append(rs * lax.rsqrt(ss * (1.0 / HEAD_DIM) + EPS))
    return jnp.concatenate(outs, axis=1)


def _proj_kernel(x_ref, mod_ref, g_ref, w_ref, qg_ref, kg_ref, ones_ref, u_ref, o_ref, us_ref, *, rows):
    tm = x_ref.shape[0]
    tn = w_ref.shape[1] // 4
    cpr = rows // CHUNK
    for r0 in range(0, tm, rows):
        xn = _rms(x_ref[r0:r0 + rows, :]) * g_ref[...]
        xm = (xn * (1.0 + mod_ref[0, 1:2, :]) + mod_ref[0, 0:1, :]).astype(BF16)
        for j in range(4):
            r = jnp.dot(xm, w_ref[:, j * tn:(j + 1) * tn], preferred_element_type=F32)
            if j == 0:
                c0 = r0 // CHUNK
                for k in range(tn // 128):
                    us_ref[k] = r[:, k * 128:(k + 1) * 128]
                for s in range(CHUNK):
                    for k in range(tn // 128):
                        us = us_ref[k, pl.ds(s, cpr, stride=CHUNK), :]
                        u_ref[0, s, k // 2, c0:c0 + cpr, (k % 2) * 128:(k % 2 + 1) * 128] = us.astype(BF16)
                continue
            if j == 1:
                r = (_head_rmsnorm(r, ones_ref) * qg_ref[...]) * Q_SCALE
            elif j == 2:
                r = _head_rmsnorm(r, ones_ref) * kg_ref[...]
            o_ref[r0:r0 + rows, (j - 1) * tn:j * tn] = r.astype(BF16)


def in_proj(x2, mod3, g_mix, w_in_bf, q_gain, k_gain, bsz, seqlen, tm=512, rows=256):
    t, d = x2.shape
    n = w_in_bf.shape[1]
    tn = n // 4
    reps = tn // HEAD_DIM
    qg = jnp.tile(q_gain.reshape(1, HEAD_DIM), (1, reps))
    kg = jnp.tile(k_gain.reshape(1, HEAD_DIM), (1, reps))
    idx = jnp.arange(LANE_TILE) // HEAD_DIM
    ones = (idx[:, None] == idx[None, :]).astype(BF16)
    const = lambda shape: pl.BlockSpec(shape, lambda i: (0, 0), pipeline_mode=pl.Buffered(1))
    tpb = seqlen // tm
    ngt = tn // LANE_TILE
    return pl.pallas_call(
        functools.partial(_proj_kernel, rows=rows),
        grid=(t // tm,),
        in_specs=[pl.BlockSpec((tm, d), lambda i: (i, 0)),
                  pl.BlockSpec((1, 6, d), lambda i: (i // tpb, 0, 0)),
                  const((1, d)), const((d, n)), const((1, tn)), const((1, tn)),
                  const((LANE_TILE, LANE_TILE))],
        out_specs=[pl.BlockSpec((1, CHUNK, ngt, tm // CHUNK, LANE_TILE), lambda i: (i // tpb, 0, 0, i % tpb, 0)),
                   pl.BlockSpec((tm, 3 * tn), lambda i: (i, 0))],
        out_shape=[jax.ShapeDtypeStruct((bsz, CHUNK, ngt, seqlen // CHUNK, LANE_TILE), BF16),
                   jax.ShapeDtypeStruct((t, 3 * tn), BF16)],
        scratch_shapes=[pltpu.VMEM((tn // 128, rows, 128), F32)],
        compiler_params=_cparams(1),
        name="proj",
    )(x2, mod3, g_mix.reshape(1, d), w_in_bf, qg, kg, ones)


def _zoh(ar, ai, dt):
    mag = jnp.exp(ar * dt)
    th = ai * dt
    lbr, lbi = mag * jnp.cos(th), mag * jnp.sin(th)
    den = ar * ar + ai * ai
    nr, ni = lbr - 1.0, lbi
    return lbr, lbi, (nr * ar + ni * ai) / den, (ni * ar - nr * ai) / den


def _s5prep_kernel(ar_ref, ai_ref, ldt_ref, btr_ref, bti_ref, cr_ref, ci_ref, d_ref,
                   bl_ref, cl_ref, kx_ref, a_ref):
    dt = jnp.exp(ldt_ref[...])
    lbr, lbi, kr, ki = _zoh(ar_ref[...], ai_ref[...], dt)
    btr, bti = btr_ref[...], bti_ref[...]
    bbr = kr * btr - ki * bti
    bbi = kr * bti + ki * btr
    c_re, c_im = cr_ref[...], ci_ref[...]
    h = c_re.shape[1]
    eye = (lax.broadcasted_iota(jnp.int32, (1, h, h), 1) ==
           lax.broadcasted_iota(jnp.int32, (1, h, h), 2))
    dn = (((2,), (2,)), ((0,), (0,)))
    pr, pi = jnp.ones_like(lbr), jnp.zeros_like(lbr)
    blrs, blis = [], []
    for j in range(CHUNK):
        blr = pr * bbr - pi * bbi
        bli = pr * bbi + pi * bbr
        bl_ref[j, 0] = blr
        bl_ref[j, 1] = bli
        blrs.append(blr)
        blis.append(bli)
        pr, pi = pr * lbr - pi * lbi, pr * lbi + pi * lbr
        cl_ref[j, 0] = c_re * pr - c_im * pi
        cl_ref[j, 1] = -(c_re * pi + c_im * pr)
    a_ref[0] = pr
    a_ref[1] = pi
    kx = (lax.dot_general(jnp.concatenate(blrs, axis=1), c_re, dn, precision=lax.Precision.HIGHEST,
                          preferred_element_type=F32)
          - lax.dot_general(jnp.concatenate(blis, axis=1), c_im, dn, precision=lax.Precision.HIGHEST,
                            preferred_element_type=F32))
    for j in range(CHUNK):
        kxj = kx[:, j * h:(j + 1) * h, :]
        if j == 0:
            kxj = kxj + jnp.where(eye, d_ref[...], 0.0)
        kx_ref[:, :, j * h:(j + 1) * h] = kxj


def s5_prep(a_re, a_im, log_dt, b_re, b_im, c_re, c_im, d_skip, gb=8):
    g, p = a_re.shape
    h = c_re.shape[1]
    vec = pl.BlockSpec((gb, 1, p), lambda i: (i, 0, 0))
    mat = pl.BlockSpec((gb, h, p), lambda i: (i, 0, 0))
    lag = pl.BlockSpec((CHUNK, 2, gb, h, p), lambda i: (0, 0, i, 0, 0))
    lag_shape = jax.ShapeDtypeStruct((CHUNK, 2, g, h, p), F32)
    return pl.pallas_call(
        _s5prep_kernel,
        grid=(g // gb,),
        in_specs=[vec, vec, pl.BlockSpec((gb, 1, 1), lambda i: (i, 0, 0)),
                  mat, mat, mat, mat, pl.BlockSpec((gb, 1, h), lambda i: (i, 0, 0))],
        out_specs=[lag, lag,
                   pl.BlockSpec((gb, h, CHUNK * h), lambda i: (i, 0, 0)),
                   pl.BlockSpec((2, gb, 1, p), lambda i: (0, i, 0, 0))],
        out_shape=[lag_shape, lag_shape,
                   jax.ShapeDtypeStruct((g, h, CHUNK * h), F32),
                   jax.ShapeDtypeStruct((2, g, 1, p), F32)],
        compiler_params=_cparams(1),
        name="s5prep",
    )(a_re.reshape(g, 1, p), a_im.reshape(g, 1, p), log_dt.reshape(g, 1, 1),
      b_re.transpose(0, 2, 1), b_im.transpose(0, 2, 1), c_re, c_im, d_skip.reshape(g, 1, h))


def _group_mask(shape, row_div, col_div):
    return (lax.broadcasted_iota(jnp.int32, shape, 0) // row_div ==
            lax.broadcasted_iota(jnp.int32, shape, 1) // col_div)


def _s5in_kernel(u_ref, bl_ref, e_ref, o_ref, w_ref):
    p, wid = e_ref.shape
    h = LANE_TILE // (wid // p)
    mask = _group_mask((LANE_TILE, wid), h, p)
    for s in range(CHUNK):
        ex = jnp.dot(bl_ref[CHUNK - 1 - s, 0].astype(BF16), e_ref[...], preferred_element_type=F32)
        w_ref[s * LANE_TILE:(s + 1) * LANE_TILE, :] = jnp.where(mask, ex, 0.0).astype(BF16)
    for b in range(u_ref.shape[0]):
        ucat = jnp.concatenate([u_ref[b, s, 0] for s in range(CHUNK)], axis=1)
        o_ref[b, 0] = jnp.dot(ucat, w_ref[...], preferred_element_type=F32)


def s5_chunk_inputs(u_tm, bl2, h):
    bsz, _, ngt, nc, _ = u_tm.shape
    p = bl2.shape[3]
    wid = (LANE_TILE // h) * p
    e = _lane_replicator(p, wid)
    return pl.pallas_call(
        _s5in_kernel,
        grid=(ngt, 2),
        in_specs=[pl.BlockSpec((bsz, CHUNK, 1, nc, LANE_TILE), lambda gt, ri: (0, 0, gt, 0, 0)),
                  pl.BlockSpec((CHUNK, 1, LANE_TILE, p), lambda gt, ri: (0, ri, gt, 0)),
                  pl.BlockSpec((p, wid), lambda gt, ri: (0, 0))],
        out_specs=pl.BlockSpec((bsz, 1, nc, wid), lambda gt, ri: (0, ri, 0, gt)),
        out_shape=jax.ShapeDtypeStruct((bsz, 2, nc, ngt * wid), F32),
        scratch_shapes=[pltpu.VMEM((CHUNK * LANE_TILE, wid), BF16)],
        compiler_params=_cparams(2),
        name="s5in",
    )(u_tm, bl2, e)


def _s5scan_kernel(s_ref, a_ref, x_ref):
    nc, w = s_ref.shape[2], s_ref.shape[3]
    a1r, a1i = a_ref[0], a_ref[1]

    def cmul(xr, xi, yr, yi):
        return xr * yr - xi * yi, xr * yi + xi * yr

    row = lax.broadcasted_iota(jnp.int32, (8, w), 0)
    pows = [(a1r, a1i)]
    for _ in range(7):
        pows.append(cmul(pows[-1][0], pows[-1][1], a1r, a1i))
    steps = []
    for k in (1, 2, 4):
        steps.append((k, jnp.where(row >= k, pows[k - 1][0], 0.0), jnp.where(row >= k, pows[k - 1][1], 0.0)))
    pcr = jnp.zeros((8, w), F32)
    pci = jnp.zeros((8, w), F32)
    for r in range(8):
        pcr = jnp.where(row == r, pows[r][0], pcr)
        pci = jnp.where(row == r, pows[r][1], pci)

    def body(blk, carry):
        cr, ci = carry
        off = pl.multiple_of(blk * 8, 8)
        xr = s_ref[0, 0, pl.ds(off, 8), :]
        xi = s_ref[0, 1, pl.ds(off, 8), :]
        for k, kr, ki in steps:
            sr, si = pltpu.roll(xr, k, 0), pltpu.roll(xi, k, 0)
            dr, di = cmul(sr, si, kr, ki)
            xr, xi = xr + dr, xi + di
        dr, di = cmul(jnp.broadcast_to(cr, (8, w)), jnp.broadcast_to(ci, (8, w)), pcr, pci)
        xr, xi = xr + dr, xi + di
        x_ref[0, 0, pl.ds(off, 8), :] = jnp.where(row == 0, cr, pltpu.roll(xr, 1, 0))
        x_ref[0, 1, pl.ds(off, 8), :] = jnp.where(row == 0, ci, pltpu.roll(xi, 1, 0))
        return xr[7:8, :], xi[7:8, :]

    zero = jnp.zeros((1, w), F32)
    lax.fori_loop(0, nc // 8, body, (zero, zero))


def s5_chunk_scan(s, a, w=1024):
    bsz, _, nc, tot = s.shape
    blk = pl.BlockSpec((1, 2, nc, w), lambda b, l: (b, 0, 0, l))
    return pl.pallas_call(
        _s5scan_kernel,
        grid=(bsz, tot // w),
        in_specs=[blk, pl.BlockSpec((2, 1, w), lambda b, l: (0, 0, l))],
        out_specs=blk,
        out_shape=jax.ShapeDtypeStruct(s.shape, F32),
        compiler_params=_cparams(2),
        name="s5scan",
    )(s, a)


def _s5out_kernel(u_ref, kx_ref, cl_ref, x_ref, e_ref, ep_ref, o_ref, tr_ref):
    bsz, _, _, nc, _ = u_ref.shape
    p, wid = ep_ref.shape
    h = LANE_TILE // CHUNK
    mask_t = _group_mask((LANE_TILE, LANE_TILE), h, h)
    mask_c = _group_mask((LANE_TILE, wid), h, p)
    nt = (((1,), (1,)), ((), ()))
    kx = kx_ref[...].astype(BF16)
    for j in range(CHUNK):
        tj = jnp.where(mask_t, jnp.dot(kx, e_ref[j], preferred_element_type=F32), 0.0)
        tr_ref[(CHUNK - 1 - j) * LANE_TILE:(CHUNK - j) * LANE_TILE, :] = tj.astype(BF16)
    ucat = jnp.concatenate(
        [jnp.concatenate([u_ref[b, s, 0] for s in range(CHUNK)], axis=1) for b in range(bsz)], axis=0)
    xr = x_ref[:, 0].reshape(bsz * nc, wid).astype(BF16)
    xi = x_ref[:, 1].reshape(bsz * nc, wid).astype(BF16)
    for t in range(CHUNK):
        wr = jnp.dot(cl_ref[t, 0].astype(BF16), ep_ref[...], preferred_element_type=F32)
        wi = jnp.dot(cl_ref[t, 1].astype(BF16), ep_ref[...], preferred_element_type=F32)
        wr = jnp.where(mask_c, wr, 0.0).astype(BF16)
        wi = jnp.where(mask_c, wi, 0.0).astype(BF16)
        y = (jnp.dot(ucat[:, :(t + 1) * LANE_TILE], tr_ref[(CHUNK - 1 - t) * LANE_TILE:, :],
                     preferred_element_type=F32)
             + lax.dot_general(xr, wr, nt, preferred_element_type=F32)
             + lax.dot_general(xi, wi, nt, preferred_element_type=F32))
        for b in range(bsz):
            o_ref[b, t, 0] = y[b * nc:(b + 1) * nc].astype(BF16)


def _lane_replicator(p, wid):
    return (jnp.arange(wid)[None, :] % p == jnp.arange(p)[:, None]).astype(BF16)


def s5_outputs(u_tm, kx2, cl2, x):
    bsz, _, ngt, nc, _ = u_tm.shape
    wid = x.shape[3] // ngt
    p = cl2.shape[3]
    h = LANE_TILE // CHUNK
    r = jnp.arange(LANE_TILE)
    e = ((r[None, :, None] // h == jnp.arange(CHUNK)[:, None, None])
         & (r[None, :, None] % h == r[None, None, :] % h)).astype(BF16)
    tile = pl.BlockSpec((bsz, CHUNK, 1, nc, LANE_TILE), lambda gt: (0, 0, gt, 0, 0))
    return pl.pallas_call(
        _s5out_kernel,
        grid=(ngt,),
        in_specs=[tile,
                  pl.BlockSpec((LANE_TILE, LANE_TILE), lambda gt: (gt, 0)),
                  pl.BlockSpec((CHUNK, 2, LANE_TILE, p), lambda gt: (0, 0, gt, 0)),
                  pl.BlockSpec((bsz, 2, nc, wid), lambda gt: (0, 0, 0, gt)),
                  pl.BlockSpec((CHUNK, LANE_TILE, LANE_TILE), lambda gt: (0, 0, 0)),
                  pl.BlockSpec((p, wid), lambda gt: (0, 0))],
        out_specs=tile,
        out_shape=jax.ShapeDtypeStruct(u_tm.shape, BF16),
        scratch_shapes=[pltpu.VMEM((CHUNK * LANE_TILE, LANE_TILE), BF16)],
        compiler_params=_cparams(1),
        name="s5out",
    )(u_tm, kx2, cl2, x, e, _lane_replicator(p, wid))


def _attn_kernel(q_ref, k_ref, v_ref, tri_ref, o_ref, vcat_ref, *, blk, nh, unroll):
    qi = pl.program_id(2)
    q = q_ref[0]
    lane = lax.broadcasted_iota(jnp.int32, q.shape, 1)
    heads = [(lane >= HEAD_DIM * h) & (lane < HEAD_DIM * (h + 1)) for h in range(nh)]
    zero = jnp.zeros_like(q)

    @pl.when(qi == 0)
    def _():
        def fill(j, _):
            vb = v_ref[0, pl.ds(pl.multiple_of(j * blk, blk), blk), :]
            vcat_ref[j] = jnp.concatenate([jnp.where(m, vb, zero) for m in heads], axis=0)
            return 0

        lax.fori_loop(0, vcat_ref.shape[0], fill, 0)

    qs = jnp.concatenate([jnp.where(m, q, zero) for m in heads], axis=0)
    row = lax.broadcasted_iota(jnp.int32, (blk, blk), 0)
    col = lax.broadcasted_iota(jnp.int32, (blk, blk), 1)
    past = jnp.concatenate([col < row] * nh, axis=0)
    nt = (((1,), (1,)), ((), ()))

    def block(kj, carry, acc, diag):
        kb = k_ref[0, pl.ds(pl.multiple_of(kj * blk, blk), blk), :]
        z = lax.dot_general(qs, kb, nt, preferred_element_type=F32)
        sp = jnp.maximum(z, 0.0) + jnp.log2(1.0 + jnp.exp2(-jnp.abs(z)))
        if diag:
            sp = jnp.where(past, sp, 0.0)
        cs = lax.dot_general(sp, tri_ref[...], (((1,), (0,)), ((), ())), preferred_element_type=F32)
        w = jnp.exp2(((z - sp) - cs) - carry)
        if diag:
            w = jnp.where(past, w, 0.0)
        wcat = jnp.concatenate([w[h * blk:(h + 1) * blk] for h in range(nh)], axis=1)
        acc = acc + lax.dot_general(wcat, vcat_ref[kj], (((1,), (0,)), ((), ())), preferred_element_type=F32)
        return carry + (cs[:, 0:1] + sp[:, 0:1]), acc

    def head(r):
        def run():
            c = block(qi, jnp.zeros((nh * blk, 1), F32), jnp.zeros(q.shape, F32), True)
            for i in range(r):
                c = block(qi - 1 - i, c[0], c[1], False)
            return c
        return run

    rem = qi % unroll
    c = lax.switch(rem, [head(r) for r in range(unroll)])

    left = qi - rem
    for n in (2 * unroll, unroll):
        def group(i, c, n=n, left=left):
            for r in range(n):
                c = block(left - 1 - r - n * i, c[0], c[1], False)
            return c

        c = lax.fori_loop(0, left // n, group, c)
        left = left % n
    o_ref[0] = c[1].astype(BF16)


def stick_attention(qkv3, width, blk=256, nh=4, unroll=4):
    bsz, seqlen, _ = qkv3.shape
    wl = nh * HEAD_DIM
    ngrp = width // wl
    ii = jnp.arange(blk)
    tri = (ii[:, None] > ii[None, :]).astype(BF16)
    return pl.pallas_call(
        functools.partial(_attn_kernel, blk=blk, nh=nh, unroll=unroll),
        grid=(bsz, ngrp, seqlen // blk),
        in_specs=[pl.BlockSpec((1, blk, wl), lambda b, h, i: (b, i, h)),
                  pl.BlockSpec((1, seqlen, wl), lambda b, h, i: (b, 0, ngrp + h)),
                  pl.BlockSpec((1, seqlen, wl), lambda b, h, i: (b, 0, 2 * ngrp + h)),
                  pl.BlockSpec((blk, blk), lambda b, h, i: (0, 0))],
        out_specs=pl.BlockSpec((1, blk, wl), lambda b, h, i: (b, i, h)),
        out_shape=jax.ShapeDtypeStruct((bsz, seqlen, width), BF16),
        scratch_shapes=[pltpu.VMEM((seqlen // blk, nh * blk, wl), BF16)],
        compiler_params=_cparams(3),
        name="attn",
    )(qkv3, qkv3, qkv3, tri)


def _out_kernel(y_ref, a_ref, x_ref, mod_ref, wglu_ref, gs_ref, ga_ref, w0_ref, w1_ref, gf_ref,
                h_ref, xf_ref, ys_ref, *, rows):
    cpr = rows // CHUNK
    for r0 in range(0, a_ref.shape[0], rows):
        rs = slice(r0, r0 + rows)
        c0 = r0 // CHUNK
        nk = ys_ref.shape[0]
        for s in range(CHUNK):
            for k in range(nk):
                ys_ref[k, pl.ds(s, cpr, stride=CHUNK), :] = (
                    y_ref[0, s, k // 2, c0:c0 + cpr, (k % 2) * 128:(k % 2 + 1) * 128].astype(F32))
        y = jax.nn.gelu(jnp.concatenate([ys_ref[k] for k in range(nk)], axis=1))
        z = jnp.dot(y.astype(BF16), wglu_ref[...], preferred_element_type=F32)
        ys = y * jax.nn.sigmoid(z)
        ysn = (_rms(ys) * gs_ref[...]).astype(BF16)
        an = (_rms(a_ref[rs, :].astype(F32)) * ga_ref[...]).astype(BF16)
        o = (jnp.dot(ysn, w0_ref[...], preferred_element_type=F32)
             + jnp.dot(an, w1_ref[...], preferred_element_type=F32))
        h = x_ref[rs, :] + mod_ref[0, 2:3, :] * o
        h_ref[rs, :] = h
        xn = _rms(h) * gf_ref[...]
        xf_ref[rs, :] = (xn * (1.0 + mod_ref[0, 4:5, :]) + mod_ref[0, 3:4, :]).astype(BF16)


def out_proj(y_tm, a2, x2, mod3, w_glu_bf, g_ssm, g_attn, w_out_bf, g_ffn, seqlen, tm=512, rows=256):
    t, d = x2.shape
    ngt = y_tm.shape[2]
    ws = ngt * LANE_TILE
    wa = a2.shape[1]
    tpb = seqlen // tm
    const = lambda shape, r=0: pl.BlockSpec(shape, lambda i: (r, 0), pipeline_mode=pl.Buffered(1))
    return pl.pallas_call(
        functools.partial(_out_kernel, rows=rows),
        grid=(t // tm,),
        in_specs=[pl.BlockSpec((1, CHUNK, ngt, tm // CHUNK, LANE_TILE), lambda i: (i // tpb, 0, 0, i % tpb, 0)),
                  pl.BlockSpec((tm, wa), lambda i: (i, 0)),
                  pl.BlockSpec((tm, d), lambda i: (i, 0)),
                  pl.BlockSpec((1, 6, d), lambda i: (i // tpb, 0, 0)),
                  const((ws, ws)), const((1, ws)), const((1, wa)),
                  const((ws, d)), const((wa, d), 1), const((1, d))],
        out_specs=[pl.BlockSpec((tm, d), lambda i: (i, 0)),
                   pl.BlockSpec((tm, d), lambda i: (i, 0))],
        out_shape=[jax.ShapeDtypeStruct((t, d), F32), jax.ShapeDtypeStruct((t, d), BF16)],
        scratch_shapes=[pltpu.VMEM((ws // 128, rows, 128), F32)],
        compiler_params=_cparams(1),
        name="outproj",
    )(y_tm, a2, x2, mod3, w_glu_bf, g_ssm.reshape(1, ws), g_attn.reshape(1, wa), w_out_bf, w_out_bf,
      g_ffn.reshape(1, d))


def _ffn_kernel(xf_ref, h_ref, mod_ref, wg_ref, wu_ref, wd_ref, o_ref):
    j = pl.program_id(1)

    @pl.when(j == 0)
    def _():
        o_ref[...] = jnp.zeros_like(o_ref)

    xf = xf_ref[...]
    g = jnp.dot(xf, wg_ref[...].astype(BF16), preferred_element_type=F32)
    u = jnp.dot(xf, wu_ref[...].astype(BF16), preferred_element_type=F32)
    hm = (jax.nn.silu(g) * u).astype(BF16)
    o_ref[...] += jnp.dot(hm, wd_ref[...].astype(BF16), preferred_element_type=F32)

    @pl.when(j == pl.num_programs(1) - 1)
    def _():
        o_ref[...] = h_ref[...] + mod_ref[0, 5:6, :] * o_ref[...]


def ffn(xf2, h2, mod3, w_gate, w_up, w_down, seqlen, tm=1024, tf=256):
    t, d = h2.shape
    dff = w_gate.shape[1]
    row = lambda dt: pl.BlockSpec((tm, d), lambda i, j: (i, 0), pipeline_mode=pl.Buffered(1))
    return pl.pallas_call(
        _ffn_kernel,
        grid=(t // tm, dff // tf),
        in_specs=[row(BF16), row(F32),
                  pl.BlockSpec((1, 6, d), lambda i, j: ((i * tm) // seqlen, 0, 0)),
                  pl.BlockSpec((d, tf), lambda i, j: (0, j)),
                  pl.BlockSpec((d, tf), lambda i, j: (0, j)),
                  pl.BlockSpec((tf, d), lambda i, j: (j, 0))],
        out_specs=pl.BlockSpec((tm, d), lambda i, j: (i, 0)),
        out_shape=jax.ShapeDtypeStruct((t, d), F32),
        compiler_params=_cparams(2),
        name="ffn",
    )(xf2, h2, mod3, w_gate, w_up, w_down)


def s5_mixer_chunked(u_tm, a_re, a_im, log_dt, b_re, b_im, c_re, c_im, d_skip):
    bsz, _, ngt, nc, _ = u_tm.shape
    g, p = a_re.shape
    h = c_re.shape[1]
    assert h * CHUNK == LANE_TILE and g * h == ngt * LANE_TILE
    bl, cl, kx, a = s5_prep(a_re, a_im, log_dt, b_re, b_im, c_re, c_im, d_skip)
    s = s5_chunk_inputs(u_tm, bl.reshape(CHUNK, 2, g * h, p), h)
    x = s5_chunk_scan(s, a.reshape(2, 1, g * p))
    return s5_outputs(u_tm, kx.reshape(g * h, CHUNK * h), cl.reshape(CHUNK, 2, g * h, p), x)


def kernel(x, c, w_ada, b_ada, g_mix, w_in, a_re, a_im, log_dt, b_re, b_im, c_re, c_im, d_skip,
           w_glu, q_gain, k_gain, g_ssm_out, g_attn_out, w_out, g_ffn, w_gate, w_up, w_down):
    bsz, seqlen, d = x.shape
    depth = w_ada.shape[0]
    width = w_glu.shape[1]
    h2 = x.reshape(bsz * seqlen, d)
    for l in range(depth):
        mod3 = ada_mod(c, w_ada[l], b_ada[l]).reshape(bsz, 6, d)
        u_tm, qkv = in_proj(h2, mod3, g_mix[l], w_in[l].astype(BF16), q_gain[l], k_gain[l], bsz, seqlen)
        y_tm = s5_mixer_chunked(u_tm, a_re[l], a_im[l], log_dt[l], b_re[l], b_im[l],
                                c_re[l], c_im[l], d_skip[l])
        att = stick_attention(qkv.reshape(bsz, seqlen, 3 * width), width)
        h2, xf = out_proj(y_tm, att.reshape(bsz * seqlen, width), h2, mod3,
                          w_glu[l].astype(BF16), g_ssm_out[l], g_attn_out[l], w_out[l].astype(BF16),
                          g_ffn[l], seqlen)
        h2 = ffn(xf, h2, mod3, w_gate[l], w_up[l], w_down[l], seqlen)
    return h2.reshape(bsz, seqlen, d)
```

```python
import functools

import jax
import jax.numpy as jnp
from jax import lax
from jax.experimental import pallas as pl
from jax.experimental.pallas import tpu as pltpu

F32 = jnp.float32
BF16 = jnp.bfloat16
EPS = 1e-6
HEAD_DIM = 64
CHUNK = 16
LANE_TILE = 256
VMEM_LIMIT = 56 * 1024 * 1024
Q_SCALE = HEAD_DIM ** -0.5 * 1.4426950408889634


def _cparams(n_axes, vmem=VMEM_LIMIT):
    return pltpu.CompilerParams(dimension_semantics=("arbitrary",) * n_axes,
                                vmem_limit_bytes=vmem)


def _rms(x):
    return x * lax.rsqrt(jnp.mean(x * x, axis=-1, keepdims=True) + EPS)


def _ada_kernel(c_ref, w_ref, b_ref, o_ref):
    d, tn = w_ref.shape
    rows = 256
    for b in range(c_ref.shape[0]):
        acc = jnp.zeros((8, tn), F32)
        for kc in range(d // rows):
            cond = jax.nn.silu(c_ref[b, kc * rows:(kc + 1) * rows, :])
            blk = cond * w_ref[kc * rows:(kc + 1) * rows, :]
            acc = acc + jnp.sum(blk.reshape(rows // 8, 8, tn), axis=0)
        o_ref[b:b + 1, :] = jnp.sum(acc, axis=0, keepdims=True) + b_ref[...]


def ada_mod(c, w_ada, b_ada, tn=1024):
    bsz, d = c.shape
    n = w_ada.shape[1]
    return pl.pallas_call(
        _ada_kernel,
        grid=(n // tn,),
        in_specs=[pl.BlockSpec((bsz, d, 1), lambda j: (0, 0, 0)),
                  pl.BlockSpec((d, tn), lambda j: (0, j)),
                  pl.BlockSpec((1, tn), lambda j: (0, j))],
        out_specs=pl.BlockSpec((bsz, tn), lambda j: (0, j)),
        out_shape=jax.ShapeDtypeStruct((bsz, n), F32),
        compiler_params=_cparams(1),
        name="ada",
    )(c.reshape(bsz, d, 1), w_ada, b_ada.reshape(1, n))


def _head_rmsnorm(r, ones_ref):
    outs = []
    for s in range(r.shape[1] // LANE_TILE):
        rs = r[:, s * LANE_TILE:(s + 1) * LANE_TILE]
        ss = jnp.dot((rs * rs).astype(BF16), ones_ref[...], preferred_element_type=F32)
        outs.append(rs * lax.rsqrt(ss * (1.0 / HEAD_DIM) + EPS))
    return jnp.concatenate(outs, axis=1)


def _proj_kernel(x_ref, mod_ref, g_ref, w_ref, qg_ref, kg_ref, ones_ref, u_ref, o_ref, us_ref, *, rows):
    tm = x_ref.shape[0]
    tn = w_ref.shape[1] // 4
    cpr = rows // CHUNK
    for r0 in range(0, tm, rows):
        xn = _rms(x_ref[r0:r0 + rows, :]) * g_ref[...]
        xm = (xn * (1.0 + mod_ref[0, 1:2, :]) + mod_ref[0, 0:1, :]).astype(BF16)
        for j in range(4):
            r = jnp.dot(xm, w_ref[:, j * tn:(j + 1) * tn], preferred_element_type=F32)
            if j == 0:
                c0 = r0 // CHUNK
                for k in range(tn // 128):
                    us_ref[k] = r[:, k * 128:(k + 1) * 128]
                for s in range(CHUNK):
                    for k in range(tn // 128):
                        us = us_ref[k, pl.ds(s, cpr, stride=CHUNK), :]
                        u_ref[0, s, k // 2, c0:c0 + cpr, (k % 2) * 128:(k % 2 + 1) * 128] = us.astype(BF16)
                continue
            if j == 1:
                r = (_head_rmsnorm(r, ones_ref) * qg_ref[...]) * Q_SCALE
            elif j == 2:
                r = _head_rmsnorm(r, ones_ref) * kg_ref[...]
            o_ref[r0:r0 + rows, (j - 1) * tn:j * tn] = r.astype(BF16)


def in_proj(x2, mod3, g_mix, w_in_bf, q_gain, k_gain, bsz, seqlen, tm=512, rows=256):
    t, d = x2.shape
    n = w_in_bf.shape[1]
    tn = n // 4
    reps = tn // HEAD_DIM
    qg = jnp.tile(q_gain.reshape(1, HEAD_DIM), (1, reps))
    kg = jnp.tile(k_gain.reshape(1, HEAD_DIM), (1, reps))
    idx = jnp.arange(LANE_TILE) // HEAD_DIM
    ones = (idx[:, None] == idx[None, :]).astype(BF16)
    const = lambda shape: pl.BlockSpec(shape, lambda i: (0, 0), pipeline_mode=pl.Buffered(1))
    tpb = seqlen // tm
    ngt = tn // LANE_TILE
    return pl.pallas_call(
        functools.partial(_proj_kernel, rows=rows),
        grid=(t // tm,),
        in_specs=[pl.BlockSpec((tm, d), lambda i: (i, 0)),
                  pl.BlockSpec((1, 6, d), lambda i: (i // tpb, 0, 0)),
                  const((1, d)), const((d, n)), const((1, tn)), const((1, tn)),
                  const((LANE_TILE, LANE_TILE))],
        out_specs=[pl.BlockSpec((1, CHUNK, ngt, tm // CHUNK, LANE_TILE), lambda i: (i // tpb, 0, 0, i % tpb, 0)),
                   pl.BlockSpec((tm, 3 * tn), lambda i: (i, 0))],
        out_shape=[jax.ShapeDtypeStruct((bsz, CHUNK, ngt, seqlen // CHUNK, LANE_TILE), BF16),
                   jax.ShapeDtypeStruct((t, 3 * tn), BF16)],
        scratch_shapes=[pltpu.VMEM((tn // 128, rows, 128), F32)],
        compiler_params=_cparams(1),
        name="proj",
    )(x2, mod3, g_mix.reshape(1, d), w_in_bf, qg, kg, ones)


def _zoh(ar, ai, dt):
    mag = jnp.exp(ar * dt)
    th = ai * dt
    lbr, lbi = mag * jnp.cos(th), mag * jnp.sin(th)
    den = ar * ar + ai * ai
    nr, ni = lbr - 1.0, lbi
    return lbr, lbi, (nr * ar + ni * ai) / den, (ni * ar - nr * ai) / den


def _s5prep_kernel(ar_ref, ai_ref, ldt_ref, btr_ref, bti_ref, cr_ref, ci_ref, d_ref,
                   bl_ref, cl_ref, kx_ref, a_ref):
    dt = jnp.exp(ldt_ref[...])
    lbr, lbi, kr, ki = _zoh(ar_ref[...], ai_ref[...], dt)
    btr, bti = btr_ref[...], bti_ref[...]
    bbr = kr * btr - ki * bti
    bbi = kr * bti + ki * btr
    c_re, c_im = cr_ref[...], ci_ref[...]
    h = c_re.shape[1]
    eye = (lax.broadcasted_iota(jnp.int32, (1, h, h), 1) ==
           lax.broadcasted_iota(jnp.int32, (1, h, h), 2))
    dn = (((2,), (2,)), ((0,), (0,)))
    pr, pi = jnp.ones_like(lbr), jnp.zeros_like(lbr)
    blrs, blis = [], []
    for j in range(CHUNK):
        blr = pr * bbr - pi * bbi
        bli = pr * bbi + pi * bbr
        bl_ref[j, 0] = blr
        bl_ref[j, 1] = bli
        blrs.append(blr)
        blis.append(bli)
        pr, pi = pr * lbr - pi * lbi, pr * lbi + pi * lbr
        cl_ref[j, 0] = c_re * pr - c_im * pi
        cl_ref[j, 1] = -(c_re * pi + c_im * pr)
    a_ref[0] = pr
    a_ref[1] = pi
    kx = (lax.dot_general(jnp.concatenate(blrs, axis=1), c_re, dn, precision=lax.Precision.HIGHEST,
                          preferred_element_type=F32)
          - lax.dot_general(jnp.concatenate(blis, axis=1), c_im, dn, precision=lax.Precision.HIGHEST,
                            preferred_element_type=F32))
    for j in range(CHUNK):
        kxj = kx[:, j * h:(j + 1) * h, :]
        if j == 0:
            kxj = kxj + jnp.where(eye, d_ref[...], 0.0)
        kx_ref[:, :, j * h:(j + 1) * h] = kxj


def s5_prep(a_re, a_im, log_dt, b_re, b_im, c_re, c_im, d_skip, gb=8):
    g, p = a_re.shape
    h = c_re.shape[1]
    vec = pl.BlockSpec((gb, 1, p), lambda i: (i, 0, 0))
    mat = pl.BlockSpec((gb, h, p), lambda i: (i, 0, 0))
    lag = pl.BlockSpec((CHUNK, 2, gb, h, p), lambda i: (0, 0, i, 0, 0))
    lag_shape = jax.ShapeDtypeStruct((CHUNK, 2, g, h, p), F32)
    return pl.pallas_call(
        _s5prep_kernel,
        grid=(g // gb,),
        in_specs=[vec, vec, pl.BlockSpec((gb, 1, 1), lambda i: (i, 0, 0)),
                  mat, mat, mat, mat, pl.BlockSpec((gb, 1, h), lambda i: (i, 0, 0))],
        out_specs=[lag, lag,
                   pl.BlockSpec((gb, h, CHUNK * h), lambda i: (i, 0, 0)),
                   pl.BlockSpec((2, gb, 1, p), lambda i: (0, i, 0, 0))],
        out_shape=[lag_shape, lag_shape,
                   jax.ShapeDtypeStruct((g, h, CHUNK * h), F32),
                   jax.ShapeDtypeStruct((2, g, 1, p), F32)],
        compiler_params=_cparams(1),
        name="s5prep",
    )(a_re.reshape(g, 1, p), a_im.reshape(g, 1, p), log_dt.reshape(g, 1, 1),
      b_re.transpose(0, 2, 1), b_im.transpose(0, 2, 1), c_re, c_im, d_skip.reshape(g, 1, h))


def _group_mask(shape, row_div, col_div):
    return (lax.broadcasted_iota(jnp.int32, shape, 0) // row_div ==
            lax.broadcasted_iota(jnp.int32, shape, 1) // col_div)


def _s5in_kernel(u_ref, bl_ref, e_ref, o_ref, w_ref):
    p, wid = e_ref.shape
    h = LANE_TILE // (wid // p)
    mask = _group_mask((LANE_TILE, wid), h, p)
    for s in range(CHUNK):
        ex = jnp.dot(bl_ref[CHUNK - 1 - s, 0].astype(BF16), e_ref[...], preferred_element_type=F32)
        w_ref[s * LANE_TILE:(s + 1) * LANE_TILE, :] = jnp.where(mask, ex, 0.0).astype(BF16)
    for b in range(u_ref.shape[0]):
        ucat = jnp.concatenate([u_ref[b, s, 0] for s in range(CHUNK)], axis=1)
        o_ref[b, 0] = jnp.dot(ucat, w_ref[...], preferred_element_type=F32)


def s5_chunk_inputs(u_tm, bl2, h):
    bsz, _, ngt, nc, _ = u_tm.shape
    p = bl2.shape[3]
    wid = (LANE_TILE // h) * p
    e = _lane_replicator(p, wid)
    return pl.pallas_call(
        _s5in_kernel,
        grid=(ngt, 2),
        in_specs=[pl.BlockSpec((bsz, CHUNK, 1, nc, LANE_TILE), lambda gt, ri: (0, 0, gt, 0, 0)),
                  pl.BlockSpec((CHUNK, 1, LANE_TILE, p), lambda gt, ri: (0, ri, gt, 0)),
                  pl.BlockSpec((p, wid), lambda gt, ri: (0, 0))],
        out_specs=pl.BlockSpec((bsz, 1, nc, wid), lambda gt, ri: (0, ri, 0, gt)),
        out_shape=jax.ShapeDtypeStruct((bsz, 2, nc, ngt * wid), F32),
        scratch_shapes=[pltpu.VMEM((CHUNK * LANE_TILE, wid), BF16)],
        compiler_params=_cparams(2),
        name="s5in",
    )(u_tm, bl2, e)


def _s5scan_kernel(s_ref, a_ref, x_ref):
    nc, w = s_ref.shape[2], s_ref.shape[3]
    a1r, a1i = a_ref[0], a_ref[1]

    def cmul(xr, xi, yr, yi):
        return xr * yr - xi * yi, xr * yi + xi * yr

    row = lax.broadcasted_iota(jnp.int32, (8, w), 0)
    pows = [(a1r, a1i)]
    for _ in range(7):
        pows.append(cmul(pows[-1][0], pows[-1][1], a1r, a1i))
    steps = []
    for k in (1, 2, 4):
        steps.append((k, jnp.where(row >= k, pows[k - 1][0], 0.0), jnp.where(row >= k, pows[k - 1][1], 0.0)))
    pcr = jnp.zeros((8, w), F32)
    pci = jnp.zeros((8, w), F32)
    for r in range(8):
        pcr = jnp.where(row == r, pows[r][0], pcr)
        pci = jnp.where(row == r, pows[r][1], pci)

    def body(blk, carry):
        cr, ci = carry
        off = pl.multiple_of(blk * 8, 8)
        xr = s_ref[0, 0, pl.ds(off, 8), :]
        xi = s_ref[0, 1, pl.ds(off, 8), :]
        for k, kr, ki in steps:
            sr, si = pltpu.roll(xr, k, 0), pltpu.roll(xi, k, 0)
            dr, di = cmul(sr, si, kr, ki)
            xr, xi = xr + dr, xi + di
        dr, di = cmul(jnp.broadcast_to(cr, (8, w)), jnp.broadcast_to(ci, (8, w)), pcr, pci)
        xr, xi = xr + dr, xi + di
        x_ref[0, 0, pl.ds(off, 8), :] = jnp.where(row == 0, cr, pltpu.roll(xr, 1, 0))
        x_ref[0, 1, pl.ds(off, 8), :] = jnp.where(row == 0, ci, pltpu.roll(xi, 1, 0))
        return xr[7:8, :], xi[7:8, :]

    zero = jnp.zeros((1, w), F32)
    lax.fori_loop(0, nc // 8, body, (zero, zero))


def s5_chunk_scan(s, a, w=1024):
    bsz, _, nc, tot = s.shape
    blk = pl.BlockSpec((1, 2, nc, w), lambda b, l: (b, 0, 0, l))
    return pl.pallas_call(
        _s5scan_kernel,
        grid=(bsz, tot // w),
        in_specs=[blk, pl.BlockSpec((2, 1, w), lambda b, l: (0, 0, l))],
        out_specs=blk,
        out_shape=jax.ShapeDtypeStruct(s.shape, F32),
        compiler_params=_cparams(2),
        name="s5scan",
    )(s, a)


def _s5out_kernel(u_ref, kx_ref, cl_ref, x_ref, e_ref, ep_ref, o_ref, tr_ref):
    bsz, _, _, nc, _ = u_ref.shape
    p, wid = ep_ref.shape
    h = LANE_TILE // CHUNK
    mask_t = _group_mask((LANE_TILE, LANE_TILE), h, h)
    mask_c = _group_mask((LANE_TILE, wid), h, p)
    nt = (((1,), (1,)), ((), ()))
    kx = kx_ref[...].astype(BF16)
    for j in range(CHUNK):
        tj = jnp.where(mask_t, jnp.dot(kx, e_ref[j], preferred_element_type=F32), 0.0)
        tr_ref[(CHUNK - 1 - j) * LANE_TILE:(CHUNK - j) * LANE_TILE, :] = tj.astype(BF16)
    ucat = jnp.concatenate(
        [jnp.concatenate([u_ref[b, s, 0] for s in range(CHUNK)], axis=1) for b in range(bsz)], axis=0)
    xr = x_ref[:, 0].reshape(bsz * nc, wid).astype(BF16)
    xi = x_ref[:, 1].reshape(bsz * nc, wid).astype(BF16)
    for t in range(CHUNK):
        wr = jnp.dot(cl_ref[t, 0].astype(BF16), ep_ref[...], preferred_element_type=F32)
        wi = jnp.dot(cl_ref[t, 1].astype(BF16), ep_ref[...], preferred_element_type=F32)
        wr = jnp.where(mask_c, wr, 0.0).astype(BF16)
        wi = jnp.where(mask_c, wi, 0.0).astype(BF16)
        y = (jnp.dot(ucat[:, :(t + 1) * LANE_TILE], tr_ref[(CHUNK - 1 - t) * LANE_TILE:, :],
                     preferred_element_type=F32)
             + lax.dot_general(xr, wr, nt, preferred_element_type=F32)
             + lax.dot_general(xi, wi, nt, preferred_element_type=F32))
        for b in range(bsz):
            o_ref[b, t, 0] = y[b * nc:(b + 1) * nc].astype(BF16)


def _lane_replicator(p, wid):
    return (jnp.arange(wid)[None, :] % p == jnp.arange(p)[:, None]).astype(BF16)


def s5_outputs(u_tm, kx2, cl2, x):
    bsz, _, ngt, nc, _ = u_tm.shape
    wid = x.shape[3] // ngt
    p = cl2.shape[3]
    h = LANE_TILE // CHUNK
    r = jnp.arange(LANE_TILE)
    e = ((r[None, :, None] // h == jnp.arange(CHUNK)[:, None, None])
         & (r[None, :, None] % h == r[None, None, :] % h)).astype(BF16)
    tile = pl.BlockSpec((bsz, CHUNK, 1, nc, LANE_TILE), lambda gt: (0, 0, gt, 0, 0))
    return pl.pallas_call(
        _s5out_kernel,
        grid=(ngt,),
        in_specs=[tile,
                  pl.BlockSpec((LANE_TILE, LANE_TILE), lambda gt: (gt, 0)),
                  pl.BlockSpec((CHUNK, 2, LANE_TILE, p), lambda gt: (0, 0, gt, 0)),
                  pl.BlockSpec((bsz, 2, nc, wid), lambda gt: (0, 0, 0, gt)),
                  pl.BlockSpec((CHUNK, LANE_TILE, LANE_TILE), lambda gt: (0, 0, 0)),
                  pl.BlockSpec((p, wid), lambda gt: (0, 0))],
        out_specs=tile,
        out_shape=jax.ShapeDtypeStruct(u_tm.shape, BF16),
        scratch_shapes=[pltpu.VMEM((CHUNK * LANE_TILE, LANE_TILE), BF16)],
        compiler_params=_cparams(1),
        name="s5out",
    )(u_tm, kx2, cl2, x, e, _lane_replicator(p, wid))


def _attn_kernel(q_ref, k_ref, v_ref, tri_ref, *rest, blk, nh, unroll, ncast):
    src_refs, o_ref, dst_refs, vcat_ref = rest[:ncast], rest[ncast], rest[ncast + 1:2 * ncast + 1], rest[-1]
    for src, dst in zip(src_refs, dst_refs):
        dst[...] = src[...].astype(BF16)
    qi = pl.program_id(2)
    q = q_ref[0]
    lane = lax.broadcasted_iota(jnp.int32, q.shape, 1)
    heads = [(lane >= HEAD_DIM * h) & (lane < HEAD_DIM * (h + 1)) for h in range(nh)]
    zero = jnp.zeros_like(q)

    @pl.when(qi == 0)
    def _():
        def fill(j, _):
            vb = v_ref[0, pl.ds(pl.multiple_of(j * blk, blk), blk), :]
            vcat_ref[j] = jnp.concatenate([jnp.where(m, vb, zero) for m in heads], axis=0)
            return 0

        lax.fori_loop(0, vcat_ref.shape[0], fill, 0)

    qs = jnp.concatenate([jnp.where(m, q, zero) for m in heads], axis=0)
    row = lax.broadcasted_iota(jnp.int32, (blk, blk), 0)
    col = lax.broadcasted_iota(jnp.int32, (blk, blk), 1)
    past = jnp.concatenate([col < row] * nh, axis=0)
    nt = (((1,), (1,)), ((), ()))

    def block(kj, carry, acc, diag):
        kb = k_ref[0, pl.ds(pl.multiple_of(kj * blk, blk), blk), :]
        z = lax.dot_general(qs, kb, nt, preferred_element_type=F32)
        sp = jnp.maximum(z, 0.0) + jnp.log2(1.0 + jnp.exp2(-jnp.abs(z)))
        if diag:
            sp = jnp.where(past, sp, 0.0)
        cs = lax.dot_general(sp, tri_ref[...], (((1,), (0,)), ((), ())), preferred_element_type=F32)
        w = jnp.exp2(((z - sp) - cs) - carry)
        if diag:
            w = jnp.where(past, w, 0.0)
        wcat = jnp.concatenate([w[h * blk:(h + 1) * blk] for h in range(nh)], axis=1)
        acc = acc + lax.dot_general(wcat, vcat_ref[kj], (((1,), (0,)), ((), ())), preferred_element_type=F32)
        return carry + (cs[:, 0:1] + sp[:, 0:1]), acc

    def head(r):
        def run():
            c = block(qi, jnp.zeros((nh * blk, 1), F32), jnp.zeros(q.shape, F32), True)
            for i in range(r):
                c = block(qi - 1 - i, c[0], c[1], False)
            return c
        return run

    rem = qi % unroll
    c = lax.switch(rem, [head(r) for r in range(unroll)])

    left = qi - rem
    for n in (2 * unroll, unroll):
        def group(i, c, n=n, left=left):
            for r in range(n):
                c = block(left - 1 - r - n * i, c[0], c[1], False)
            return c

        c = lax.fori_loop(0, left // n, group, c)
        left = left % n
    o_ref[0] = c[1].astype(BF16)


def stick_attention(qkv3, width, cast=(), blk=256, nh=4, unroll=4):
    bsz, seqlen, _ = qkv3.shape
    wl = nh * HEAD_DIM
    ngrp = width // wl
    nq = seqlen // blk
    ii = jnp.arange(blk)
    tri = (ii[:, None] > ii[None, :]).astype(BF16)
    nsteps = bsz * ngrp * nq
    slabs = [w.reshape(nsteps, 16, w.size // (nsteps * 16)) for w in cast]
    assert all(s.shape[2] % 128 == 0 for s in slabs)
    slab_specs = [pl.BlockSpec((1,) + s.shape[1:], lambda b, h, i: ((b * ngrp + h) * nq + i, 0, 0)) for s in slabs]
    outs = pl.pallas_call(
        functools.partial(_attn_kernel, blk=blk, nh=nh, unroll=unroll, ncast=len(cast)),
        grid=(bsz, ngrp, nq),
        in_specs=[pl.BlockSpec((1, blk, wl), lambda b, h, i: (b, i, h)),
                  pl.BlockSpec((1, seqlen, wl), lambda b, h, i: (b, 0, ngrp + h)),
                  pl.BlockSpec((1, seqlen, wl), lambda b, h, i: (b, 0, 2 * ngrp + h)),
                  pl.BlockSpec((blk, blk), lambda b, h, i: (0, 0))] + slab_specs,
        out_specs=[pl.BlockSpec((1, blk, wl), lambda b, h, i: (b, i, h))] + slab_specs,
        out_shape=[jax.ShapeDtypeStruct((bsz, seqlen, width), BF16)]
                  + [jax.ShapeDtypeStruct(s.shape, BF16) for s in slabs],
        scratch_shapes=[pltpu.VMEM((nq, nh * blk, wl), BF16)],
        compiler_params=_cparams(3),
        name="attn",
    )(qkv3, qkv3, qkv3, tri, *slabs)
    return outs[0], [o.reshape(w.shape) for o, w in zip(outs[1:], cast)]


def _out_kernel(y_ref, a_ref, x_ref, mod_ref, wglu_ref, gs_ref, ga_ref, w0_ref, w1_ref, gf_ref,
                h_ref, xf_ref, ys_ref, *, rows):
    cpr = rows // CHUNK
    for r0 in range(0, a_ref.shape[0], rows):
        rs = slice(r0, r0 + rows)
        c0 = r0 // CHUNK
        nk = ys_ref.shape[0]
        for s in range(CHUNK):
            for k in range(nk):
                ys_ref[k, pl.ds(s, cpr, stride=CHUNK), :] = (
                    y_ref[0, s, k // 2, c0:c0 + cpr, (k % 2) * 128:(k % 2 + 1) * 128].astype(F32))
        y = jax.nn.gelu(jnp.concatenate([ys_ref[k] for k in range(nk)], axis=1))
        z = jnp.dot(y.astype(BF16), wglu_ref[...], preferred_element_type=F32)
        ys = y * jax.nn.sigmoid(z)
        ysn = (_rms(ys) * gs_ref[...]).astype(BF16)
        an = (_rms(a_ref[rs, :].astype(F32)) * ga_ref[...]).astype(BF16)
        o = (jnp.dot(ysn, w0_ref[...], preferred_element_type=F32)
             + jnp.dot(an, w1_ref[...], preferred_element_type=F32))
        h = x_ref[rs, :] + mod_ref[0, 2:3, :] * o
        h_ref[rs, :] = h
        xn = _rms(h) * gf_ref[...]
        xf_ref[rs, :] = (xn * (1.0 + mod_ref[0, 4:5, :]) + mod_ref[0, 3:4, :]).astype(BF16)


def out_proj(y_tm, a2, x2, mod3, w_glu_bf, g_ssm, g_attn, w_out_bf, g_ffn, seqlen, tm=512, rows=256):
    t, d = x2.shape
    ngt = y_tm.shape[2]
    ws = ngt * LANE_TILE
    wa = a2.shape[1]
    tpb = seqlen // tm
    const = lambda shape, r=0: pl.BlockSpec(shape, lambda i: (r, 0), pipeline_mode=pl.Buffered(1))
    return pl.pallas_call(
        functools.partial(_out_kernel, rows=rows),
        grid=(t // tm,),
        in_specs=[pl.BlockSpec((1, CHUNK, ngt, tm // CHUNK, LANE_TILE), lambda i: (i // tpb, 0, 0, i % tpb, 0)),
                  pl.BlockSpec((tm, wa), lambda i: (i, 0)),
                  pl.BlockSpec((tm, d), lambda i: (i, 0)),
                  pl.BlockSpec((1, 6, d), lambda i: (i // tpb, 0, 0)),
                  const((ws, ws)), const((1, ws)), const((1, wa)),
                  const((ws, d)), const((wa, d), 1), const((1, d))],
        out_specs=[pl.BlockSpec((tm, d), lambda i: (i, 0)),
                   pl.BlockSpec((tm, d), lambda i: (i, 0))],
        out_shape=[jax.ShapeDtypeStruct((t, d), F32), jax.ShapeDtypeStruct((t, d), BF16)],
        scratch_shapes=[pltpu.VMEM((ws // 128, rows, 128), F32)],
        compiler_params=_cparams(1),
        name="outproj",
    )(y_tm, a2, x2, mod3, w_glu_bf, g_ssm.reshape(1, ws), g_attn.reshape(1, wa), w_out_bf, w_out_bf,
      g_ffn.reshape(1, d))


def _ffn_kernel(xf_ref, h_ref, mod_ref, wg_ref, wu_ref, wd_ref, o_ref):
    j = pl.program_id(1)

    @pl.when(j == 0)
    def _():
        o_ref[...] = jnp.zeros_like(o_ref)

    xf = xf_ref[...]
    g = jnp.dot(xf, wg_ref[...], preferred_element_type=F32)
    u = jnp.dot(xf, wu_ref[...], preferred_element_type=F32)
    hm = (jax.nn.silu(g) * u).astype(BF16)
    o_ref[...] += jnp.dot(hm, wd_ref[...], preferred_element_type=F32)

    @pl.when(j == pl.num_programs(1) - 1)
    def _():
        o_ref[...] = h_ref[...] + mod_ref[0, 5:6, :] * o_ref[...]


def ffn(xf2, h2, mod3, w_gate, w_up, w_down, seqlen, tm=1024, tf=512):
    t, d = h2.shape
    dff = w_gate.shape[1]
    row = lambda dt: pl.BlockSpec((tm, d), lambda i, j: (i, 0), pipeline_mode=pl.Buffered(1))
    return pl.pallas_call(
        _ffn_kernel,
        grid=(t // tm, dff // tf),
        in_specs=[row(BF16), row(F32),
                  pl.BlockSpec((1, 6, d), lambda i, j: ((i * tm) // seqlen, 0, 0)),
                  pl.BlockSpec((d, tf), lambda i, j: (0, j)),
                  pl.BlockSpec((d, tf), lambda i, j: (0, j)),
                  pl.BlockSpec((tf, d), lambda i, j: (j, 0))],
        out_specs=pl.BlockSpec((tm, d), lambda i, j: (i, 0)),
        out_shape=jax.ShapeDtypeStruct((t, d), F32),
        compiler_params=_cparams(2),
        name="ffn",
    )(xf2, h2, mod3, w_gate, w_up, w_down)


def s5_mixer_chunked(u_tm, a_re, a_im, log_dt, b_re, b_im, c_re, c_im, d_skip):
    bsz, _, ngt, nc, _ = u_tm.shape
    g, p = a_re.shape
    h = c_re.shape[1]
    assert h * CHUNK == LANE_TILE and g * h == ngt * LANE_TILE
    bl, cl, kx, a = s5_prep(a_re, a_im, log_dt, b_re, b_im, c_re, c_im, d_skip)
    s = s5_chunk_inputs(u_tm, bl.reshape(CHUNK, 2, g * h, p), h)
    x = s5_chunk_scan(s, a.reshape(2, 1, g * p))
    return s5_outputs(u_tm, kx.reshape(g * h, CHUNK * h), cl.reshape(CHUNK, 2, g * h, p), x)


def kernel(x, c, w_ada, b_ada, g_mix, w_in, a_re, a_im, log_dt, b_re, b_im, c_re, c_im, d_skip,
           w_glu, q_gain, k_gain, g_ssm_out, g_attn_out, w_out, g_ffn, w_gate, w_up, w_down):
    bsz, seqlen, d = x.shape
    depth = w_ada.shape[0]
    width = w_glu.shape[1]
    h2 = x.reshape(bsz * seqlen, d)
    for l in range(depth):
        mod3 = ada_mod(c, w_ada[l], b_ada[l]).reshape(bsz, 6, d)
        u_tm, qkv = in_proj(h2, mod3, g_mix[l], w_in[l].astype(BF16), q_gain[l], k_gain[l], bsz, seqlen)
        y_tm = s5_mixer_chunked(u_tm, a_re[l], a_im[l], log_dt[l], b_re[l], b_im[l],
                                c_re[l], c_im[l], d_skip[l])
        att, (wglu_bf, wout_bf, wg_bf, wu_bf, wd_bf) = stick_attention(
            qkv.reshape(bsz, seqlen, 3 * width), width,
            cast=(w_glu[l], w_out[l], w_gate[l], w_up[l], w_down[l]))
        h2, xf = out_proj(y_tm, att.reshape(bsz * seqlen, width), h2, mod3,
                          wglu_bf, g_ssm_out[l], g_attn_out[l], wout_bf, g_ffn[l], seqlen)
        h2 = ffn(xf, h2, mod3, wg_bf, wu_bf, wd_bf, seqlen)
    return h2.reshape(bsz, seqlen, d)
```

```python
import functools

import jax
import jax.numpy as jnp
from jax import lax
from jax.experimental import pallas as pl
from jax.experimental.pallas import tpu as pltpu

F32 = jnp.float32
BF16 = jnp.bfloat16
EPS = 1e-6
HEAD_DIM = 64
CHUNK = 16
LANE_TILE = 256
VMEM_LIMIT = 56 * 1024 * 1024
Q_SCALE = HEAD_DIM ** -0.5 * 1.4426950408889634


def _cparams(n_axes, vmem=VMEM_LIMIT):
    return pltpu.CompilerParams(dimension_semantics=("arbitrary",) * n_axes,
                                vmem_limit_bytes=vmem)


def _rms(x):
    return x * lax.rsqrt(jnp.mean(x * x, axis=-1, keepdims=True) + EPS)


def _ada_kernel(c_ref, w_ref, b_ref, o_ref):
    d, tn = w_ref.shape
    rows = 256
    for b in range(c_ref.shape[0]):
        acc = jnp.zeros((8, tn), F32)
        for kc in range(d // rows):
            cond = jax.nn.silu(c_ref[b, kc * rows:(kc + 1) * rows, :])
            blk = cond * w_ref[kc * rows:(kc + 1) * rows, :]
            acc = acc + jnp.sum(blk.reshape(rows // 8, 8, tn), axis=0)
        o_ref[b:b + 1, :] = jnp.sum(acc, axis=0, keepdims=True) + b_ref[...]


def ada_mod(c, w_ada, b_ada, tn=1024):
    bsz, d = c.shape
    n = w_ada.shape[1]
    return pl.pallas_call(
        _ada_kernel,
        grid=(n // tn,),
        in_specs=[pl.BlockSpec((bsz, d, 1), lambda j: (0, 0, 0)),
                  pl.BlockSpec((d, tn), lambda j: (0, j)),
                  pl.BlockSpec((1, tn), lambda j: (0, j))],
        out_specs=pl.BlockSpec((bsz, tn), lambda j: (0, j)),
        out_shape=jax.ShapeDtypeStruct((bsz, n), F32),
        compiler_params=_cparams(1),
        name="ada",
    )(c.reshape(bsz, d, 1), w_ada, b_ada.reshape(1, n))


def _head_rmsnorm(r, ones_ref):
    outs = []
    for s in range(r.shape[1] // LANE_TILE):
        rs = r[:, s * LANE_TILE:(s + 1) * LANE_TILE]
        ss = jnp.dot((rs * rs).astype(BF16), ones_ref[...], preferred_element_type=F32)
        outs.append(rs * lax.rsqrt(ss * (1.0 / HEAD_DIM) + EPS))
    return jnp.concatenate(outs, axis=1)


def _proj_kernel(x_ref, mod_ref, g_ref, w_ref, qg_ref, kg_ref, ones_ref, u_ref, o_ref, us_ref, *, rows):
    tm = x_ref.shape[0]
    tn = w_ref.shape[1] // 4
    cpr = rows // CHUNK
    for r0 in range(0, tm, rows):
        xn = _rms(x_ref[r0:r0 + rows, :]) * g_ref[...]
        xm = (xn * (1.0 + mod_ref[0, 1:2, :]) + mod_ref[0, 0:1, :]).astype(BF16)
        for j in range(4):
            r = jnp.dot(xm, w_ref[:, j * tn:(j + 1) * tn], preferred_element_type=F32)
            if j == 0:
                c0 = r0 // CHUNK
                for k in range(tn // 128):
                    us_ref[k] = r[:, k * 128:(k + 1) * 128]
                for s in range(CHUNK):
                    for k in range(tn // 128):
                        us = us_ref[k, pl.ds(s, cpr, stride=CHUNK), :]
                        u_ref[0, s, k // 2, c0:c0 + cpr, (k % 2) * 128:(k % 2 + 1) * 128] = us.astype(BF16)
                continue
            if j == 1:
                r = (_head_rmsnorm(r, ones_ref) * qg_ref[...]) * Q_SCALE
            elif j == 2:
                r = _head_rmsnorm(r, ones_ref) * kg_ref[...]
            o_ref[r0:r0 + rows, (j - 1) * tn:j * tn] = r.astype(BF16)


def in_proj(x2, mod3, g_mix, w_in_bf, q_gain, k_gain, bsz, seqlen, tm=512, rows=256):
    t, d = x2.shape
    n = w_in_bf.shape[1]
    tn = n // 4
    reps = tn // HEAD_DIM
    qg = jnp.tile(q_gain.reshape(1, HEAD_DIM), (1, reps))
    kg = jnp.tile(k_gain.reshape(1, HEAD_DIM), (1, reps))
    idx = jnp.arange(LANE_TILE) // HEAD_DIM
    ones = (idx[:, None] == idx[None, :]).astype(BF16)
    const = lambda shape: pl.BlockSpec(shape, lambda i: (0, 0), pipeline_mode=pl.Buffered(1))
    tpb = seqlen // tm
    ngt = tn // LANE_TILE
    return pl.pallas_call(
        functools.partial(_proj_kernel, rows=rows),
        grid=(t // tm,),
        in_specs=[pl.BlockSpec((tm, d), lambda i: (i, 0)),
                  pl.BlockSpec((1, 6, d), lambda i: (i // tpb, 0, 0)),
                  const((1, d)), const((d, n)), const((1, tn)), const((1, tn)),
                  const((LANE_TILE, LANE_TILE))],
        out_specs=[pl.BlockSpec((1, CHUNK, ngt, tm // CHUNK, LANE_TILE), lambda i: (i // tpb, 0, 0, i % tpb, 0)),
                   pl.BlockSpec((tm, 3 * tn), lambda i: (i, 0))],
        out_shape=[jax.ShapeDtypeStruct((bsz, CHUNK, ngt, seqlen // CHUNK, LANE_TILE), BF16),
                   jax.ShapeDtypeStruct((t, 3 * tn), BF16)],
        scratch_shapes=[pltpu.VMEM((tn // 128, rows, 128), F32)],
        compiler_params=_cparams(1),
        name="proj",
    )(x2, mod3, g_mix.reshape(1, d), w_in_bf, qg, kg, ones)


def _zoh(ar, ai, dt):
    mag = jnp.exp(ar * dt)
    th = ai * dt
    lbr, lbi = mag * jnp.cos(th), mag * jnp.sin(th)
    den = ar * ar + ai * ai
    nr, ni = lbr - 1.0, lbi
    return lbr, lbi, (nr * ar + ni * ai) / den, (ni * ar - nr * ai) / den


def _s5prep_kernel(ar_ref, ai_ref, ldt_ref, btr_ref, bti_ref, cr_ref, ci_ref, d_ref,
                   bl_ref, cl_ref, kx_ref, a_ref):
    dt = jnp.exp(ldt_ref[...])
    lbr, lbi, kr, ki = _zoh(ar_ref[...], ai_ref[...], dt)
    btr, bti = btr_ref[...], bti_ref[...]
    bbr = kr * btr - ki * bti
    bbi = kr * bti + ki * btr
    c_re, c_im = cr_ref[...], ci_ref[...]
    h = c_re.shape[1]
    eye = (lax.broadcasted_iota(jnp.int32, (1, h, h), 1) ==
           lax.broadcasted_iota(jnp.int32, (1, h, h), 2))
    dn = (((2,), (2,)), ((0,), (0,)))
    pr, pi = jnp.ones_like(lbr), jnp.zeros_like(lbr)
    blrs, blis = [], []
    for j in range(CHUNK):
        blr = pr * bbr - pi * bbi
        bli = pr * bbi + pi * bbr
        bl_ref[j, 0] = blr
        bl_ref[j, 1] = bli
        blrs.append(blr)
        blis.append(bli)
        pr, pi = pr * lbr - pi * lbi, pr * lbi + pi * lbr
        cl_ref[j, 0] = c_re * pr - c_im * pi
        cl_ref[j, 1] = -(c_re * pi + c_im * pr)
    a_ref[0] = pr
    a_ref[1] = pi
    kx = (lax.dot_general(jnp.concatenate(blrs, axis=1), c_re, dn, precision=lax.Precision.HIGHEST,
                          preferred_element_type=F32)
          - lax.dot_general(jnp.concatenate(blis, axis=1), c_im, dn, precision=lax.Precision.HIGHEST,
                            preferred_element_type=F32))
    for j in range(CHUNK):
        kxj = kx[:, j * h:(j + 1) * h, :]
        if j == 0:
            kxj = kxj + jnp.where(eye, d_ref[...], 0.0)
        kx_ref[:, :, j * h:(j + 1) * h] = kxj


def s5_prep(a_re, a_im, log_dt, b_re, b_im, c_re, c_im, d_skip, gb=8):
    g, p = a_re.shape
    h = c_re.shape[1]
    vec = pl.BlockSpec((gb, 1, p), lambda i: (i, 0, 0))
    mat = pl.BlockSpec((gb, h, p), lambda i: (i, 0, 0))
    lag = pl.BlockSpec((CHUNK, 2, gb, h, p), lambda i: (0, 0, i, 0, 0))
    lag_shape = jax.ShapeDtypeStruct((CHUNK, 2, g, h, p), F32)
    return pl.pallas_call(
        _s5prep_kernel,
        grid=(g // gb,),
        in_specs=[vec, vec, pl.BlockSpec((gb, 1, 1), lambda i: (i, 0, 0)),
                  mat, mat, mat, mat, pl.BlockSpec((gb, 1, h), lambda i: (i, 0, 0))],
        out_specs=[lag, lag,
                   pl.BlockSpec((gb, h, CHUNK * h), lambda i: (i, 0, 0)),
                   pl.BlockSpec((2, gb, 1, p), lambda i: (0, i, 0, 0))],
        out_shape=[lag_shape, lag_shape,
                   jax.ShapeDtypeStruct((g, h, CHUNK * h), F32),
                   jax.ShapeDtypeStruct((2, g, 1, p), F32)],
        compiler_params=_cparams(1),
        name="s5prep",
    )(a_re.reshape(g, 1, p), a_im.reshape(g, 1, p), log_dt.reshape(g, 1, 1),
      b_re.transpose(0, 2, 1), b_im.transpose(0, 2, 1), c_re, c_im, d_skip.reshape(g, 1, h))


def _group_mask(shape, row_div, col_div):
    return (lax.broadcasted_iota(jnp.int32, shape, 0) // row_div ==
            lax.broadcasted_iota(jnp.int32, shape, 1) // col_div)


def _s5in_kernel(u_ref, bl_ref, e_ref, o_ref, w_ref):
    p, wid = e_ref.shape
    h = LANE_TILE // (wid // p)
    mask = _group_mask((LANE_TILE, wid), h, p)
    for s in range(CHUNK):
        ex = jnp.dot(bl_ref[CHUNK - 1 - s, 0].astype(BF16), e_ref[...], preferred_element_type=F32)
        w_ref[s * LANE_TILE:(s + 1) * LANE_TILE, :] = jnp.where(mask, ex, 0.0).astype(BF16)
    for b in range(u_ref.shape[0]):
        ucat = jnp.concatenate([u_ref[b, s, 0] for s in range(CHUNK)], axis=1)
        o_ref[b, 0] = jnp.dot(ucat, w_ref[...], preferred_element_type=F32)


def s5_chunk_inputs(u_tm, bl2, h):
    bsz, _, ngt, nc, _ = u_tm.shape
    p = bl2.shape[3]
    wid = (LANE_TILE // h) * p
    e = _lane_replicator(p, wid)
    return pl.pallas_call(
        _s5in_kernel,
        grid=(ngt, 2),
        in_specs=[pl.BlockSpec((bsz, CHUNK, 1, nc, LANE_TILE), lambda gt, ri: (0, 0, gt, 0, 0)),
                  pl.BlockSpec((CHUNK, 1, LANE_TILE, p), lambda gt, ri: (0, ri, gt, 0)),
                  pl.BlockSpec((p, wid), lambda gt, ri: (0, 0))],
        out_specs=pl.BlockSpec((bsz, 1, nc, wid), lambda gt, ri: (0, ri, 0, gt)),
        out_shape=jax.ShapeDtypeStruct((bsz, 2, nc, ngt * wid), F32),
        scratch_shapes=[pltpu.VMEM((CHUNK * LANE_TILE, wid), BF16)],
        compiler_params=_cparams(2),
        name="s5in",
    )(u_tm, bl2, e)


def _s5scan_kernel(s_ref, a_ref, x_ref):
    nc, w = s_ref.shape[2], s_ref.shape[3]
    a1r, a1i = a_ref[0], a_ref[1]

    def cmul(xr, xi, yr, yi):
        return xr * yr - xi * yi, xr * yi + xi * yr

    row = lax.broadcasted_iota(jnp.int32, (8, w), 0)
    pows = [(a1r, a1i)]
    for _ in range(7):
        pows.append(cmul(pows[-1][0], pows[-1][1], a1r, a1i))
    steps = []
    for k in (1, 2, 4):
        steps.append((k, jnp.where(row >= k, pows[k - 1][0], 0.0), jnp.where(row >= k, pows[k - 1][1], 0.0)))
    pcr = jnp.zeros((8, w), F32)
    pci = jnp.zeros((8, w), F32)
    for r in range(8):
        pcr = jnp.where(row == r, pows[r][0], pcr)
        pci = jnp.where(row == r, pows[r][1], pci)

    def body(blk, carry):
        cr, ci = carry
        off = pl.multiple_of(blk * 8, 8)
        xr = s_ref[0, 0, pl.ds(off, 8), :]
        xi = s_ref[0, 1, pl.ds(off, 8), :]
        for k, kr, ki in steps:
            sr, si = pltpu.roll(xr, k, 0), pltpu.roll(xi, k, 0)
            dr, di = cmul(sr, si, kr, ki)
            xr, xi = xr + dr, xi + di
        dr, di = cmul(jnp.broadcast_to(cr, (8, w)), jnp.broadcast_to(ci, (8, w)), pcr, pci)
        xr, xi = xr + dr, xi + di
        x_ref[0, 0, pl.ds(off, 8), :] = jnp.where(row == 0, cr, pltpu.roll(xr, 1, 0))
        x_ref[0, 1, pl.ds(off, 8), :] = jnp.where(row == 0, ci, pltpu.roll(xi, 1, 0))
        return xr[7:8, :], xi[7:8, :]

    zero = jnp.zeros((1, w), F32)
    lax.fori_loop(0, nc // 8, body, (zero, zero))


def s5_chunk_scan(s, a, w=1024):
    bsz, _, nc, tot = s.shape
    blk = pl.BlockSpec((1, 2, nc, w), lambda b, l: (b, 0, 0, l))
    return pl.pallas_call(
        _s5scan_kernel,
        grid=(bsz, tot // w),
        in_specs=[blk, pl.BlockSpec((2, 1, w), lambda b, l: (0, 0, l))],
        out_specs=blk,
        out_shape=jax.ShapeDtypeStruct(s.shape, F32),
        compiler_params=_cparams(2),
        name="s5scan",
    )(s, a)


def _s5out_kernel(u_ref, kx_ref, cl_ref, x_ref, e_ref, ep_ref, o_ref, tr_ref):
    bsz, _, _, nc, _ = u_ref.shape
    p, wid = ep_ref.shape
    h = LANE_TILE // CHUNK
    mask_t = _group_mask((LANE_TILE, LANE_TILE), h, h)
    mask_c = _group_mask((LANE_TILE, wid), h, p)
    nt = (((1,), (1,)), ((), ()))
    kx = kx_ref[...].astype(BF16)
    for j in range(CHUNK):
        tj = jnp.where(mask_t, jnp.dot(kx, e_ref[j], preferred_element_type=F32), 0.0)
        tr_ref[(CHUNK - 1 - j) * LANE_TILE:(CHUNK - j) * LANE_TILE, :] = tj.astype(BF16)
    ucat = jnp.concatenate(
        [jnp.concatenate([u_ref[b, s, 0] for s in range(CHUNK)], axis=1) for b in range(bsz)], axis=0)
    xr = x_ref[:, 0].reshape(bsz * nc, wid).astype(BF16)
    xi = x_ref[:, 1].reshape(bsz * nc, wid).astype(BF16)
    for t in range(CHUNK):
        wr = jnp.dot(cl_ref[t, 0].astype(BF16), ep_ref[...], preferred_element_type=F32)
        wi = jnp.dot(cl_ref[t, 1].astype(BF16), ep_ref[...], preferred_element_type=F32)
        wr = jnp.where(mask_c, wr, 0.0).astype(BF16)
        wi = jnp.where(mask_c, wi, 0.0).astype(BF16)
        y = (jnp.dot(ucat[:, :(t + 1) * LANE_TILE], tr_ref[(CHUNK - 1 - t) * LANE_TILE:, :],
                     preferred_element_type=F32)
             + lax.dot_general(xr, wr, nt, preferred_element_type=F32)
             + lax.dot_general(xi, wi, nt, preferred_element_type=F32))
        for b in range(bsz):
            o_ref[b, t, 0] = y[b * nc:(b + 1) * nc].astype(BF16)


def _lane_replicator(p, wid):
    return (jnp.arange(wid)[None, :] % p == jnp.arange(p)[:, None]).astype(BF16)


def s5_outputs(u_tm, kx2, cl2, x):
    bsz, _, ngt, nc, _ = u_tm.shape
    wid = x.shape[3] // ngt
    p = cl2.shape[3]
    h = LANE_TILE // CHUNK
    r = jnp.arange(LANE_TILE)
    e = ((r[None, :, None] // h == jnp.arange(CHUNK)[:, None, None])
         & (r[None, :, None] % h == r[None, None, :] % h)).astype(BF16)
    tile = pl.BlockSpec((bsz, CHUNK, 1, nc, LANE_TILE), lambda gt: (0, 0, gt, 0, 0))
    return pl.pallas_call(
        _s5out_kernel,
        grid=(ngt,),
        in_specs=[tile,
                  pl.BlockSpec((LANE_TILE, LANE_TILE), lambda gt: (gt, 0)),
                  pl.BlockSpec((CHUNK, 2, LANE_TILE, p), lambda gt: (0, 0, gt, 0)),
                  pl.BlockSpec((bsz, 2, nc, wid), lambda gt: (0, 0, 0, gt)),
                  pl.BlockSpec((CHUNK, LANE_TILE, LANE_TILE), lambda gt: (0, 0, 0)),
                  pl.BlockSpec((p, wid), lambda gt: (0, 0))],
        out_specs=tile,
        out_shape=jax.ShapeDtypeStruct(u_tm.shape, BF16),
        scratch_shapes=[pltpu.VMEM((CHUNK * LANE_TILE, LANE_TILE), BF16)],
        compiler_params=_cparams(1),
        name="s5out",
    )(u_tm, kx2, cl2, x, e, _lane_replicator(p, wid))


def _attn_kernel(q_ref, k_ref, v_ref, tri_ref, *rest, blk, nh, unroll, ncast):
    src_refs, o_ref, dst_refs, vcat_ref = rest[:ncast], rest[ncast], rest[ncast + 1:2 * ncast + 1], rest[-1]
    for src, dst in zip(src_refs, dst_refs):
        dst[...] = src[...].astype(BF16)
    qi = pl.program_id(2)
    q = q_ref[0]
    lane = lax.broadcasted_iota(jnp.int32, q.shape, 1)
    heads = [(lane >= HEAD_DIM * h) & (lane < HEAD_DIM * (h + 1)) for h in range(nh)]
    zero = jnp.zeros_like(q)

    @pl.when(qi == 0)
    def _():
        def fill(j, _):
            vb = v_ref[0, pl.ds(pl.multiple_of(j * blk, blk), blk), :]
            vcat_ref[j] = jnp.concatenate([jnp.where(m, vb, zero) for m in heads], axis=0)
            return 0

        lax.fori_loop(0, vcat_ref.shape[0], fill, 0)

    qs = jnp.concatenate([jnp.where(m, q, zero) for m in heads], axis=0)
    row = lax.broadcasted_iota(jnp.int32, (blk, blk), 0)
    col = lax.broadcasted_iota(jnp.int32, (blk, blk), 1)
    past = jnp.concatenate([col < row] * nh, axis=0)
    nt = (((1,), (1,)), ((), ()))

    def block(kj, carry, acc, diag):
        kb = k_ref[0, pl.ds(pl.multiple_of(kj * blk, blk), blk), :]
        z = lax.dot_general(qs, kb, nt, preferred_element_type=F32)
        sp = jnp.maximum(z, 0.0) + jnp.log2(1.0 + jnp.exp2(-jnp.abs(z)))
        if diag:
            sp = jnp.where(past, sp, 0.0)
        cs = lax.dot_general(sp, tri_ref[...], (((1,), (0,)), ((), ())), preferred_element_type=F32)
        w = jnp.exp2(((z - sp) - cs) - carry)
        if diag:
            w = jnp.where(past, w, 0.0)
        wcat = jnp.concatenate([w[h * blk:(h + 1) * blk] for h in range(nh)], axis=1)
        acc = acc + lax.dot_general(wcat, vcat_ref[kj], (((1,), (0,)), ((), ())), preferred_element_type=F32)
        return carry + (cs[:, 0:1] + sp[:, 0:1]), acc

    def head(r):
        def run():
            c = block(qi, jnp.zeros((nh * blk, 1), F32), jnp.zeros(q.shape, F32), True)
            for i in range(r):
                c = block(qi - 1 - i, c[0], c[1], False)
            return c
        return run

    rem = qi % unroll
    c = lax.switch(rem, [head(r) for r in range(unroll)])

    left = qi - rem
    for n in (2 * unroll, unroll):
        def group(i, c, n=n, left=left):
            for r in range(n):
                c = block(left - 1 - r - n * i, c[0], c[1], False)
            return c

        c = lax.fori_loop(0, left // n, group, c)
        left = left % n
    o_ref[0] = c[1].astype(BF16)


def stick_attention(qkv3, width, cast=(), blk=256, nh=4, unroll=4):
    bsz, seqlen, _ = qkv3.shape
    wl = nh * HEAD_DIM
    ngrp = width // wl
    nq = seqlen // blk
    ii = jnp.arange(blk)
    tri = (ii[:, None] > ii[None, :]).astype(BF16)
    nsteps = bsz * ngrp * nq

    def slab_spec(w):
        rows, cols = w.shape
        for c in range(1, cols // 128 + 1):
            r = nsteps // c
            if cols % (128 * c) == 0 and nsteps % c == 0 and rows % (16 * r) == 0:
                return pl.BlockSpec((rows // r, cols // c),
                                    lambda b, h, i, c=c: (((b * ngrp + h) * nq + i) // c, ((b * ngrp + h) * nq + i) % c))
        raise ValueError(f"cannot cut {w.shape} into {nsteps} blocks")

    slab_specs = [slab_spec(w) for w in cast]
    outs = pl.pallas_call(
        functools.partial(_attn_kernel, blk=blk, nh=nh, unroll=unroll, ncast=len(cast)),
        grid=(bsz, ngrp, nq),
        in_specs=[pl.BlockSpec((1, blk, wl), lambda b, h, i: (b, i, h)),
                  pl.BlockSpec((1, seqlen, wl), lambda b, h, i: (b, 0, ngrp + h)),
                  pl.BlockSpec((1, seqlen, wl), lambda b, h, i: (b, 0, 2 * ngrp + h)),
                  pl.BlockSpec((blk, blk), lambda b, h, i: (0, 0))] + slab_specs,
        out_specs=[pl.BlockSpec((1, blk, wl), lambda b, h, i: (b, i, h))] + slab_specs,
        out_shape=[jax.ShapeDtypeStruct((bsz, seqlen, width), BF16)]
                  + [jax.ShapeDtypeStruct(w.shape, BF16) for w in cast],
        scratch_shapes=[pltpu.VMEM((nq, nh * blk, wl), BF16)],
        compiler_params=_cparams(3),
        name="attn",
    )(qkv3, qkv3, qkv3, tri, *cast)
    return outs[0], list(outs[1:])


def _out_kernel(y_ref, a_ref, x_ref, mod_ref, wglu_ref, gs_ref, ga_ref, w0_ref, w1_ref, gf_ref,
                h_ref, xf_ref, ys_ref, *, rows):
    cpr = rows // CHUNK
    for r0 in range(0, a_ref.shape[0], rows):
        rs = slice(r0, r0 + rows)
        c0 = r0 // CHUNK
        nk = ys_ref.shape[0]
        for s in range(CHUNK):
            for k in range(nk):
                ys_ref[k, pl.ds(s, cpr, stride=CHUNK), :] = (
                    y_ref[0, s, k // 2, c0:c0 + cpr, (k % 2) * 128:(k % 2 + 1) * 128].astype(F32))
        y = jax.nn.gelu(jnp.concatenate([ys_ref[k] for k in range(nk)], axis=1))
        z = jnp.dot(y.astype(BF16), wglu_ref[...], preferred_element_type=F32)
        ys = y * jax.nn.sigmoid(z)
        ysn = (_rms(ys) * gs_ref[...]).astype(BF16)
        an = (_rms(a_ref[rs, :].astype(F32)) * ga_ref[...]).astype(BF16)
        o = (jnp.dot(ysn, w0_ref[...], preferred_element_type=F32)
             + jnp.dot(an, w1_ref[...], preferred_element_type=F32))
        h = x_ref[rs, :] + mod_ref[0, 2:3, :] * o
        h_ref[rs, :] = h
        xn = _rms(h) * gf_ref[...]
        xf_ref[rs, :] = (xn * (1.0 + mod_ref[0, 4:5, :]) + mod_ref[0, 3:4, :]).astype(BF16)


def out_proj(y_tm, a2, x2, mod3, w_glu_bf, g_ssm, g_attn, w_out_bf, g_ffn, seqlen, tm=512, rows=256):
    t, d = x2.shape
    ngt = y_tm.shape[2]
    ws = ngt * LANE_TILE
    wa = a2.shape[1]
    tpb = seqlen // tm
    const = lambda shape, r=0: pl.BlockSpec(shape, lambda i: (r, 0), pipeline_mode=pl.Buffered(1))
    return pl.pallas_call(
        functools.partial(_out_kernel, rows=rows),
        grid=(t // tm,),
        in_specs=[pl.BlockSpec((1, CHUNK, ngt, tm // CHUNK, LANE_TILE), lambda i: (i // tpb, 0, 0, i % tpb, 0)),
                  pl.BlockSpec((tm, wa), lambda i: (i, 0)),
                  pl.BlockSpec((tm, d), lambda i: (i, 0)),
                  pl.BlockSpec((1, 6, d), lambda i: (i // tpb, 0, 0)),
                  const((ws, ws)), const((1, ws)), const((1, wa)),
                  const((ws, d)), const((wa, d), 1), const((1, d))],
        out_specs=[pl.BlockSpec((tm, d), lambda i: (i, 0)),
                   pl.BlockSpec((tm, d), lambda i: (i, 0))],
        out_shape=[jax.ShapeDtypeStruct((t, d), F32), jax.ShapeDtypeStruct((t, d), BF16)],
        scratch_shapes=[pltpu.VMEM((ws // 128, rows, 128), F32)],
        compiler_params=_cparams(1),
        name="outproj",
    )(y_tm, a2, x2, mod3, w_glu_bf, g_ssm.reshape(1, ws), g_attn.reshape(1, wa), w_out_bf, w_out_bf,
      g_ffn.reshape(1, d))


def _ffn_kernel(xf_ref, h_ref, mod_ref, wg_ref, wu_ref, wd_ref, o_ref):
    j = pl.program_id(1)

    @pl.when(j == 0)
    def _():
        o_ref[...] = jnp.zeros_like(o_ref)

    xf = xf_ref[...]
    g = jnp.dot(xf, wg_ref[...], preferred_element_type=F32)
    u = jnp.dot(xf, wu_ref[...], preferred_element_type=F32)
    hm = (jax.nn.silu(g) * u).astype(BF16)
    o_ref[...] += jnp.dot(hm, wd_ref[...], preferred_element_type=F32)

    @pl.when(j == pl.num_programs(1) - 1)
    def _():
        o_ref[...] = h_ref[...] + mod_ref[0, 5:6, :] * o_ref[...]


def ffn(xf2, h2, mod3, w_gate, w_up, w_down, seqlen, tm=1024, tf=512):
    t, d = h2.shape
    dff = w_gate.shape[1]
    row = lambda dt: pl.BlockSpec((tm, d), lambda i, j: (i, 0), pipeline_mode=pl.Buffered(1))
    return pl.pallas_call(
        _ffn_kernel,
        grid=(t // tm, dff // tf),
        in_specs=[row(BF16), row(F32),
                  pl.BlockSpec((1, 6, d), lambda i, j: ((i * tm) // seqlen, 0, 0)),
                  pl.BlockSpec((d, tf), lambda i, j: (0, j)),
                  pl.BlockSpec((d, tf), lambda i, j: (0, j)),
                  pl.BlockSpec((tf, d), lambda i, j: (j, 0))],
        out_specs=pl.BlockSpec((tm, d), lambda i, j: (i, 0)),
        out_shape=jax.ShapeDtypeStruct((t, d), F32),
        compiler_params=_cparams(2),
        name="ffn",
    )(xf2, h2, mod3, w_gate, w_up, w_down)


def s5_mixer_chunked(u_tm, a_re, a_im, log_dt, b_re, b_im, c_re, c_im, d_skip):
    bsz, _, ngt, nc, _ = u_tm.shape
    g, p = a_re.shape
    h = c_re.shape[1]
    assert h * CHUNK == LANE_TILE and g * h == ngt * LANE_TILE
    bl, cl, kx, a = s5_prep(a_re, a_im, log_dt, b_re, b_im, c_re, c_im, d_skip)
    s = s5_chunk_inputs(u_tm, bl.reshape(CHUNK, 2, g * h, p), h)
    x = s5_chunk_scan(s, a.reshape(2, 1, g * p))
    return s5_outputs(u_tm, kx.reshape(g * h, CHUNK * h), cl.reshape(CHUNK, 2, g * h, p), x)


def kernel(x, c, w_ada, b_ada, g_mix, w_in, a_re, a_im, log_dt, b_re, b_im, c_re, c_im, d_skip,
           w_glu, q_gain, k_gain, g_ssm_out, g_attn_out, w_out, g_ffn, w_gate, w_up, w_down):
    bsz, seqlen, d = x.shape
    depth = w_ada.shape[0]
    width = w_glu.shape[1]
    h2 = x.reshape(bsz * seqlen, d)
    for l in range(depth):
        mod3 = ada_mod(c, w_ada[l], b_ada[l]).reshape(bsz, 6, d)
        u_tm, qkv = in_proj(h2, mod3, g_mix[l], w_in[l].astype(BF16), q_gain[l], k_gain[l], bsz, seqlen)
        y_tm = s5_mixer_chunked(u_tm, a_re[l], a_im[l], log_dt[l], b_re[l], b_im[l],
                                c_re[l], c_im[l], d_skip[l])
        att, (wglu_bf, wout_bf, wg_bf, wu_bf, wd_bf) = stick_attention(
            qkv.reshape(bsz, seqlen, 3 * width), width,
            cast=(w_glu[l], w_out[l], w_gate[l], w_up[l], w_down[l]))
        h2, xf = out_proj(y_tm, att.reshape(bsz * seqlen, width), h2, mod3,
                          wglu_bf, g_ssm_out[l], g_attn_out[l], wout_bf, g_ffn[l], seqlen)
        h2 = ffn(xf, h2, mod3, wg_bf, wu_bf, wd_bf, seqlen)
    return h2.reshape(bsz, seqlen, d)
```

```python
import functools

import jax
import jax.numpy as jnp
from jax import lax
from jax.experimental import pallas as pl
from jax.experimental.pallas import tpu as pltpu

F32 = jnp.float32
BF16 = jnp.bfloat16
EPS = 1e-6
HEAD_DIM = 64
CHUNK = 16
LANE_TILE = 256
VMEM_LIMIT = 56 * 1024 * 1024
Q_SCALE = HEAD_DIM ** -0.5 * 1.4426950408889634


def _cparams(n_axes, vmem=VMEM_LIMIT):
    return pltpu.CompilerParams(dimension_semantics=("arbitrary",) * n_axes,
                                vmem_limit_bytes=vmem)


def _rms(x):
    return x * lax.rsqrt(jnp.mean(x * x, axis=-1, keepdims=True) + EPS)


def _ada_kernel(c_ref, w_ref, b_ref, o_ref):
    d, tn = w_ref.shape
    rows = 256
    for b in range(c_ref.shape[0]):
        acc = jnp.zeros((8, tn), F32)
        for kc in range(d // rows):
            cond = jax.nn.silu(c_ref[b, kc * rows:(kc + 1) * rows, :])
            blk = cond * w_ref[kc * rows:(kc + 1) * rows, :]
            acc = acc + jnp.sum(blk.reshape(rows // 8, 8, tn), axis=0)
        o_ref[b:b + 1, :] = jnp.sum(acc, axis=0, keepdims=True) + b_ref[...]


def ada_mod(c, w_ada, b_ada, tn=1024):
    bsz, d = c.shape
    n = w_ada.shape[1]
    return pl.pallas_call(
        _ada_kernel,
        grid=(n // tn,),
        in_specs=[pl.BlockSpec((bsz, d, 1), lambda j: (0, 0, 0)),
                  pl.BlockSpec((d, tn), lambda j: (0, j)),
                  pl.BlockSpec((1, tn), lambda j: (0, j))],
        out_specs=pl.BlockSpec((bsz, tn), lambda j: (0, j)),
        out_shape=jax.ShapeDtypeStruct((bsz, n), F32),
        compiler_params=_cparams(1),
        name="ada",
    )(c.reshape(bsz, d, 1), w_ada, b_ada.reshape(1, n))


def _head_rmsnorm(r, ones_ref):
    outs = []
    for s in range(r.shape[1] // LANE_TILE):
        rs = r[:, s * LANE_TILE:(s + 1) * LANE_TILE]
        ss = jnp.dot((rs * rs).astype(BF16), ones_ref[...], preferred_element_type=F32)
        outs.append(rs * lax.rsqrt(ss * (1.0 / HEAD_DIM) + EPS))
    return jnp.concatenate(outs, axis=1)


def _proj_kernel(x_ref, mod_ref, g_ref, w_ref, qg_ref, kg_ref, ones_ref, u_ref, o_ref, us_ref, *, rows):
    tm = x_ref.shape[0]
    tn = w_ref.shape[1] // 4
    cpr = rows // CHUNK
    for r0 in range(0, tm, rows):
        xn = _rms(x_ref[r0:r0 + rows, :]) * g_ref[...]
        xm = (xn * (1.0 + mod_ref[0, 1:2, :]) + mod_ref[0, 0:1, :]).astype(BF16)
        for j in range(4):
            r = jnp.dot(xm, w_ref[:, j * tn:(j + 1) * tn], preferred_element_type=F32)
            if j == 0:
                c0 = r0 // CHUNK
                for k in range(tn // 128):
                    us_ref[k] = r[:, k * 128:(k + 1) * 128]
                for s in range(CHUNK):
                    for k in range(tn // 128):
                        us = us_ref[k, pl.ds(s, cpr, stride=CHUNK), :]
                        u_ref[0, s, k // 2, c0:c0 + cpr, (k % 2) * 128:(k % 2 + 1) * 128] = us.astype(BF16)
                continue
            if j == 1:
                r = (_head_rmsnorm(r, ones_ref) * qg_ref[...]) * Q_SCALE
            elif j == 2:
                r = _head_rmsnorm(r, ones_ref) * kg_ref[...]
            o_ref[r0:r0 + rows, (j - 1) * tn:j * tn] = r.astype(BF16)


def in_proj(x2, mod3, g_mix, w_in_bf, q_gain, k_gain, bsz, seqlen, tm=512, rows=256):
    t, d = x2.shape
    n = w_in_bf.shape[1]
    tn = n // 4
    reps = tn // HEAD_DIM
    qg = jnp.tile(q_gain.reshape(1, HEAD_DIM), (1, reps))
    kg = jnp.tile(k_gain.reshape(1, HEAD_DIM), (1, reps))
    idx = jnp.arange(LANE_TILE) // HEAD_DIM
    ones = (idx[:, None] == idx[None, :]).astype(BF16)
    const = lambda shape: pl.BlockSpec(shape, lambda i: (0, 0), pipeline_mode=pl.Buffered(1))
    tpb = seqlen // tm
    ngt = tn // LANE_TILE
    return pl.pallas_call(
        functools.partial(_proj_kernel, rows=rows),
        grid=(t // tm,),
        in_specs=[pl.BlockSpec((tm, d), lambda i: (i, 0)),
                  pl.BlockSpec((1, 6, d), lambda i: (i // tpb, 0, 0)),
                  const((1, d)), const((d, n)), const((1, tn)), const((1, tn)),
                  const((LANE_TILE, LANE_TILE))],
        out_specs=[pl.BlockSpec((1, CHUNK, ngt, tm // CHUNK, LANE_TILE), lambda i: (i // tpb, 0, 0, i % tpb, 0)),
                   pl.BlockSpec((tm, 3 * tn), lambda i: (i, 0))],
        out_shape=[jax.ShapeDtypeStruct((bsz, CHUNK, ngt, seqlen // CHUNK, LANE_TILE), BF16),
                   jax.ShapeDtypeStruct((t, 3 * tn), BF16)],
        scratch_shapes=[pltpu.VMEM((tn // 128, rows, 128), F32)],
        compiler_params=_cparams(1),
        name="proj",
    )(x2, mod3, g_mix.reshape(1, d), w_in_bf, qg, kg, ones)


def _zoh(ar, ai, dt):
    mag = jnp.exp(ar * dt)
    th = ai * dt
    lbr, lbi = mag * jnp.cos(th), mag * jnp.sin(th)
    den = ar * ar + ai * ai
    nr, ni = lbr - 1.0, lbi
    return lbr, lbi, (nr * ar + ni * ai) / den, (ni * ar - nr * ai) / den


def _s5prep_kernel(ar_ref, ai_ref, ldt_ref, btr_ref, bti_ref, cr_ref, ci_ref, d_ref,
                   bl_ref, cl_ref, kx_ref, a_ref):
    dt = jnp.exp(ldt_ref[...])
    lbr, lbi, kr, ki = _zoh(ar_ref[...], ai_ref[...], dt)
    btr, bti = btr_ref[...], bti_ref[...]
    bbr = kr * btr - ki * bti
    bbi = kr * bti + ki * btr
    c_re, c_im = cr_ref[...], ci_ref[...]
    h = c_re.shape[1]
    eye = (lax.broadcasted_iota(jnp.int32, (1, h, h), 1) ==
           lax.broadcasted_iota(jnp.int32, (1, h, h), 2))
    dn = (((2,), (2,)), ((0,), (0,)))
    pr, pi = jnp.ones_like(lbr), jnp.zeros_like(lbr)
    blrs, blis = [], []
    for j in range(CHUNK):
        blr = pr * bbr - pi * bbi
        bli = pr * bbi + pi * bbr
        bl_ref[j, 0] = blr
        bl_ref[j, 1] = bli
        blrs.append(blr)
        blis.append(bli)
        pr, pi = pr * lbr - pi * lbi, pr * lbi + pi * lbr
        cl_ref[j, 0] = c_re * pr - c_im * pi
        cl_ref[j, 1] = -(c_re * pi + c_im * pr)
    a_ref[0] = pr
    a_ref[1] = pi
    kx = (lax.dot_general(jnp.concatenate(blrs, axis=1), c_re, dn, precision=lax.Precision.HIGHEST,
                          preferred_element_type=F32)
          - lax.dot_general(jnp.concatenate(blis, axis=1), c_im, dn, precision=lax.Precision.HIGHEST,
                            preferred_element_type=F32))
    for j in range(CHUNK):
        kxj = kx[:, j * h:(j + 1) * h, :]
        if j == 0:
            kxj = kxj + jnp.where(eye, d_ref[...], 0.0)
        kx_ref[:, :, j * h:(j + 1) * h] = kxj


def s5_prep(a_re, a_im, log_dt, b_re, b_im, c_re, c_im, d_skip, gb=8):
    g, p = a_re.shape
    h = c_re.shape[1]
    vec = pl.BlockSpec((gb, 1, p), lambda i: (i, 0, 0))
    mat = pl.BlockSpec((gb, h, p), lambda i: (i, 0, 0))
    lag = pl.BlockSpec((CHUNK, 2, gb, h, p), lambda i: (0, 0, i, 0, 0))
    lag_shape = jax.ShapeDtypeStruct((CHUNK, 2, g, h, p), F32)
    return pl.pallas_call(
        _s5prep_kernel,
        grid=(g // gb,),
        in_specs=[vec, vec, pl.BlockSpec((gb, 1, 1), lambda i: (i, 0, 0)),
                  mat, mat, mat, mat, pl.BlockSpec((gb, 1, h), lambda i: (i, 0, 0))],
        out_specs=[lag, lag,
                   pl.BlockSpec((gb, h, CHUNK * h), lambda i: (i, 0, 0)),
                   pl.BlockSpec((2, gb, 1, p), lambda i: (0, i, 0, 0))],
        out_shape=[lag_shape, lag_shape,
                   jax.ShapeDtypeStruct((g, h, CHUNK * h), F32),
                   jax.ShapeDtypeStruct((2, g, 1, p), F32)],
        compiler_params=_cparams(1),
        name="s5prep",
    )(a_re.reshape(g, 1, p), a_im.reshape(g, 1, p), log_dt.reshape(g, 1, 1),
      b_re.transpose(0, 2, 1), b_im.transpose(0, 2, 1), c_re, c_im, d_skip.reshape(g, 1, h))


def _group_mask(shape, row_div, col_div):
    return (lax.broadcasted_iota(jnp.int32, shape, 0) // row_div ==
            lax.broadcasted_iota(jnp.int32, shape, 1) // col_div)


def _s5in_kernel(u_ref, bl_ref, o_ref, w_ref):
    p, wid = bl_ref.shape[3], o_ref.shape[3]
    h = LANE_TILE // (wid // p)
    mask = _group_mask((LANE_TILE, wid), h, p)
    for s in range(CHUNK):
        ex = jnp.concatenate([bl_ref[CHUNK - 1 - s, 0]] * (wid // p), axis=1)
        w_ref[s * LANE_TILE:(s + 1) * LANE_TILE, :] = jnp.where(mask, ex, 0.0).astype(BF16)
    for b in range(u_ref.shape[0]):
        ucat = jnp.concatenate([u_ref[b, s, 0] for s in range(CHUNK)], axis=1)
        o_ref[b, 0] = jnp.dot(ucat, w_ref[...], preferred_element_type=F32)


def s5_chunk_inputs(u_tm, bl2, h):
    bsz, _, ngt, nc, _ = u_tm.shape
    p = bl2.shape[3]
    wid = (LANE_TILE // h) * p
    return pl.pallas_call(
        _s5in_kernel,
        grid=(ngt, 2),
        in_specs=[pl.BlockSpec((bsz, CHUNK, 1, nc, LANE_TILE), lambda gt, ri: (0, 0, gt, 0, 0)),
                  pl.BlockSpec((CHUNK, 1, LANE_TILE, p), lambda gt, ri: (0, ri, gt, 0))],
        out_specs=pl.BlockSpec((bsz, 1, nc, wid), lambda gt, ri: (0, ri, 0, gt)),
        out_shape=jax.ShapeDtypeStruct((bsz, 2, nc, ngt * wid), F32),
        scratch_shapes=[pltpu.VMEM((CHUNK * LANE_TILE, wid), BF16)],
        compiler_params=_cparams(2),
        name="s5in",
    )(u_tm, bl2)


def _chunk_scan(s_ref, a_ref, x_ref, b):
    nc, w = s_ref.shape[2], s_ref.shape[3]
    a1r, a1i = a_ref[0], a_ref[1]

    def cmul(xr, xi, yr, yi):
        return xr * yr - xi * yi, xr * yi + xi * yr

    row = lax.broadcasted_iota(jnp.int32, (8, w), 0)
    pows = [(a1r, a1i)]
    for _ in range(7):
        pows.append(cmul(pows[-1][0], pows[-1][1], a1r, a1i))
    steps = []
    for k in (1, 2, 4):
        steps.append((k, jnp.where(row >= k, pows[k - 1][0], 0.0), jnp.where(row >= k, pows[k - 1][1], 0.0)))
    pcr = jnp.zeros((8, w), F32)
    pci = jnp.zeros((8, w), F32)
    for r in range(8):
        pcr = jnp.where(row == r, pows[r][0], pcr)
        pci = jnp.where(row == r, pows[r][1], pci)

    def body(blk, carry):
        cr, ci = carry
        off = pl.multiple_of(blk * 8, 8)
        xr = s_ref[b, 0, pl.ds(off, 8), :]
        xi = s_ref[b, 1, pl.ds(off, 8), :]
        for k, kr, ki in steps:
            sr, si = pltpu.roll(xr, k, 0), pltpu.roll(xi, k, 0)
            dr, di = cmul(sr, si, kr, ki)
            xr, xi = xr + dr, xi + di
        dr, di = cmul(jnp.broadcast_to(cr, (8, w)), jnp.broadcast_to(ci, (8, w)), pcr, pci)
        xr, xi = xr + dr, xi + di
        x_ref[b, 0, pl.ds(off, 8), :] = jnp.where(row == 0, cr, pltpu.roll(xr, 1, 0))
        x_ref[b, 1, pl.ds(off, 8), :] = jnp.where(row == 0, ci, pltpu.roll(xi, 1, 0))
        return xr[7:8, :], xi[7:8, :]

    zero = jnp.zeros((1, w), F32)
    lax.fori_loop(0, nc // 8, body, (zero, zero))


def _s5out_kernel(u_ref, kx_ref, cl_ref, s_ref, a_ref, e_ref, o_ref, tr_ref, x_ref):
    bsz, _, _, nc, _ = u_ref.shape
    p, wid = cl_ref.shape[3], x_ref.shape[3]
    h = LANE_TILE // CHUNK
    mask_t = _group_mask((LANE_TILE, LANE_TILE), h, h)
    mask_c = _group_mask((LANE_TILE, wid), h, p)
    nt = (((1,), (1,)), ((), ()))
    for b in range(bsz):
        _chunk_scan(s_ref, a_ref, x_ref, b)
    kx = kx_ref[...].astype(BF16)
    for j in range(CHUNK):
        tj = jnp.where(mask_t, jnp.dot(kx, e_ref[j], preferred_element_type=F32), 0.0)
        tr_ref[(CHUNK - 1 - j) * LANE_TILE:(CHUNK - j) * LANE_TILE, :] = tj.astype(BF16)
    ucat = jnp.concatenate(
        [jnp.concatenate([u_ref[b, s, 0] for s in range(CHUNK)], axis=1) for b in range(bsz)], axis=0)
    xr = x_ref[:, 0].reshape(bsz * nc, wid).astype(BF16)
    xi = x_ref[:, 1].reshape(bsz * nc, wid).astype(BF16)
    for t in range(CHUNK):
        wr = jnp.concatenate([cl_ref[t, 0]] * (wid // p), axis=1)
        wi = jnp.concatenate([cl_ref[t, 1]] * (wid // p), axis=1)
        wr = jnp.where(mask_c, wr, 0.0).astype(BF16)
        wi = jnp.where(mask_c, wi, 0.0).astype(BF16)
        y = (jnp.dot(ucat[:, :(t + 1) * LANE_TILE], tr_ref[(CHUNK - 1 - t) * LANE_TILE:, :],
                     preferred_element_type=F32)
             + lax.dot_general(xr, wr, nt, preferred_element_type=F32)
             + lax.dot_general(xi, wi, nt, preferred_element_type=F32))
        for b in range(bsz):
            o_ref[b, t, 0] = y[b * nc:(b + 1) * nc].astype(BF16)


def s5_outputs(u_tm, kx2, cl2, s, a):
    bsz, _, ngt, nc, _ = u_tm.shape
    wid = s.shape[3] // ngt
    p = cl2.shape[3]
    once = pl.Buffered(1)
    h = LANE_TILE // CHUNK
    r = jnp.arange(LANE_TILE)
    e = ((r[None, :, None] // h == jnp.arange(CHUNK)[:, None, None])
         & (r[None, :, None] % h == r[None, None, :] % h)).astype(BF16)
    tile = pl.BlockSpec((bsz, CHUNK, 1, nc, LANE_TILE), lambda gt: (0, 0, gt, 0, 0))
    return pl.pallas_call(
        _s5out_kernel,
        grid=(ngt,),
        in_specs=[tile,
                  pl.BlockSpec((LANE_TILE, LANE_TILE), lambda gt: (gt, 0)),
                  pl.BlockSpec((CHUNK, 2, LANE_TILE, p), lambda gt: (0, 0, gt, 0)),
                  pl.BlockSpec((bsz, 2, nc, wid), lambda gt: (0, 0, 0, gt)),
                  pl.BlockSpec((2, 1, wid), lambda gt: (0, 0, gt)),
                  pl.BlockSpec((CHUNK, LANE_TILE, LANE_TILE), lambda gt: (0, 0, 0), pipeline_mode=once)],
        out_specs=tile,
        out_shape=jax.ShapeDtypeStruct(u_tm.shape, BF16),
        scratch_shapes=[pltpu.VMEM((CHUNK * LANE_TILE, LANE_TILE), BF16),
                        pltpu.VMEM((bsz, 2, nc, wid), F32)],
        compiler_params=_cparams(1),
        name="s5out",
    )(u_tm, kx2, cl2, s, a, e)


def _attn_kernel(q_ref, k_ref, v_ref, tri_ref, *rest, blk, nh, unroll, ncast):
    src_refs, o_ref, dst_refs, vcat_ref = rest[:ncast], rest[ncast], rest[ncast + 1:2 * ncast + 1], rest[-1]
    for src, dst in zip(src_refs, dst_refs):
        dst[...] = src[...].astype(BF16)
    qi = pl.program_id(2)
    q = q_ref[0]
    lane = lax.broadcasted_iota(jnp.int32, q.shape, 1)
    heads = [(lane >= HEAD_DIM * h) & (lane < HEAD_DIM * (h + 1)) for h in range(nh)]
    zero = jnp.zeros_like(q)

    @pl.when(qi == 0)
    def _():
        def fill(j, _):
            vb = v_ref[0, pl.ds(pl.multiple_of(j * blk, blk), blk), :]
            vcat_ref[j] = jnp.concatenate([jnp.where(m, vb, zero) for m in heads], axis=0)
            return 0

        lax.fori_loop(0, vcat_ref.shape[0], fill, 0)

    qs = jnp.concatenate([jnp.where(m, q, zero) for m in heads], axis=0)
    row = lax.broadcasted_iota(jnp.int32, (blk, blk), 0)
    col = lax.broadcasted_iota(jnp.int32, (blk, blk), 1)
    past = jnp.concatenate([col < row] * nh, axis=0)
    nt = (((1,), (1,)), ((), ()))

    def block(kj, carry, acc, diag):
        kb = k_ref[0, pl.ds(pl.multiple_of(kj * blk, blk), blk), :]
        z = lax.dot_general(qs, kb, nt, preferred_element_type=F32)
        sp = jnp.maximum(z, 0.0) + jnp.log2(1.0 + jnp.exp2(-jnp.abs(z)))
        if diag:
            sp = jnp.where(past, sp, 0.0)
        cs = lax.dot_general(sp, tri_ref[...], (((1,), (0,)), ((), ())), preferred_element_type=F32)
        w = jnp.exp2(((z - sp) - cs) - carry)
        if diag:
            w = jnp.where(past, w, 0.0)
        wcat = jnp.concatenate([w[h * blk:(h + 1) * blk] for h in range(nh)], axis=1)
        acc = acc + lax.dot_general(wcat, vcat_ref[kj], (((1,), (0,)), ((), ())), preferred_element_type=F32)
        return carry + (cs[:, 0:1] + sp[:, 0:1]), acc

    def head(r):
        def run():
            c = block(qi, jnp.zeros((nh * blk, 1), F32), jnp.zeros(q.shape, F32), True)
            for i in range(r):
                c = block(qi - 1 - i, c[0], c[1], False)
            return c
        return run

    rem = qi % unroll
    c = lax.switch(rem, [head(r) for r in range(unroll)])

    left = qi - rem
    for n in (2 * unroll, unroll):
        def group(i, c, n=n, left=left):
            for r in range(n):
                c = block(left - 1 - r - n * i, c[0], c[1], False)
            return c

        c = lax.fori_loop(0, left // n, group, c)
        left = left % n
    o_ref[0] = c[1].astype(BF16)


def stick_attention(qkv3, width, cast=(), blk=256, nh=4, unroll=4):
    bsz, seqlen, _ = qkv3.shape
    wl = nh * HEAD_DIM
    ngrp = width // wl
    nq = seqlen // blk
    ii = jnp.arange(blk)
    tri = (ii[:, None] > ii[None, :]).astype(BF16)
    nsteps = bsz * ngrp * nq

    def slab_spec(w):
        rows, cols = w.shape
        for c in range(1, cols // 128 + 1):
            r = nsteps // c
            if cols % (128 * c) == 0 and nsteps % c == 0 and rows % (16 * r) == 0:
                return pl.BlockSpec((rows // r, cols // c),
                                    lambda b, h, i, c=c: (((b * ngrp + h) * nq + i) // c, ((b * ngrp + h) * nq + i) % c))
        raise ValueError(f"cannot cut {w.shape} into {nsteps} blocks")

    slab_specs = [slab_spec(w) for w in cast]
    outs = pl.pallas_call(
        functools.partial(_attn_kernel, blk=blk, nh=nh, unroll=unroll, ncast=len(cast)),
        grid=(bsz, ngrp, nq),
        in_specs=[pl.BlockSpec((1, blk, wl), lambda b, h, i: (b, i, h)),
                  pl.BlockSpec((1, seqlen, wl), lambda b, h, i: (b, 0, ngrp + h)),
                  pl.BlockSpec((1, seqlen, wl), lambda b, h, i: (b, 0, 2 * ngrp + h)),
                  pl.BlockSpec((blk, blk), lambda b, h, i: (0, 0))] + slab_specs,
        out_specs=[pl.BlockSpec((1, blk, wl), lambda b, h, i: (b, i, h))] + slab_specs,
        out_shape=[jax.ShapeDtypeStruct((bsz, seqlen, width), BF16)]
                  + [jax.ShapeDtypeStruct(w.shape, BF16) for w in cast],
        scratch_shapes=[pltpu.VMEM((nq, nh * blk, wl), BF16)],
        compiler_params=_cparams(3),
        name="attn",
    )(qkv3, qkv3, qkv3, tri, *cast)
    return outs[0], list(outs[1:])


def _out_kernel(y_ref, a_ref, x_ref, mod_ref, wglu_ref, gs_ref, ga_ref, w0_ref, w1_ref, gf_ref,
                h_ref, xf_ref, ys_ref, *, rows):
    cpr = rows // CHUNK
    for r0 in range(0, a_ref.shape[0], rows):
        rs = slice(r0, r0 + rows)
        c0 = r0 // CHUNK
        nk = ys_ref.shape[0]
        for s in range(CHUNK):
            for k in range(nk):
                ys_ref[k, pl.ds(s, cpr, stride=CHUNK), :] = (
                    y_ref[0, s, k // 2, c0:c0 + cpr, (k % 2) * 128:(k % 2 + 1) * 128].astype(F32))
        y = jax.nn.gelu(jnp.concatenate([ys_ref[k] for k in range(nk)], axis=1))
        z = jnp.dot(y.astype(BF16), wglu_ref[...], preferred_element_type=F32)
        ys = y * jax.nn.sigmoid(z)
        ysn = (_rms(ys) * gs_ref[...]).astype(BF16)
        an = (_rms(a_ref[rs, :].astype(F32)) * ga_ref[...]).astype(BF16)
        o = (jnp.dot(ysn, w0_ref[...], preferred_element_type=F32)
             + jnp.dot(an, w1_ref[...], preferred_element_type=F32))
        h = x_ref[rs, :] + mod_ref[0, 2:3, :] * o
        h_ref[rs, :] = h
        xn = _rms(h) * gf_ref[...]
        xf_ref[rs, :] = (xn * (1.0 + mod_ref[0, 4:5, :]) + mod_ref[0, 3:4, :]).astype(BF16)


def out_proj(y_tm, a2, x2, mod3, w_glu_bf, g_ssm, g_attn, w_out_bf, g_ffn, seqlen, tm=512, rows=256):
    t, d = x2.shape
    ngt = y_tm.shape[2]
    ws = ngt * LANE_TILE
    wa = a2.shape[1]
    tpb = seqlen // tm
    const = lambda shape, r=0: pl.BlockSpec(shape, lambda i: (r, 0), pipeline_mode=pl.Buffered(1))
    return pl.pallas_call(
        functools.partial(_out_kernel, rows=rows),
        grid=(t // tm,),
        in_specs=[pl.BlockSpec((1, CHUNK, ngt, tm // CHUNK, LANE_TILE), lambda i: (i // tpb, 0, 0, i % tpb, 0)),
                  pl.BlockSpec((tm, wa), lambda i: (i, 0)),
                  pl.BlockSpec((tm, d), lambda i: (i, 0)),
                  pl.BlockSpec((1, 6, d), lambda i: (i // tpb, 0, 0)),
                  const((ws, ws)), const((1, ws)), const((1, wa)),
                  const((ws, d)), const((wa, d), 1), const((1, d))],
        out_specs=[pl.BlockSpec((tm, d), lambda i: (i, 0)),
                   pl.BlockSpec((tm, d), lambda i: (i, 0))],
        out_shape=[jax.ShapeDtypeStruct((t, d), F32), jax.ShapeDtypeStruct((t, d), BF16)],
        scratch_shapes=[pltpu.VMEM((ws // 128, rows, 128), F32)],
        compiler_params=_cparams(1),
        name="outproj",
    )(y_tm, a2, x2, mod3, w_glu_bf, g_ssm.reshape(1, ws), g_attn.reshape(1, wa), w_out_bf, w_out_bf,
      g_ffn.reshape(1, d))


def _ffn_kernel(xf_ref, h_ref, mod_ref, wg_ref, wu_ref, wd_ref, o_ref):
    j = pl.program_id(1)

    @pl.when(j == 0)
    def _():
        o_ref[...] = jnp.zeros_like(o_ref)

    xf = xf_ref[...]
    g = jnp.dot(xf, wg_ref[...], preferred_element_type=F32)
    u = jnp.dot(xf, wu_ref[...], preferred_element_type=F32)
    hm = (jax.nn.silu(g) * u).astype(BF16)
    o_ref[...] += jnp.dot(hm, wd_ref[...], preferred_element_type=F32)

    @pl.when(j == pl.num_programs(1) - 1)
    def _():
        o_ref[...] = h_ref[...] + mod_ref[0, 5:6, :] * o_ref[...]


def ffn(xf2, h2, mod3, w_gate, w_up, w_down, seqlen, tm=1024, tf=512):
    t, d = h2.shape
    dff = w_gate.shape[1]
    row = lambda dt: pl.BlockSpec((tm, d), lambda i, j: (i, 0), pipeline_mode=pl.Buffered(1))
    return pl.pallas_call(
        _ffn_kernel,
        grid=(t // tm, dff // tf),
        in_specs=[row(BF16), row(F32),
                  pl.BlockSpec((1, 6, d), lambda i, j: ((i * tm) // seqlen, 0, 0)),
                  pl.BlockSpec((d, tf), lambda i, j: (0, j)),
                  pl.BlockSpec((d, tf), lambda i, j: (0, j)),
                  pl.BlockSpec((tf, d), lambda i, j: (j, 0))],
        out_specs=pl.BlockSpec((tm, d), lambda i, j: (i, 0)),
        out_shape=jax.ShapeDtypeStruct((t, d), F32),
        compiler_params=_cparams(2),
        name="ffn",
    )(xf2, h2, mod3, w_gate, w_up, w_down)


def s5_mixer_chunked(u_tm, a_re, a_im, log_dt, b_re, b_im, c_re, c_im, d_skip):
    bsz, _, ngt, nc, _ = u_tm.shape
    g, p = a_re.shape
    h = c_re.shape[1]
    assert h * CHUNK == LANE_TILE and g * h == ngt * LANE_TILE
    bl, cl, kx, a = s5_prep(a_re, a_im, log_dt, b_re, b_im, c_re, c_im, d_skip)
    s = s5_chunk_inputs(u_tm, bl.reshape(CHUNK, 2, g * h, p), h)
    return s5_outputs(u_tm, kx.reshape(g * h, CHUNK * h), cl.reshape(CHUNK, 2, g * h, p), s,
                      a.reshape(2, 1, g * p))


def kernel(x, c, w_ada, b_ada, g_mix, w_in, a_re, a_im, log_dt, b_re, b_im, c_re, c_im, d_skip,
           w_glu, q_gain, k_gain, g_ssm_out, g_attn_out, w_out, g_ffn, w_gate, w_up, w_down):
    bsz, seqlen, d = x.shape
    depth = w_ada.shape[0]
    width = w_glu.shape[1]
    h2 = x.reshape(bsz * seqlen, d)
    for l in range(depth):
        mod3 = ada_mod(c, w_ada[l], b_ada[l]).reshape(bsz, 6, d)
        u_tm, qkv = in_proj(h2, mod3, g_mix[l], w_in[l].astype(BF16), q_gain[l], k_gain[l], bsz, seqlen)
        y_tm = s5_mixer_chunked(u_tm, a_re[l], a_im[l], log_dt[l], b_re[l], b_im[l],
                                c_re[l], c_im[l], d_skip[l])
        att, (wglu_bf, wout_bf, wg_bf, wu_bf, wd_bf) = stick_attention(
            qkv.reshape(bsz, seqlen, 3 * width), width,
            cast=(w_glu[l], w_out[l], w_gate[l], w_up[l], w_down[l]))
        h2, xf = out_proj(y_tm, att.reshape(bsz * seqlen, width), h2, mod3,
                          wglu_bf, g_ssm_out[l], g_attn_out[l], wout_bf, g_ffn[l], seqlen)
        h2 = ffn(xf, h2, mod3, wg_bf, wu_bf, wd_bf, seqlen)
    return h2.reshape(bsz, seqlen, d)
```

```python
import functools

import jax
import jax.numpy as jnp
from jax import lax
from jax.experimental import pallas as pl
from jax.experimental.pallas import tpu as pltpu

F32 = jnp.float32
BF16 = jnp.bfloat16
EPS = 1e-6
HEAD_DIM = 64
CHUNK = 16
LANE_TILE = 256
VMEM_LIMIT = 56 * 1024 * 1024
Q_SCALE = HEAD_DIM ** -0.5 * 1.4426950408889634


def _cparams(n_axes, vmem=VMEM_LIMIT):
    return pltpu.CompilerParams(dimension_semantics=("arbitrary",) * n_axes,
                                vmem_limit_bytes=vmem)


def _rms(x):
    return x * lax.rsqrt(jnp.mean(x * x, axis=-1, keepdims=True) + EPS)


def _ada_kernel(c_ref, w_ref, b_ref, o_ref):
    d, tn = w_ref.shape
    rows = 256
    for b in range(c_ref.shape[0]):
        acc = jnp.zeros((8, tn), F32)
        for kc in range(d // rows):
            cond = jax.nn.silu(c_ref[b, kc * rows:(kc + 1) * rows, :])
            blk = cond * w_ref[kc * rows:(kc + 1) * rows, :]
            acc = acc + jnp.sum(blk.reshape(rows // 8, 8, tn), axis=0)
        o_ref[b:b + 1, :] = jnp.sum(acc, axis=0, keepdims=True) + b_ref[...]


def ada_mod(c, w_ada, b_ada, tn=1024):
    bsz, d = c.shape
    n = w_ada.shape[1]
    return pl.pallas_call(
        _ada_kernel,
        grid=(n // tn,),
        in_specs=[pl.BlockSpec((bsz, d, 1), lambda j: (0, 0, 0)),
                  pl.BlockSpec((d, tn), lambda j: (0, j)),
                  pl.BlockSpec((1, tn), lambda j: (0, j))],
        out_specs=pl.BlockSpec((bsz, tn), lambda j: (0, j)),
        out_shape=jax.ShapeDtypeStruct((bsz, n), F32),
        compiler_params=_cparams(1),
        name="ada",
    )(c.reshape(bsz, d, 1), w_ada, b_ada.reshape(1, n))


def _head_rmsnorm(r, ones_ref):
    outs = []
    for s in range(r.shape[1] // LANE_TILE):
        rs = r[:, s * LANE_TILE:(s + 1) * LANE_TILE]
        ss = jnp.dot((rs * rs).astype(BF16), ones_ref[...], preferred_element_type=F32)
        outs.append(rs * lax.rsqrt(ss * (1.0 / HEAD_DIM) + EPS))
    return jnp.concatenate(outs, axis=1)


def _proj_kernel(x_ref, mod_ref, g_ref, w_ref, qg_ref, kg_ref, ones_ref, u_ref, o_ref, us_ref, *, rows):
    tm = x_ref.shape[0]
    tn = w_ref.shape[1] // 4
    cpr = rows // CHUNK
    for r0 in range(0, tm, rows):
        xn = _rms(x_ref[r0:r0 + rows, :]) * g_ref[...]
        xm = (xn * (1.0 + mod_ref[0, 1:2, :]) + mod_ref[0, 0:1, :]).astype(BF16)
        for j in range(4):
            r = jnp.dot(xm, w_ref[:, j * tn:(j + 1) * tn], preferred_element_type=F32)
            if j == 0:
                c0 = r0 // CHUNK
                for k in range(tn // 128):
                    us_ref[k] = r[:, k * 128:(k + 1) * 128]
                for s in range(CHUNK):
                    for k in range(tn // 128):
                        us = us_ref[k, pl.ds(s, cpr, stride=CHUNK), :]
                        u_ref[0, s, k // 2, c0:c0 + cpr, (k % 2) * 128:(k % 2 + 1) * 128] = us.astype(BF16)
                continue
            if j == 1:
                r = (_head_rmsnorm(r, ones_ref) * qg_ref[...]) * Q_SCALE
            elif j == 2:
                r = _head_rmsnorm(r, ones_ref) * kg_ref[...]
            o_ref[r0:r0 + rows, (j - 1) * tn:j * tn] = r.astype(BF16)


def in_proj(x2, mod3, g_mix, w_in_bf, q_gain, k_gain, bsz, seqlen, tm=512, rows=256):
    t, d = x2.shape
    n = w_in_bf.shape[1]
    tn = n // 4
    reps = tn // HEAD_DIM
    qg = jnp.tile(q_gain.reshape(1, HEAD_DIM), (1, reps))
    kg = jnp.tile(k_gain.reshape(1, HEAD_DIM), (1, reps))
    idx = jnp.arange(LANE_TILE) // HEAD_DIM
    ones = (idx[:, None] == idx[None, :]).astype(BF16)
    const = lambda shape: pl.BlockSpec(shape, lambda i: (0, 0), pipeline_mode=pl.Buffered(1))
    tpb = seqlen // tm
    ngt = tn // LANE_TILE
    return pl.pallas_call(
        functools.partial(_proj_kernel, rows=rows),
        grid=(t // tm,),
        in_specs=[pl.BlockSpec((tm, d), lambda i: (i, 0)),
                  pl.BlockSpec((1, 6, d), lambda i: (i // tpb, 0, 0)),
                  const((1, d)), const((d, n)), const((1, tn)), const((1, tn)),
                  const((LANE_TILE, LANE_TILE))],
        out_specs=[pl.BlockSpec((1, CHUNK, ngt, tm // CHUNK, LANE_TILE), lambda i: (i // tpb, 0, 0, i % tpb, 0)),
                   pl.BlockSpec((tm, 3 * tn), lambda i: (i, 0))],
        out_shape=[jax.ShapeDtypeStruct((bsz, CHUNK, ngt, seqlen // CHUNK, LANE_TILE), BF16),
                   jax.ShapeDtypeStruct((t, 3 * tn), BF16)],
        scratch_shapes=[pltpu.VMEM((tn // 128, rows, 128), F32)],
        compiler_params=_cparams(1),
        name="proj",
    )(x2, mod3, g_mix.reshape(1, d), w_in_bf, qg, kg, ones)


def _zoh(ar, ai, dt):
    mag = jnp.exp(ar * dt)
    th = ai * dt
    lbr, lbi = mag * jnp.cos(th), mag * jnp.sin(th)
    den = ar * ar + ai * ai
    nr, ni = lbr - 1.0, lbi
    return lbr, lbi, (nr * ar + ni * ai) / den, (ni * ar - nr * ai) / den


def _s5prep_kernel(ar_ref, ai_ref, ldt_ref, btr_ref, bti_ref, cr_ref, ci_ref, d_ref,
                   bl_ref, cl_ref, kx_ref, a_ref):
    dt = jnp.exp(ldt_ref[...])
    lbr, lbi, kr, ki = _zoh(ar_ref[...], ai_ref[...], dt)
    btr, bti = btr_ref[...], bti_ref[...]
    bbr = kr * btr - ki * bti
    bbi = kr * bti + ki * btr
    c_re, c_im = cr_ref[...], ci_ref[...]
    h = c_re.shape[1]
    eye = (lax.broadcasted_iota(jnp.int32, (1, h, h), 1) ==
           lax.broadcasted_iota(jnp.int32, (1, h, h), 2))
    dn = (((2,), (2,)), ((0,), (0,)))
    pr, pi = jnp.ones_like(lbr), jnp.zeros_like(lbr)
    blrs, blis = [], []
    for j in range(CHUNK):
        blr = pr * bbr - pi * bbi
        bli = pr * bbi + pi * bbr
        bl_ref[j, 0] = blr
        bl_ref[j, 1] = bli
        blrs.append(blr)
        blis.append(bli)
        pr, pi = pr * lbr - pi * lbi, pr * lbi + pi * lbr
        cl_ref[j, 0] = c_re * pr - c_im * pi
        cl_ref[j, 1] = -(c_re * pi + c_im * pr)
    a_ref[0] = pr
    a_ref[1] = pi
    kx = (lax.dot_general(jnp.concatenate(blrs, axis=1), c_re, dn, precision=lax.Precision.HIGHEST,
                          preferred_element_type=F32)
          - lax.dot_general(jnp.concatenate(blis, axis=1), c_im, dn, precision=lax.Precision.HIGHEST,
                            preferred_element_type=F32))
    for j in range(CHUNK):
        kxj = kx[:, j * h:(j + 1) * h, :]
        if j == 0:
            kxj = kxj + jnp.where(eye, d_ref[...], 0.0)
        kx_ref[:, :, j * h:(j + 1) * h] = kxj


def s5_prep(a_re, a_im, log_dt, b_re, b_im, c_re, c_im, d_skip, gb=8):
    g, p = a_re.shape
    h = c_re.shape[1]
    vec = pl.BlockSpec((gb, 1, p), lambda i: (i, 0, 0))
    mat = pl.BlockSpec((gb, h, p), lambda i: (i, 0, 0))
    lag = pl.BlockSpec((CHUNK, 2, gb, h, p), lambda i: (0, 0, i, 0, 0))
    lag_shape = jax.ShapeDtypeStruct((CHUNK, 2, g, h, p), F32)
    return pl.pallas_call(
        _s5prep_kernel,
        grid=(g // gb,),
        in_specs=[vec, vec, pl.BlockSpec((gb, 1, 1), lambda i: (i, 0, 0)),
                  mat, mat, mat, mat, pl.BlockSpec((gb, 1, h), lambda i: (i, 0, 0))],
        out_specs=[lag, lag,
                   pl.BlockSpec((gb, h, CHUNK * h), lambda i: (i, 0, 0)),
                   pl.BlockSpec((2, gb, 1, p), lambda i: (0, i, 0, 0))],
        out_shape=[lag_shape, lag_shape,
                   jax.ShapeDtypeStruct((g, h, CHUNK * h), F32),
                   jax.ShapeDtypeStruct((2, g, 1, p), F32)],
        compiler_params=_cparams(1),
        name="s5prep",
    )(a_re.reshape(g, 1, p), a_im.reshape(g, 1, p), log_dt.reshape(g, 1, 1),
      b_re.transpose(0, 2, 1), b_im.transpose(0, 2, 1), c_re, c_im, d_skip.reshape(g, 1, h))


def _group_mask(shape, row_div, col_div):
    return (lax.broadcasted_iota(jnp.int32, shape, 0) // row_div ==
            lax.broadcasted_iota(jnp.int32, shape, 1) // col_div)


def _s5in_kernel(u_ref, bl_ref, o_ref, w_ref):
    p, wid = bl_ref.shape[3], o_ref.shape[3]
    h = LANE_TILE // (wid // p)
    mask = _group_mask((LANE_TILE, wid), h, p)
    for s in range(CHUNK):
        ex = jnp.concatenate([bl_ref[CHUNK - 1 - s, 0]] * (wid // p), axis=1)
        w_ref[s * LANE_TILE:(s + 1) * LANE_TILE, :] = jnp.where(mask, ex, 0.0).astype(BF16)
    for b in range(u_ref.shape[0]):
        ucat = jnp.concatenate([u_ref[b, s, 0] for s in range(CHUNK)], axis=1)
        o_ref[b, 0] = jnp.dot(ucat, w_ref[...], preferred_element_type=F32)


def s5_chunk_inputs(u_tm, bl2, h):
    bsz, _, ngt, nc, _ = u_tm.shape
    p = bl2.shape[3]
    wid = (LANE_TILE // h) * p
    return pl.pallas_call(
        _s5in_kernel,
        grid=(ngt, 2),
        in_specs=[pl.BlockSpec((bsz, CHUNK, 1, nc, LANE_TILE), lambda gt, ri: (0, 0, gt, 0, 0)),
                  pl.BlockSpec((CHUNK, 1, LANE_TILE, p), lambda gt, ri: (0, ri, gt, 0))],
        out_specs=pl.BlockSpec((bsz, 1, nc, wid), lambda gt, ri: (0, ri, 0, gt)),
        out_shape=jax.ShapeDtypeStruct((bsz, 2, nc, ngt * wid), F32),
        scratch_shapes=[pltpu.VMEM((CHUNK * LANE_TILE, wid), BF16)],
        compiler_params=_cparams(2),
        name="s5in",
    )(u_tm, bl2)


def _chunk_scan(s_ref, a_ref, x_ref, b):
    nc, w = s_ref.shape[2], s_ref.shape[3]
    a1r, a1i = a_ref[0], a_ref[1]

    def cmul(xr, xi, yr, yi):
        return xr * yr - xi * yi, xr * yi + xi * yr

    row = lax.broadcasted_iota(jnp.int32, (8, w), 0)
    pows = [(a1r, a1i)]
    for _ in range(7):
        pows.append(cmul(pows[-1][0], pows[-1][1], a1r, a1i))
    steps = []
    for k in (1, 2, 4):
        steps.append((k, jnp.where(row >= k, pows[k - 1][0], 0.0), jnp.where(row >= k, pows[k - 1][1], 0.0)))
    pcr = jnp.zeros((8, w), F32)
    pci = jnp.zeros((8, w), F32)
    for r in range(8):
        pcr = jnp.where(row == r, pows[r][0], pcr)
        pci = jnp.where(row == r, pows[r][1], pci)

    def body(blk, carry):
        cr, ci = carry
        off = pl.multiple_of(blk * 8, 8)
        xr = s_ref[b, 0, pl.ds(off, 8), :]
        xi = s_ref[b, 1, pl.ds(off, 8), :]
        for k, kr, ki in steps:
            sr, si = pltpu.roll(xr, k, 0), pltpu.roll(xi, k, 0)
            dr, di = cmul(sr, si, kr, ki)
            xr, xi = xr + dr, xi + di
        dr, di = cmul(jnp.broadcast_to(cr, (8, w)), jnp.broadcast_to(ci, (8, w)), pcr, pci)
        xr, xi = xr + dr, xi + di
        x_ref[b, 0, pl.ds(off, 8), :] = jnp.where(row == 0, cr, pltpu.roll(xr, 1, 0))
        x_ref[b, 1, pl.ds(off, 8), :] = jnp.where(row == 0, ci, pltpu.roll(xi, 1, 0))
        return xr[7:8, :], xi[7:8, :]

    zero = jnp.zeros((1, w), F32)
    lax.fori_loop(0, nc // 8, body, (zero, zero))


def _s5out_kernel(u_ref, kx_ref, cl_ref, s_ref, a_ref, e_ref, o_ref, tr_ref, x_ref):
    bsz, _, _, nc, _ = u_ref.shape
    p, wid = cl_ref.shape[3], x_ref.shape[3]
    h = LANE_TILE // CHUNK
    mask_t = _group_mask((LANE_TILE, LANE_TILE), h, h)
    mask_c = _group_mask((LANE_TILE, wid), h, p)
    nt = (((1,), (1,)), ((), ()))
    for b in range(bsz):
        _chunk_scan(s_ref, a_ref, x_ref, b)
    kx = kx_ref[...].astype(BF16)
    for j in range(CHUNK):
        tj = jnp.where(mask_t, jnp.dot(kx, e_ref[j], preferred_element_type=F32), 0.0)
        tr_ref[(CHUNK - 1 - j) * LANE_TILE:(CHUNK - j) * LANE_TILE, :] = tj.astype(BF16)
    ucat = jnp.concatenate(
        [jnp.concatenate([u_ref[b, s, 0] for s in range(CHUNK)], axis=1) for b in range(bsz)], axis=0)
    xr = x_ref[:, 0].reshape(bsz * nc, wid).astype(BF16)
    xi = x_ref[:, 1].reshape(bsz * nc, wid).astype(BF16)
    for t in range(CHUNK):
        wr = jnp.concatenate([cl_ref[t, 0]] * (wid // p), axis=1)
        wi = jnp.concatenate([cl_ref[t, 1]] * (wid // p), axis=1)
        wr = jnp.where(mask_c, wr, 0.0).astype(BF16)
        wi = jnp.where(mask_c, wi, 0.0).astype(BF16)
        y = (jnp.dot(ucat[:, :(t + 1) * LANE_TILE], tr_ref[(CHUNK - 1 - t) * LANE_TILE:, :],
                     preferred_element_type=F32)
             + lax.dot_general(xr, wr, nt, preferred_element_type=F32)
             + lax.dot_general(xi, wi, nt, preferred_element_type=F32))
        for b in range(bsz):
            o_ref[b, t, 0] = y[b * nc:(b + 1) * nc].astype(BF16)


def s5_outputs(u_tm, kx2, cl2, s, a):
    bsz, _, ngt, nc, _ = u_tm.shape
    wid = s.shape[3] // ngt
    p = cl2.shape[3]
    once = pl.Buffered(1)
    h = LANE_TILE // CHUNK
    r = jnp.arange(LANE_TILE)
    e = ((r[None, :, None] // h == jnp.arange(CHUNK)[:, None, None])
         & (r[None, :, None] % h == r[None, None, :] % h)).astype(BF16)
    tile = pl.BlockSpec((bsz, CHUNK, 1, nc, LANE_TILE), lambda gt: (0, 0, gt, 0, 0))
    return pl.pallas_call(
        _s5out_kernel,
        grid=(ngt,),
        in_specs=[tile,
                  pl.BlockSpec((LANE_TILE, LANE_TILE), lambda gt: (gt, 0)),
                  pl.BlockSpec((CHUNK, 2, LANE_TILE, p), lambda gt: (0, 0, gt, 0)),
                  pl.BlockSpec((bsz, 2, nc, wid), lambda gt: (0, 0, 0, gt)),
                  pl.BlockSpec((2, 1, wid), lambda gt: (0, 0, gt)),
                  pl.BlockSpec((CHUNK, LANE_TILE, LANE_TILE), lambda gt: (0, 0, 0), pipeline_mode=once)],
        out_specs=tile,
        out_shape=jax.ShapeDtypeStruct(u_tm.shape, BF16),
        scratch_shapes=[pltpu.VMEM((CHUNK * LANE_TILE, LANE_TILE), BF16),
                        pltpu.VMEM((bsz, 2, nc, wid), F32)],
        compiler_params=_cparams(1),
        name="s5out",
    )(u_tm, kx2, cl2, s, a, e)


def _attn_kernel(q_ref, k_ref, v_ref, tri_ref, *rest, blk, nh, unroll, groups, ncast):
    src_refs, o_ref, dst_refs, vcat_ref = rest[:ncast], rest[ncast], rest[ncast + 1:2 * ncast + 1], rest[-1]
    for src, dst in zip(src_refs, dst_refs):
        dst[...] = src[...].astype(BF16)
    qi = pl.program_id(2)
    q = q_ref[0]
    lane = lax.broadcasted_iota(jnp.int32, q.shape, 1)
    heads = [(lane >= HEAD_DIM * h) & (lane < HEAD_DIM * (h + 1)) for h in range(nh)]
    zero = jnp.zeros_like(q)

    @pl.when(qi == 0)
    def _():
        def fill(j, _):
            vb = v_ref[0, pl.ds(pl.multiple_of(j * blk, blk), blk), :]
            vcat_ref[j] = jnp.concatenate([jnp.where(m, vb, zero) for m in heads], axis=0)
            return 0

        lax.fori_loop(0, vcat_ref.shape[0], fill, 0)

    qs = jnp.concatenate([jnp.where(m, q, zero) for m in heads], axis=0)
    row = lax.broadcasted_iota(jnp.int32, (blk, blk), 0)
    col = lax.broadcasted_iota(jnp.int32, (blk, blk), 1)
    past = jnp.concatenate([col < row] * nh, axis=0)
    nt = (((1,), (1,)), ((), ()))

    def block(kj, carry, acc, diag):
        kb = k_ref[0, pl.ds(pl.multiple_of(kj * blk, blk), blk), :]
        z = lax.dot_general(qs, kb, nt, preferred_element_type=F32)
        sp = jnp.maximum(z, 0.0) + jnp.log2(1.0 + jnp.exp2(-jnp.abs(z)))
        if diag:
            sp = jnp.where(past, sp, 0.0)
        cs = lax.dot_general(sp, tri_ref[...], (((1,), (0,)), ((), ())), preferred_element_type=F32)
        w = jnp.exp2(((z - sp) - cs) - carry)
        if diag:
            w = jnp.where(past, w, 0.0)
        wcat = jnp.concatenate([w[h * blk:(h + 1) * blk] for h in range(nh)], axis=1)
        acc = acc + lax.dot_general(wcat, vcat_ref[kj], (((1,), (0,)), ((), ())), preferred_element_type=F32)
        return carry + (cs[:, 0:1] + sp[:, 0:1]), acc

    def head(r):
        def run():
            c = block(qi, jnp.zeros((nh * blk, 1), F32), jnp.zeros(q.shape, F32), True)
            for i in range(r):
                c = block(qi - 1 - i, c[0], c[1], False)
            return c
        return run

    rem = qi % unroll
    c = lax.switch(rem, [head(r) for r in range(unroll)])

    left = qi - rem
    for n in groups:
        def group(i, c, n=n, left=left):
            for r in range(n):
                c = block(left - 1 - r - n * i, c[0], c[1], False)
            return c

        c = lax.fori_loop(0, left // n, group, c)
        left = left % n
    o_ref[0] = c[1].astype(BF16)


def stick_attention(qkv3, width, cast=(), blk=256, nh=4, unroll=8, groups=(8,)):
    bsz, seqlen, _ = qkv3.shape
    wl = nh * HEAD_DIM
    ngrp = width // wl
    nq = seqlen // blk
    ii = jnp.arange(blk)
    tri = (ii[:, None] > ii[None, :]).astype(BF16)
    nsteps = bsz * ngrp * nq

    def slab_spec(w):
        rows, cols = w.shape
        for c in range(1, cols // 128 + 1):
            r = nsteps // c
            if cols % (128 * c) == 0 and nsteps % c == 0 and rows % (16 * r) == 0:
                return pl.BlockSpec((rows // r, cols // c),
                                    lambda b, h, i, c=c: (((b * ngrp + h) * nq + i) // c, ((b * ngrp + h) * nq + i) % c))
        raise ValueError(f"cannot cut {w.shape} into {nsteps} blocks")

    slab_specs = [slab_spec(w) for w in cast]
    outs = pl.pallas_call(
        functools.partial(_attn_kernel, blk=blk, nh=nh, unroll=unroll, groups=groups, ncast=len(cast)),
        grid=(bsz, ngrp, nq),
        in_specs=[pl.BlockSpec((1, blk, wl), lambda b, h, i: (b, i, h)),
                  pl.BlockSpec((1, seqlen, wl), lambda b, h, i: (b, 0, ngrp + h)),
                  pl.BlockSpec((1, seqlen, wl), lambda b, h, i: (b, 0, 2 * ngrp + h)),
                  pl.BlockSpec((blk, blk), lambda b, h, i: (0, 0))] + slab_specs,
        out_specs=[pl.BlockSpec((1, blk, wl), lambda b, h, i: (b, i, h))] + slab_specs,
        out_shape=[jax.ShapeDtypeStruct((bsz, seqlen, width), BF16)]
                  + [jax.ShapeDtypeStruct(w.shape, BF16) for w in cast],
        scratch_shapes=[pltpu.VMEM((nq, nh * blk, wl), BF16)],
        compiler_params=_cparams(3),
        name="attn",
    )(qkv3, qkv3, qkv3, tri, *cast)
    return outs[0], list(outs[1:])


def _out_kernel(y_ref, a_ref, x_ref, mod_ref, wglu_ref, gs_ref, ga_ref, w0_ref, w1_ref, gf_ref,
                h_ref, xf_ref, ys_ref, *, rows):
    cpr = rows // CHUNK
    for r0 in range(0, a_ref.shape[0], rows):
        rs = slice(r0, r0 + rows)
        c0 = r0 // CHUNK
        nk = ys_ref.shape[0]
        for s in range(CHUNK):
            for k in range(nk):
                ys_ref[k, pl.ds(s, cpr, stride=CHUNK), :] = (
                    y_ref[0, s, k // 2, c0:c0 + cpr, (k % 2) * 128:(k % 2 + 1) * 128].astype(F32))
        y = jax.nn.gelu(jnp.concatenate([ys_ref[k] for k in range(nk)], axis=1))
        z = jnp.dot(y.astype(BF16), wglu_ref[...], preferred_element_type=F32)
        ys = y * jax.nn.sigmoid(z)
        ysn = (_rms(ys) * gs_ref[...]).astype(BF16)
        an = (_rms(a_ref[rs, :].astype(F32)) * ga_ref[...]).astype(BF16)
        o = (jnp.dot(ysn, w0_ref[...], preferred_element_type=F32)
             + jnp.dot(an, w1_ref[...], preferred_element_type=F32))
        h = x_ref[rs, :] + mod_ref[0, 2:3, :] * o
        h_ref[rs, :] = h
        xn = _rms(h) * gf_ref[...]
        xf_ref[rs, :] = (xn * (1.0 + mod_ref[0, 4:5, :]) + mod_ref[0, 3:4, :]).astype(BF16)


def out_proj(y_tm, a2, x2, mod3, w_glu_bf, g_ssm, g_attn, w_out_bf, g_ffn, seqlen, tm=512, rows=256):
    t, d = x2.shape
    ngt = y_tm.shape[2]
    ws = ngt * LANE_TILE
    wa = a2.shape[1]
    tpb = seqlen // tm
    const = lambda shape, r=0: pl.BlockSpec(shape, lambda i: (r, 0), pipeline_mode=pl.Buffered(1))
    return pl.pallas_call(
        functools.partial(_out_kernel, rows=rows),
        grid=(t // tm,),
        in_specs=[pl.BlockSpec((1, CHUNK, ngt, tm // CHUNK, LANE_TILE), lambda i: (i // tpb, 0, 0, i % tpb, 0)),
                  pl.BlockSpec((tm, wa), lambda i: (i, 0)),
                  pl.BlockSpec((tm, d), lambda i: (i, 0)),
                  pl.BlockSpec((1, 6, d), lambda i: (i // tpb, 0, 0)),
                  const((ws, ws)), const((1, ws)), const((1, wa)),
                  const((ws, d)), const((wa, d), 1), const((1, d))],
        out_specs=[pl.BlockSpec((tm, d), lambda i: (i, 0)),
                   pl.BlockSpec((tm, d), lambda i: (i, 0))],
        out_shape=[jax.ShapeDtypeStruct((t, d), F32), jax.ShapeDtypeStruct((t, d), BF16)],
        scratch_shapes=[pltpu.VMEM((ws // 128, rows, 128), F32)],
        compiler_params=_cparams(1),
        name="outproj",
    )(y_tm, a2, x2, mod3, w_glu_bf, g_ssm.reshape(1, ws), g_attn.reshape(1, wa), w_out_bf, w_out_bf,
      g_ffn.reshape(1, d))


def _ffn_kernel(xf_ref, h_ref, mod_ref, wg_ref, wu_ref, wd_ref, o_ref):
    j = pl.program_id(1)

    @pl.when(j == 0)
    def _():
        o_ref[...] = jnp.zeros_like(o_ref)

    xf = xf_ref[...]
    g = jnp.dot(xf, wg_ref[...], preferred_element_type=F32)
    u = jnp.dot(xf, wu_ref[...], preferred_element_type=F32)
    hm = (jax.nn.silu(g) * u).astype(BF16)
    o_ref[...] += jnp.dot(hm, wd_ref[...], preferred_element_type=F32)

    @pl.when(j == pl.num_programs(1) - 1)
    def _():
        o_ref[...] = h_ref[...] + mod_ref[0, 5:6, :] * o_ref[...]


def ffn(xf2, h2, mod3, w_gate, w_up, w_down, seqlen, tm=1024, tf=512):
    t, d = h2.shape
    dff = w_gate.shape[1]
    row = lambda dt: pl.BlockSpec((tm, d), lambda i, j: (i, 0), pipeline_mode=pl.Buffered(1))
    return pl.pallas_call(
        _ffn_kernel,
        grid=(t // tm, dff // tf),
        in_specs=[row(BF16), row(F32),
                  pl.BlockSpec((1, 6, d), lambda i, j: ((i * tm) // seqlen, 0, 0)),
                  pl.BlockSpec((d, tf), lambda i, j: (0, j)),
                  pl.BlockSpec((d, tf), lambda i, j: (0, j)),
                  pl.BlockSpec((tf, d), lambda i, j: (j, 0))],
        out_specs=pl.BlockSpec((tm, d), lambda i, j: (i, 0)),
        out_shape=jax.ShapeDtypeStruct((t, d), F32),
        compiler_params=_cparams(2),
        name="ffn",
    )(xf2, h2, mod3, w_gate, w_up, w_down)


def s5_mixer_chunked(u_tm, a_re, a_im, log_dt, b_re, b_im, c_re, c_im, d_skip):
    bsz, _, ngt, nc, _ = u_tm.shape
    g, p = a_re.shape
    h = c_re.shape[1]
    assert h * CHUNK == LANE_TILE and g * h == ngt * LANE_TILE
    bl, cl, kx, a = s5_prep(a_re, a_im, log_dt, b_re, b_im, c_re, c_im, d_skip)
    s = s5_chunk_inputs(u_tm, bl.reshape(CHUNK, 2, g * h, p), h)
    return s5_outputs(u_tm, kx.reshape(g * h, CHUNK * h), cl.reshape(CHUNK, 2, g * h, p), s,
                      a.reshape(2, 1, g * p))


def kernel(x, c, w_ada, b_ada, g_mix, w_in, a_re, a_im, log_dt, b_re, b_im, c_re, c_im, d_skip,
           w_glu, q_gain, k_gain, g_ssm_out, g_attn_out, w_out, g_ffn, w_gate, w_up, w_down):
    bsz, seqlen, d = x.shape
    depth = w_ada.shape[0]
    width = w_glu.shape[1]
    h2 = x.reshape(bsz * seqlen, d)
    for l in range(depth):
        mod3 = ada_mod(c, w_ada[l], b_ada[l]).reshape(bsz, 6, d)
        u_tm, qkv = in_proj(h2, mod3, g_mix[l], w_in[l].astype(BF16), q_gain[l], k_gain[l], bsz, seqlen)
        y_tm = s5_mixer_chunked(u_tm, a_re[l], a_im[l], log_dt[l], b_re[l], b_im[l],
                                c_re[l], c_im[l], d_skip[l])
        att, (wglu_bf, wout_bf, wg_bf, wu_bf, wd_bf) = stick_attention(
            qkv.reshape(bsz, seqlen, 3 * width), width,
            cast=(w_glu[l], w_out[l], w_gate[l], w_up[l], w_down[l]))
        h2, xf = out_proj(y_tm, att.reshape(bsz * seqlen, width), h2, mod3,
                          wglu_bf, g_ssm_out[l], g_attn_out[l], wout_bf, g_ffn[l], seqlen)
        h2 = ffn(xf, h2, mod3, wg_bf, wu_bf, wd_bf, seqlen)
    return h2.reshape(bsz, seqlen, d)
```

```python
import functools

import jax
import jax.numpy as jnp
from jax import lax
from jax.experimental import pallas as pl
from jax.experimental.pallas import tpu as pltpu

F32 = jnp.float32
BF16 = jnp.bfloat16
EPS = 1e-6
HEAD_DIM = 64
CHUNK = 16
LANES = 128
SUBLANES = 8
BF16_ROWS = 16
LANE_TILE = 256
VMEM_LIMIT = 56 * 1024 * 1024
Q_SCALE = HEAD_DIM ** -0.5 * 1.4426950408889634


def _cparams(n_axes, vmem=VMEM_LIMIT):
    return pltpu.CompilerParams(dimension_semantics=("arbitrary",) * n_axes,
                                vmem_limit_bytes=vmem)


def _rms(x):
    return x * lax.rsqrt(jnp.mean(x * x, axis=-1, keepdims=True) + EPS)


def _ada_kernel(c_ref, w_ref, b_ref, o_ref):
    d, tn = w_ref.shape
    rows = LANE_TILE
    for b in range(c_ref.shape[0]):
        acc = jnp.zeros((SUBLANES, tn), F32)
        for kc in range(d // rows):
            cond = jax.nn.silu(c_ref[b, kc * rows:(kc + 1) * rows, :])
            blk = cond * w_ref[kc * rows:(kc + 1) * rows, :]
            acc = acc + jnp.sum(blk.reshape(rows // SUBLANES, SUBLANES, tn), axis=0)
        o_ref[b:b + 1, :] = jnp.sum(acc, axis=0, keepdims=True) + b_ref[...]


def ada_mod(c, w_ada, b_ada, tn=1024):
    bsz, d = c.shape
    n = w_ada.shape[1]
    return pl.pallas_call(
        _ada_kernel,
        grid=(n // tn,),
        in_specs=[pl.BlockSpec((bsz, d, 1), lambda j: (0, 0, 0)),
                  pl.BlockSpec((d, tn), lambda j: (0, j)),
                  pl.BlockSpec((1, tn), lambda j: (0, j))],
        out_specs=pl.BlockSpec((bsz, tn), lambda j: (0, j)),
        out_shape=jax.ShapeDtypeStruct((bsz, n), F32),
        compiler_params=_cparams(1),
        name="ada",
    )(c.reshape(bsz, d, 1), w_ada, b_ada.reshape(1, n))


def _head_rmsnorm(r, ones_ref):
    outs = []
    for s in range(r.shape[1] // LANE_TILE):
        rs = r[:, s * LANE_TILE:(s + 1) * LANE_TILE]
        ss = jnp.dot((rs * rs).astype(BF16), ones_ref[...], preferred_element_type=F32)
        outs.append(rs * lax.rsqrt(ss * (1.0 / HEAD_DIM) + EPS))
    return jnp.concatenate(outs, axis=1)


def _proj_kernel(x_ref, mod_ref, g_ref, w_ref, qg_ref, kg_ref, ones_ref, u_ref, o_ref, us_ref, *, rows):
    tm = x_ref.shape[0]
    tn = w_ref.shape[1] // 4
    cpr = rows // CHUNK
    for r0 in range(0, tm, rows):
        xn = _rms(x_ref[r0:r0 + rows, :]) * g_ref[...]
        xm = (xn * (1.0 + mod_ref[0, 1:2, :]) + mod_ref[0, 0:1, :]).astype(BF16)
        for j in range(4):
            r = jnp.dot(xm, w_ref[:, j * tn:(j + 1) * tn], preferred_element_type=F32)
            if j == 0:
                c0 = r0 // CHUNK
                per = LANE_TILE // LANES
                for k in range(tn // LANES):
                    us_ref[k] = r[:, k * LANES:(k + 1) * LANES]
                for s in range(CHUNK):
                    for k in range(tn // LANES):
                        us = us_ref[k, pl.ds(s, cpr, stride=CHUNK), :]
                        u_ref[0, s, k // per, c0:c0 + cpr, (k % per) * LANES:(k % per + 1) * LANES] = us.astype(BF16)
                continue
            if j == 1:
                r = (_head_rmsnorm(r, ones_ref) * qg_ref[...]) * Q_SCALE
            elif j == 2:
                r = _head_rmsnorm(r, ones_ref) * kg_ref[...]
            o_ref[r0:r0 + rows, (j - 1) * tn:j * tn] = r.astype(BF16)


def in_proj(x2, mod3, g_mix, w_in_bf, q_gain, k_gain, bsz, seqlen, tm=512, rows=256):
    t, d = x2.shape
    n = w_in_bf.shape[1]
    tn = n // 4
    assert seqlen % tm == 0 and tm % rows == 0 and rows % (CHUNK * BF16_ROWS) == 0 and tn % LANE_TILE == 0
    reps = tn // HEAD_DIM
    qg = jnp.tile(q_gain.reshape(1, HEAD_DIM), (1, reps))
    kg = jnp.tile(k_gain.reshape(1, HEAD_DIM), (1, reps))
    idx = jnp.arange(LANE_TILE) // HEAD_DIM
    ones = (idx[:, None] == idx[None, :]).astype(BF16)
    const = lambda shape: pl.BlockSpec(shape, lambda i: (0, 0), pipeline_mode=pl.Buffered(1))
    tpb = seqlen // tm
    ngt = tn // LANE_TILE
    return pl.pallas_call(
        functools.partial(_proj_kernel, rows=rows),
        grid=(t // tm,),
        in_specs=[pl.BlockSpec((tm, d), lambda i: (i, 0)),
                  pl.BlockSpec((1, 6, d), lambda i: (i // tpb, 0, 0)),
                  const((1, d)), const((d, n)), const((1, tn)), const((1, tn)),
                  const((LANE_TILE, LANE_TILE))],
        out_specs=[pl.BlockSpec((1, CHUNK, ngt, tm // CHUNK, LANE_TILE), lambda i: (i // tpb, 0, 0, i % tpb, 0)),
                   pl.BlockSpec((tm, 3 * tn), lambda i: (i, 0))],
        out_shape=[jax.ShapeDtypeStruct((bsz, CHUNK, ngt, seqlen // CHUNK, LANE_TILE), BF16),
                   jax.ShapeDtypeStruct((t, 3 * tn), BF16)],
        scratch_shapes=[pltpu.VMEM((tn // LANES, rows, LANES), F32)],
        compiler_params=_cparams(1),
        name="proj",
    )(x2, mod3, g_mix.reshape(1, d), w_in_bf, qg, kg, ones)


def _zoh(ar, ai, dt):
    mag = jnp.exp(ar * dt)
    th = ai * dt
    lbr, lbi = mag * jnp.cos(th), mag * jnp.sin(th)
    den = ar * ar + ai * ai
    nr, ni = lbr - 1.0, lbi
    return lbr, lbi, (nr * ar + ni * ai) / den, (ni * ar - nr * ai) / den


def _s5prep_kernel(ar_ref, ai_ref, ldt_ref, btr_ref, bti_ref, cr_ref, ci_ref, d_ref,
                   bl_ref, cl_ref, kx_ref, a_ref):
    dt = jnp.exp(ldt_ref[...])
    lbr, lbi, kr, ki = _zoh(ar_ref[...], ai_ref[...], dt)
    btr, bti = btr_ref[...], bti_ref[...]
    bbr = kr * btr - ki * bti
    bbi = kr * bti + ki * btr
    c_re, c_im = cr_ref[...], ci_ref[...]
    h = c_re.shape[1]
    eye = (lax.broadcasted_iota(jnp.int32, (1, h, h), 1) ==
           lax.broadcasted_iota(jnp.int32, (1, h, h), 2))
    dn = (((2,), (2,)), ((0,), (0,)))
    pr, pi = jnp.ones_like(lbr), jnp.zeros_like(lbr)
    blrs, blis = [], []
    for j in range(CHUNK):
        blr = pr * bbr - pi * bbi
        bli = pr * bbi + pi * bbr
        bl_ref[j, 0] = blr
        bl_ref[j, 1] = bli
        blrs.append(blr)
        blis.append(bli)
        pr, pi = pr * lbr - pi * lbi, pr * lbi + pi * lbr
        cl_ref[j, 0] = c_re * pr - c_im * pi
        cl_ref[j, 1] = -(c_re * pi + c_im * pr)
    a_ref[0] = pr
    a_ref[1] = pi
    kx = (lax.dot_general(jnp.concatenate(blrs, axis=1), c_re, dn, precision=lax.Precision.HIGHEST,
                          preferred_element_type=F32)
          - lax.dot_general(jnp.concatenate(blis, axis=1), c_im, dn, precision=lax.Precision.HIGHEST,
                            preferred_element_type=F32))
    for j in range(CHUNK):
        kxj = kx[:, j * h:(j + 1) * h, :]
        if j == 0:
            kxj = kxj + jnp.where(eye, d_ref[...], 0.0)
        kx_ref[:, :, j * h:(j + 1) * h] = kxj


def s5_prep(a_re, a_im, log_dt, b_re, b_im, c_re, c_im, d_skip, gb=8):
    g, p = a_re.shape
    h = c_re.shape[1]
    vec = pl.BlockSpec((gb, 1, p), lambda i: (i, 0, 0))
    mat = pl.BlockSpec((gb, h, p), lambda i: (i, 0, 0))
    lag = pl.BlockSpec((CHUNK, 2, gb, h, p), lambda i: (0, 0, i, 0, 0))
    lag_shape = jax.ShapeDtypeStruct((CHUNK, 2, g, h, p), F32)
    return pl.pallas_call(
        _s5prep_kernel,
        grid=(g // gb,),
        in_specs=[vec, vec, pl.BlockSpec((gb, 1, 1), lambda i: (i, 0, 0)),
                  mat, mat, mat, mat, pl.BlockSpec((gb, 1, h), lambda i: (i, 0, 0))],
        out_specs=[lag, lag,
                   pl.BlockSpec((gb, h, CHUNK * h), lambda i: (i, 0, 0)),
                   pl.BlockSpec((2, gb, 1, p), lambda i: (0, i, 0, 0))],
        out_shape=[lag_shape, lag_shape,
                   jax.ShapeDtypeStruct((g, h, CHUNK * h), F32),
                   jax.ShapeDtypeStruct((2, g, 1, p), F32)],
        compiler_params=_cparams(1),
        name="s5prep",
    )(a_re.reshape(g, 1, p), a_im.reshape(g, 1, p), log_dt.reshape(g, 1, 1),
      b_re.transpose(0, 2, 1), b_im.transpose(0, 2, 1), c_re, c_im, d_skip.reshape(g, 1, h))


def _group_mask(shape, row_div, col_div):
    return (lax.broadcasted_iota(jnp.int32, shape, 0) // row_div ==
            lax.broadcasted_iota(jnp.int32, shape, 1) // col_div)


def _s5in_kernel(u_ref, bl_ref, o_ref, w_ref):
    p, wid = bl_ref.shape[3], o_ref.shape[3]
    h = LANE_TILE // (wid // p)
    mask = _group_mask((LANE_TILE, wid), h, p)
    for s in range(CHUNK):
        ex = jnp.concatenate([bl_ref[CHUNK - 1 - s, 0]] * (wid // p), axis=1)
        w_ref[s * LANE_TILE:(s + 1) * LANE_TILE, :] = jnp.where(mask, ex, 0.0).astype(BF16)
    for b in range(u_ref.shape[0]):
        ucat = jnp.concatenate([u_ref[b, s, 0] for s in range(CHUNK)], axis=1)
        o_ref[b, 0] = jnp.dot(ucat, w_ref[...], preferred_element_type=F32)


def s5_chunk_inputs(u_tm, bl2, h):
    bsz, _, ngt, nc, _ = u_tm.shape
    p = bl2.shape[3]
    wid = (LANE_TILE // h) * p
    return pl.pallas_call(
        _s5in_kernel,
        grid=(ngt, 2),
        in_specs=[pl.BlockSpec((bsz, CHUNK, 1, nc, LANE_TILE), lambda gt, ri: (0, 0, gt, 0, 0)),
                  pl.BlockSpec((CHUNK, 1, LANE_TILE, p), lambda gt, ri: (0, ri, gt, 0))],
        out_specs=pl.BlockSpec((bsz, 1, nc, wid), lambda gt, ri: (0, ri, 0, gt)),
        out_shape=jax.ShapeDtypeStruct((bsz, 2, nc, ngt * wid), F32),
        scratch_shapes=[pltpu.VMEM((CHUNK * LANE_TILE, wid), BF16)],
        compiler_params=_cparams(2),
        name="s5in",
    )(u_tm, bl2)


def _chunk_scan(s_ref, a_ref, x_ref, b):
    nc, w = s_ref.shape[2], s_ref.shape[3]
    a1r, a1i = a_ref[0], a_ref[1]

    def cmul(xr, xi, yr, yi):
        return xr * yr - xi * yi, xr * yi + xi * yr

    n = SUBLANES
    row = lax.broadcasted_iota(jnp.int32, (n, w), 0)
    pows = [(a1r, a1i)]
    for _ in range(n - 1):
        pows.append(cmul(pows[-1][0], pows[-1][1], a1r, a1i))
    steps = []
    for k in (1 << i for i in range(n.bit_length() - 1)):
        steps.append((k, jnp.where(row >= k, pows[k - 1][0], 0.0), jnp.where(row >= k, pows[k - 1][1], 0.0)))
    pcr = jnp.zeros((n, w), F32)
    pci = jnp.zeros((n, w), F32)
    for r in range(n):
        pcr = jnp.where(row == r, pows[r][0], pcr)
        pci = jnp.where(row == r, pows[r][1], pci)

    def body(blk, carry):
        cr, ci = carry
        off = pl.multiple_of(blk * n, n)
        xr = s_ref[b, 0, pl.ds(off, n), :]
        xi = s_ref[b, 1, pl.ds(off, n), :]
        for k, kr, ki in steps:
            sr, si = pltpu.roll(xr, k, 0), pltpu.roll(xi, k, 0)
            dr, di = cmul(sr, si, kr, ki)
            xr, xi = xr + dr, xi + di
        dr, di = cmul(jnp.broadcast_to(cr, (n, w)), jnp.broadcast_to(ci, (n, w)), pcr, pci)
        xr, xi = xr + dr, xi + di
        x_ref[b, 0, pl.ds(off, n), :] = jnp.where(row == 0, cr, pltpu.roll(xr, 1, 0))
        x_ref[b, 1, pl.ds(off, n), :] = jnp.where(row == 0, ci, pltpu.roll(xi, 1, 0))
        return xr[n - 1:n, :], xi[n - 1:n, :]

    zero = jnp.zeros((1, w), F32)
    lax.fori_loop(0, nc // n, body, (zero, zero))


def _s5out_kernel(u_ref, kx_ref, cl_ref, s_ref, a_ref, e_ref, o_ref, tr_ref, x_ref):
    bsz, _, _, nc, _ = u_ref.shape
    p, wid = cl_ref.shape[3], x_ref.shape[3]
    h = LANE_TILE // CHUNK
    mask_t = _group_mask((LANE_TILE, LANE_TILE), h, h)
    mask_c = _group_mask((LANE_TILE, wid), h, p)
    nt = (((1,), (1,)), ((), ()))
    for b in range(bsz):
        _chunk_scan(s_ref, a_ref, x_ref, b)
    kx = kx_ref[...].astype(BF16)
    for j in range(CHUNK):
        tj = jnp.where(mask_t, jnp.dot(kx, e_ref[j], preferred_element_type=F32), 0.0)
        tr_ref[(CHUNK - 1 - j) * LANE_TILE:(CHUNK - j) * LANE_TILE, :] = tj.astype(BF16)
    ucat = jnp.concatenate(
        [jnp.concatenate([u_ref[b, s, 0] for s in range(CHUNK)], axis=1) for b in range(bsz)], axis=0)
    xr = x_ref[:, 0].reshape(bsz * nc, wid).astype(BF16)
    xi = x_ref[:, 1].reshape(bsz * nc, wid).astype(BF16)
    for t in range(CHUNK):
        wr = jnp.concatenate([cl_ref[t, 0]] * (wid // p), axis=1)
        wi = jnp.concatenate([cl_ref[t, 1]] * (wid // p), axis=1)
        wr = jnp.where(mask_c, wr, 0.0).astype(BF16)
        wi = jnp.where(mask_c, wi, 0.0).astype(BF16)
        y = (jnp.dot(ucat[:, :(t + 1) * LANE_TILE], tr_ref[(CHUNK - 1 - t) * LANE_TILE:, :],
                     preferred_element_type=F32)
             + lax.dot_general(xr, wr, nt, preferred_element_type=F32)
             + lax.dot_general(xi, wi, nt, preferred_element_type=F32))
        for b in range(bsz):
            o_ref[b, t, 0] = y[b * nc:(b + 1) * nc].astype(BF16)


def s5_outputs(u_tm, kx2, cl2, s, a):
    bsz, _, ngt, nc, _ = u_tm.shape
    wid = s.shape[3] // ngt
    p = cl2.shape[3]
    once = pl.Buffered(1)
    h = LANE_TILE // CHUNK
    r = jnp.arange(LANE_TILE)
    e = ((r[None, :, None] // h == jnp.arange(CHUNK)[:, None, None])
         & (r[None, :, None] % h == r[None, None, :] % h)).astype(BF16)
    tile = pl.BlockSpec((bsz, CHUNK, 1, nc, LANE_TILE), lambda gt: (0, 0, gt, 0, 0))
    return pl.pallas_call(
        _s5out_kernel,
        grid=(ngt,),
        in_specs=[tile,
                  pl.BlockSpec((LANE_TILE, LANE_TILE), lambda gt: (gt, 0)),
                  pl.BlockSpec((CHUNK, 2, LANE_TILE, p), lambda gt: (0, 0, gt, 0)),
                  pl.BlockSpec((bsz, 2, nc, wid), lambda gt: (0, 0, 0, gt)),
                  pl.BlockSpec((2, 1, wid), lambda gt: (0, 0, gt)),
                  pl.BlockSpec((CHUNK, LANE_TILE, LANE_TILE), lambda gt: (0, 0, 0), pipeline_mode=once)],
        out_specs=tile,
        out_shape=jax.ShapeDtypeStruct(u_tm.shape, BF16),
        scratch_shapes=[pltpu.VMEM((CHUNK * LANE_TILE, LANE_TILE), BF16),
                        pltpu.VMEM((bsz, 2, nc, wid), F32)],
        compiler_params=_cparams(1),
        name="s5out",
    )(u_tm, kx2, cl2, s, a, e)


def _attn_kernel(q_ref, k_ref, v_ref, tri_ref, *rest, blk, nh, unroll, ncast):
    src_refs, o_ref, dst_refs, vcat_ref = rest[:ncast], rest[ncast], rest[ncast + 1:2 * ncast + 1], rest[-1]
    for src, dst in zip(src_refs, dst_refs):
        dst[...] = src[...].astype(BF16)
    qi = pl.program_id(2)
    q = q_ref[0]
    lane = lax.broadcasted_iota(jnp.int32, q.shape, 1)
    heads = [(lane >= HEAD_DIM * h) & (lane < HEAD_DIM * (h + 1)) for h in range(nh)]
    zero = jnp.zeros_like(q)

    @pl.when(qi == 0)
    def _():
        def fill(j, _):
            vb = v_ref[0, pl.ds(pl.multiple_of(j * blk, blk), blk), :]
            vcat_ref[j] = jnp.concatenate([jnp.where(m, vb, zero) for m in heads], axis=0)
            return 0

        lax.fori_loop(0, vcat_ref.shape[0], fill, 0)

    qs = jnp.concatenate([jnp.where(m, q, zero) for m in heads], axis=0)
    row = lax.broadcasted_iota(jnp.int32, (blk, blk), 0)
    col = lax.broadcasted_iota(jnp.int32, (blk, blk), 1)
    past = jnp.concatenate([col < row] * nh, axis=0)
    nt = (((1,), (1,)), ((), ()))

    def block(kj, carry, acc, diag):
        kb = k_ref[0, pl.ds(pl.multiple_of(kj * blk, blk), blk), :]
        z = lax.dot_general(qs, kb, nt, preferred_element_type=F32)
        sp = jnp.maximum(z, 0.0) + jnp.log2(1.0 + jnp.exp2(-jnp.abs(z)))
        if diag:
            sp = jnp.where(past, sp, 0.0)
        cs = lax.dot_general(sp, tri_ref[...], (((1,), (0,)), ((), ())), preferred_element_type=F32)
        w = jnp.exp2(((z - sp) - cs) - carry)
        if diag:
            w = jnp.where(past, w, 0.0)
        wcat = jnp.concatenate([w[h * blk:(h + 1) * blk] for h in range(nh)], axis=1)
        acc = acc + lax.dot_general(wcat, vcat_ref[kj], (((1,), (0,)), ((), ())), preferred_element_type=F32)
        return carry + (cs[:, 0:1] + sp[:, 0:1]), acc

    def head(r):
        def run():
            c = block(qi, jnp.zeros((nh * blk, 1), F32), jnp.zeros(q.shape, F32), True)
            for i in range(r):
                c = block(qi - 1 - i, c[0], c[1], False)
            return c
        return run

    rem = qi % unroll
    c = lax.switch(rem, [head(r) for r in range(unroll)])

    left = qi - rem
    for n in (2 * unroll, unroll):
        def group(i, c, n=n, left=left):
            for r in range(n):
                c = block(left - 1 - r - n * i, c[0], c[1], False)
            return c

        c = lax.fori_loop(0, left // n, group, c)
        left = left % n
    o_ref[0] = c[1].astype(BF16)


def stick_attention(qkv3, width, cast=(), blk=256, nh=4, unroll=4):
    bsz, seqlen, _ = qkv3.shape
    wl = nh * HEAD_DIM
    assert wl == LANE_TILE and width % wl == 0 and seqlen % blk == 0
    ngrp = width // wl
    nq = seqlen // blk
    ii = jnp.arange(blk)
    tri = (ii[:, None] > ii[None, :]).astype(BF16)
    nsteps = bsz * ngrp * nq

    def slab_spec(w):
        rows, cols = w.shape
        for c in range(1, cols // LANES + 1):
            r = nsteps // c
            if cols % (LANES * c) == 0 and nsteps % c == 0 and rows % (BF16_ROWS * r) == 0:
                return pl.BlockSpec((rows // r, cols // c),
                                    lambda b, h, i, c=c: (((b * ngrp + h) * nq + i) // c, ((b * ngrp + h) * nq + i) % c))
        raise ValueError(f"cannot cut {w.shape} into {nsteps} blocks")

    slab_specs = [slab_spec(w) for w in cast]
    outs = pl.pallas_call(
        functools.partial(_attn_kernel, blk=blk, nh=nh, unroll=unroll, ncast=len(cast)),
        grid=(bsz, ngrp, nq),
        in_specs=[pl.BlockSpec((1, blk, wl), lambda b, h, i: (b, i, h)),
                  pl.BlockSpec((1, seqlen, wl), lambda b, h, i: (b, 0, ngrp + h)),
                  pl.BlockSpec((1, seqlen, wl), lambda b, h, i: (b, 0, 2 * ngrp + h)),
                  pl.BlockSpec((blk, blk), lambda b, h, i: (0, 0))] + slab_specs,
        out_specs=[pl.BlockSpec((1, blk, wl), lambda b, h, i: (b, i, h))] + slab_specs,
        out_shape=[jax.ShapeDtypeStruct((bsz, seqlen, width), BF16)]
                  + [jax.ShapeDtypeStruct(w.shape, BF16) for w in cast],
        scratch_shapes=[pltpu.VMEM((nq, nh * blk, wl), BF16)],
        compiler_params=_cparams(3),
        name="attn",
    )(qkv3, qkv3, qkv3, tri, *cast)
    return outs[0], list(outs[1:])


def _out_kernel(y_ref, a_ref, x_ref, mod_ref, wglu_ref, gs_ref, ga_ref, w0_ref, w1_ref, gf_ref,
                h_ref, xf_ref, ys_ref, *, rows):
    cpr = rows // CHUNK
    per = LANE_TILE // LANES
    for r0 in range(0, a_ref.shape[0], rows):
        rs = slice(r0, r0 + rows)
        c0 = r0 // CHUNK
        nk = ys_ref.shape[0]
        for s in range(CHUNK):
            for k in range(nk):
                ys_ref[k, pl.ds(s, cpr, stride=CHUNK), :] = (
                    y_ref[0, s, k // per, c0:c0 + cpr, (k % per) * LANES:(k % per + 1) * LANES].astype(F32))
        y = jax.nn.gelu(jnp.concatenate([ys_ref[k] for k in range(nk)], axis=1))
        z = jnp.dot(y.astype(BF16), wglu_ref[...], preferred_element_type=F32)
        ys = y * jax.nn.sigmoid(z)
        ysn = (_rms(ys) * gs_ref[...]).astype(BF16)
        an = (_rms(a_ref[rs, :].astype(F32)) * ga_ref[...]).astype(BF16)
        o = (jnp.dot(ysn, w0_ref[...], preferred_element_type=F32)
             + jnp.dot(an, w1_ref[...], preferred_element_type=F32))
        h = x_ref[rs, :] + mod_ref[0, 2:3, :] * o
        h_ref[rs, :] = h
        xn = _rms(h) * gf_ref[...]
        xf_ref[rs, :] = (xn * (1.0 + mod_ref[0, 4:5, :]) + mod_ref[0, 3:4, :]).astype(BF16)


def out_proj(y_tm, a2, x2, mod3, w_glu_bf, g_ssm, g_attn, w_out_bf, g_ffn, seqlen, tm=512, rows=256):
    t, d = x2.shape
    ngt = y_tm.shape[2]
    ws = ngt * LANE_TILE
    wa = a2.shape[1]
    assert seqlen % tm == 0 and tm % rows == 0 and rows % (CHUNK * BF16_ROWS) == 0 and ws == wa
    tpb = seqlen // tm
    const = lambda shape, r=0: pl.BlockSpec(shape, lambda i: (r, 0), pipeline_mode=pl.Buffered(1))
    return pl.pallas_call(
        functools.partial(_out_kernel, rows=rows),
        grid=(t // tm,),
        in_specs=[pl.BlockSpec((1, CHUNK, ngt, tm // CHUNK, LANE_TILE), lambda i: (i // tpb, 0, 0, i % tpb, 0)),
                  pl.BlockSpec((tm, wa), lambda i: (i, 0)),
                  pl.BlockSpec((tm, d), lambda i: (i, 0)),
                  pl.BlockSpec((1, 6, d), lambda i: (i // tpb, 0, 0)),
                  const((ws, ws)), const((1, ws)), const((1, wa)),
                  const((ws, d)), const((wa, d), 1), const((1, d))],
        out_specs=[pl.BlockSpec((tm, d), lambda i: (i, 0)),
                   pl.BlockSpec((tm, d), lambda i: (i, 0))],
        out_shape=[jax.ShapeDtypeStruct((t, d), F32), jax.ShapeDtypeStruct((t, d), BF16)],
        scratch_shapes=[pltpu.VMEM((ws // LANES, rows, LANES), F32)],
        compiler_params=_cparams(1),
        name="outproj",
    )(y_tm, a2, x2, mod3, w_glu_bf, g_ssm.reshape(1, ws), g_attn.reshape(1, wa), w_out_bf, w_out_bf,
      g_ffn.reshape(1, d))


def _ffn_kernel(xf_ref, h_ref, mod_ref, wg_ref, wu_ref, wd_ref, o_ref):
    j = pl.program_id(1)

    @pl.when(j == 0)
    def _():
        o_ref[...] = jnp.zeros_like(o_ref)

    xf = xf_ref[...]
    g = jnp.dot(xf, wg_ref[...], preferred_element_type=F32)
    u = jnp.dot(xf, wu_ref[...], preferred_element_type=F32)
    hm = (jax.nn.silu(g) * u).astype(BF16)
    o_ref[...] += jnp.dot(hm, wd_ref[...], preferred_element_type=F32)

    @pl.when(j == pl.num_programs(1) - 1)
    def _():
        o_ref[...] = h_ref[...] + mod_ref[0, 5:6, :] * o_ref[...]


def ffn(xf2, h2, mod3, w_gate, w_up, w_down, seqlen, tm=1024, tf=512):
    t, d = h2.shape
    dff = w_gate.shape[1]
    assert seqlen % tm == 0 and dff % tf == 0
    row = pl.BlockSpec((tm, d), lambda i, j: (i, 0), pipeline_mode=pl.Buffered(1))
    return pl.pallas_call(
        _ffn_kernel,
        grid=(t // tm, dff // tf),
        in_specs=[row, row,
                  pl.BlockSpec((1, 6, d), lambda i, j: ((i * tm) // seqlen, 0, 0)),
                  pl.BlockSpec((d, tf), lambda i, j: (0, j)),
                  pl.BlockSpec((d, tf), lambda i, j: (0, j)),
                  pl.BlockSpec((tf, d), lambda i, j: (j, 0))],
        out_specs=pl.BlockSpec((tm, d), lambda i, j: (i, 0)),
        out_shape=jax.ShapeDtypeStruct((t, d), F32),
        compiler_params=_cparams(2),
        name="ffn",
    )(xf2, h2, mod3, w_gate, w_up, w_down)


def s5_mixer_chunked(u_tm, a_re, a_im, log_dt, b_re, b_im, c_re, c_im, d_skip):
    bsz, _, ngt, nc, _ = u_tm.shape
    g, p = a_re.shape
    h = c_re.shape[1]
    assert h * CHUNK == LANE_TILE and g * h == ngt * LANE_TILE
    bl, cl, kx, a = s5_prep(a_re, a_im, log_dt, b_re, b_im, c_re, c_im, d_skip)
    s = s5_chunk_inputs(u_tm, bl.reshape(CHUNK, 2, g * h, p), h)
    return s5_outputs(u_tm, kx.reshape(g * h, CHUNK * h), cl.reshape(CHUNK, 2, g * h, p), s,
                      a.reshape(2, 1, g * p))


def kernel(x, c, w_ada, b_ada, g_mix, w_in, a_re, a_im, log_dt, b_re, b_im, c_re, c_im, d_skip,
           w_glu, q_gain, k_gain, g_ssm_out, g_attn_out, w_out, g_ffn, w_gate, w_up, w_down):
    bsz, seqlen, d = x.shape
    depth = w_ada.shape[0]
    width = w_glu.shape[1]
    h2 = x.reshape(bsz * seqlen, d)
    for l in range(depth):
        mod3 = ada_mod(c, w_ada[l], b_ada[l]).reshape(bsz, 6, d)
        u_tm, qkv = in_proj(h2, mod3, g_mix[l], w_in[l].astype(BF16), q_gain[l], k_gain[l], bsz, seqlen)
        y_tm = s5_mixer_chunked(u_tm, a_re[l], a_im[l], log_dt[l], b_re[l], b_im[l],
                                c_re[l], c_im[l], d_skip[l])
        att, (wglu_bf, wout_bf, wg_bf, wu_bf, wd_bf) = stick_attention(
            qkv.reshape(bsz, seqlen, 3 * width), width,
            cast=(w_glu[l], w_out[l], w_gate[l], w_up[l], w_down[l]))
        h2, xf = out_proj(y_tm, att.reshape(bsz * seqlen, width), h2, mod3,
                          wglu_bf, g_ssm_out[l], g_attn_out[l], wout_bf, g_ffn[l], seqlen)
        h2 = ffn(xf, h2, mod3, wg_bf, wu_bf, wd_bf, seqlen)
    return h2.reshape(bsz, seqlen, d)
```

```python
import functools

import jax
import jax.numpy as jnp
from jax import lax
from jax.experimental import pallas as pl
from jax.experimental.pallas import tpu as pltpu

F32 = jnp.float32
BF16 = jnp.bfloat16
EPS = 1e-6
HEAD_DIM = 64
CHUNK = 16
LANES = 128
SUBLANES = 8
BF16_ROWS = 16
LANE_TILE = 256
VMEM_LIMIT = 56 * 1024 * 1024
Q_SCALE = HEAD_DIM ** -0.5 * 1.4426950408889634


def _cparams(n_axes, vmem=VMEM_LIMIT):
    return pltpu.CompilerParams(dimension_semantics=("arbitrary",) * n_axes,
                                vmem_limit_bytes=vmem)


def _rms(x):
    return x * lax.rsqrt(jnp.mean(x * x, axis=-1, keepdims=True) + EPS)


def _ada_kernel(c_ref, w_ref, b_ref, o_ref):
    d, tn = w_ref.shape
    rows = LANE_TILE
    for b in range(c_ref.shape[0]):
        acc = jnp.zeros((SUBLANES, tn), F32)
        for kc in range(d // rows):
            cond = jax.nn.silu(c_ref[b, kc * rows:(kc + 1) * rows, :])
            blk = cond * w_ref[kc * rows:(kc + 1) * rows, :]
            acc = acc + jnp.sum(blk.reshape(rows // SUBLANES, SUBLANES, tn), axis=0)
        o_ref[b:b + 1, :] = jnp.sum(acc, axis=0, keepdims=True) + b_ref[...]


def ada_mod(c3, w_ada, b_ada2, ncols, tn=1024):
    bsz, d, _ = c3.shape
    assert ncols % tn == 0
    return pl.pallas_call(
        _ada_kernel,
        grid=(ncols // tn,),
        in_specs=[pl.BlockSpec((bsz, d, 1), lambda j: (0, 0, 0)),
                  pl.BlockSpec((d, tn), lambda j: (0, j)),
                  pl.BlockSpec((1, tn), lambda j: (0, j))],
        out_specs=pl.BlockSpec((bsz, tn), lambda j: (0, j)),
        out_shape=jax.ShapeDtypeStruct((bsz, ncols), F32),
        compiler_params=_cparams(1),
        name="ada",
    )(c3, w_ada, b_ada2)


def _head_rmsnorm(r, ones_ref):
    outs = []
    for s in range(r.shape[1] // LANE_TILE):
        rs = r[:, s * LANE_TILE:(s + 1) * LANE_TILE]
        ss = jnp.dot((rs * rs).astype(BF16), ones_ref[...], preferred_element_type=F32)
        outs.append(rs * lax.rsqrt(ss * (1.0 / HEAD_DIM) + EPS))
    return jnp.concatenate(outs, axis=1)


def _proj_kernel(x_ref, mod_ref, g_ref, w_ref, qg_ref, kg_ref, ones_ref, c_ref, wa_ref, ba_ref,
                 u_ref, o_ref, mo_ref, us_ref, *, rows):
    _ada_kernel(c_ref, wa_ref, ba_ref, mo_ref)
    tm = x_ref.shape[0]
    tn = w_ref.shape[1] // 4
    cpr = rows // CHUNK
    for r0 in range(0, tm, rows):
        xn = _rms(x_ref[r0:r0 + rows, :]) * g_ref[...]
        xm = (xn * (1.0 + mod_ref[0, 1:2, :]) + mod_ref[0, 0:1, :]).astype(BF16)
        for j in range(4):
            r = jnp.dot(xm, w_ref[:, j * tn:(j + 1) * tn], preferred_element_type=F32)
            if j == 0:
                c0 = r0 // CHUNK
                per = LANE_TILE // LANES
                for k in range(tn // LANES):
                    us_ref[k] = r[:, k * LANES:(k + 1) * LANES]
                for s in range(CHUNK):
                    for k in range(tn // LANES):
                        us = us_ref[k, pl.ds(s, cpr, stride=CHUNK), :]
                        u_ref[0, s, k // per, c0:c0 + cpr, (k % per) * LANES:(k % per + 1) * LANES] = us.astype(BF16)
                continue
            if j == 1:
                r = (_head_rmsnorm(r, ones_ref) * qg_ref[...]) * Q_SCALE
            elif j == 2:
                r = _head_rmsnorm(r, ones_ref) * kg_ref[...]
            o_ref[r0:r0 + rows, (j - 1) * tn:j * tn] = r.astype(BF16)


def in_proj(x2, mod_a3, g_mix, w_in_bf, q_gain, k_gain, c3, w_ada, b_ada2, bsz, seqlen, tm=512, rows=256):
    t, d = x2.shape
    n = w_in_bf.shape[1]
    tn = n // 4
    nsteps = t // tm
    done = mod_a3.shape[1] * d
    ta = (w_ada.shape[1] - done) // nsteps
    assert seqlen % tm == 0 and tm % rows == 0 and rows % (CHUNK * BF16_ROWS) == 0 and tn % LANE_TILE == 0
    assert ta % LANES == 0 and ta * nsteps == w_ada.shape[1] - done and done % ta == 0
    reps = tn // HEAD_DIM
    qg = jnp.tile(q_gain.reshape(1, HEAD_DIM), (1, reps))
    kg = jnp.tile(k_gain.reshape(1, HEAD_DIM), (1, reps))
    idx = jnp.arange(LANE_TILE) // HEAD_DIM
    ones = (idx[:, None] == idx[None, :]).astype(BF16)
    const = lambda shape: pl.BlockSpec(shape, lambda i: (0, 0), pipeline_mode=pl.Buffered(1))
    tpb = seqlen // tm
    ngt = tn // LANE_TILE
    return pl.pallas_call(
        functools.partial(_proj_kernel, rows=rows),
        grid=(nsteps,),
        in_specs=[pl.BlockSpec((tm, d), lambda i: (i, 0)),
                  pl.BlockSpec((1,) + mod_a3.shape[1:], lambda i: (i // tpb, 0, 0)),
                  const((1, d)), const((d, n)), const((1, tn)), const((1, tn)),
                  const((LANE_TILE, LANE_TILE)),
                  pl.BlockSpec(c3.shape, lambda i: (0, 0, 0), pipeline_mode=pl.Buffered(1)),
                  pl.BlockSpec((d, ta), lambda i: (0, done // ta + i)),
                  pl.BlockSpec((1, ta), lambda i: (0, done // ta + i))],
        out_specs=[pl.BlockSpec((1, CHUNK, ngt, tm // CHUNK, LANE_TILE), lambda i: (i // tpb, 0, 0, i % tpb, 0)),
                   pl.BlockSpec((tm, 3 * tn), lambda i: (i, 0)),
                   pl.BlockSpec((bsz, ta), lambda i: (0, i))],
        out_shape=[jax.ShapeDtypeStruct((bsz, CHUNK, ngt, seqlen // CHUNK, LANE_TILE), BF16),
                   jax.ShapeDtypeStruct((t, 3 * tn), BF16),
                   jax.ShapeDtypeStruct((bsz, w_ada.shape[1] - done), F32)],
        scratch_shapes=[pltpu.VMEM((tn // LANES, rows, LANES), F32)],
        compiler_params=_cparams(1),
        name="proj",
    )(x2, mod_a3, g_mix.reshape(1, d), w_in_bf, qg, kg, ones, c3, w_ada, b_ada2)


def _zoh(ar, ai, dt):
    mag = jnp.exp(ar * dt)
    th = ai * dt
    lbr, lbi = mag * jnp.cos(th), mag * jnp.sin(th)
    den = ar * ar + ai * ai
    nr, ni = lbr - 1.0, lbi
    return lbr, lbi, (nr * ar + ni * ai) / den, (ni * ar - nr * ai) / den


def _s5prep_kernel(ar_ref, ai_ref, ldt_ref, btr_ref, bti_ref, cr_ref, ci_ref, d_ref,
                   bl_ref, cl_ref, kx_ref, a_ref):
    dt = jnp.exp(ldt_ref[...])
    lbr, lbi, kr, ki = _zoh(ar_ref[...], ai_ref[...], dt)
    btr, bti = btr_ref[...], bti_ref[...]
    bbr = kr * btr - ki * bti
    bbi = kr * bti + ki * btr
    c_re, c_im = cr_ref[...], ci_ref[...]
    h = c_re.shape[1]
    eye = (lax.broadcasted_iota(jnp.int32, (1, h, h), 1) ==
           lax.broadcasted_iota(jnp.int32, (1, h, h), 2))
    dn = (((2,), (2,)), ((0,), (0,)))
    pr, pi = jnp.ones_like(lbr), jnp.zeros_like(lbr)
    blrs, blis = [], []
    for j in range(CHUNK):
        blr = pr * bbr - pi * bbi
        bli = pr * bbi + pi * bbr
        bl_ref[j, 0] = blr
        bl_ref[j, 1] = bli
        blrs.append(blr)
        blis.append(bli)
        pr, pi = pr * lbr - pi * lbi, pr * lbi + pi * lbr
        cl_ref[j, 0] = c_re * pr - c_im * pi
        cl_ref[j, 1] = -(c_re * pi + c_im * pr)
    a_ref[0] = pr
    a_ref[1] = pi
    kx = (lax.dot_general(jnp.concatenate(blrs, axis=1), c_re, dn, precision=lax.Precision.HIGHEST,
                          preferred_element_type=F32)
          - lax.dot_general(jnp.concatenate(blis, axis=1), c_im, dn, precision=lax.Precision.HIGHEST,
                            preferred_element_type=F32))
    for j in range(CHUNK):
        kxj = kx[:, j * h:(j + 1) * h, :]
        if j == 0:
            kxj = kxj + jnp.where(eye, d_ref[...], 0.0)
        kx_ref[:, :, j * h:(j + 1) * h] = kxj


def s5_prep(a_re, a_im, log_dt, b_re, b_im, c_re, c_im, d_skip, gb=8):
    g, p = a_re.shape
    h = c_re.shape[1]
    vec = pl.BlockSpec((gb, 1, p), lambda i: (i, 0, 0))
    mat = pl.BlockSpec((gb, h, p), lambda i: (i, 0, 0))
    lag = pl.BlockSpec((CHUNK, 2, gb, h, p), lambda i: (0, 0, i, 0, 0))
    lag_shape = jax.ShapeDtypeStruct((CHUNK, 2, g, h, p), F32)
    return pl.pallas_call(
        _s5prep_kernel,
        grid=(g // gb,),
        in_specs=[vec, vec, pl.BlockSpec((gb, 1, 1), lambda i: (i, 0, 0)),
                  mat, mat, mat, mat, pl.BlockSpec((gb, 1, h), lambda i: (i, 0, 0))],
        out_specs=[lag, lag,
                   pl.BlockSpec((gb, h, CHUNK * h), lambda i: (i, 0, 0)),
                   pl.BlockSpec((2, gb, 1, p), lambda i: (0, i, 0, 0))],
        out_shape=[lag_shape, lag_shape,
                   jax.ShapeDtypeStruct((g, h, CHUNK * h), F32),
                   jax.ShapeDtypeStruct((2, g, 1, p), F32)],
        compiler_params=_cparams(1),
        name="s5prep",
    )(a_re.reshape(g, 1, p), a_im.reshape(g, 1, p), log_dt.reshape(g, 1, 1),
      b_re.transpose(0, 2, 1), b_im.transpose(0, 2, 1), c_re, c_im, d_skip.reshape(g, 1, h))


def _group_mask(shape, row_div, col_div):
    return (lax.broadcasted_iota(jnp.int32, shape, 0) // row_div ==
            lax.broadcasted_iota(jnp.int32, shape, 1) // col_div)


def _s5in_kernel(u_ref, bl_ref, o_ref, w_ref):
    p, wid = bl_ref.shape[3], o_ref.shape[3]
    h = LANE_TILE // (wid // p)
    mask = _group_mask((LANE_TILE, wid), h, p)
    for s in range(CHUNK):
        ex = jnp.concatenate([bl_ref[CHUNK - 1 - s, 0]] * (wid // p), axis=1)
        w_ref[s * LANE_TILE:(s + 1) * LANE_TILE, :] = jnp.where(mask, ex, 0.0).astype(BF16)
    for b in range(u_ref.shape[0]):
        ucat = jnp.concatenate([u_ref[b, s, 0] for s in range(CHUNK)], axis=1)
        o_ref[b, 0] = jnp.dot(ucat, w_ref[...], preferred_element_type=F32)


def s5_chunk_inputs(u_tm, bl2, h):
    bsz, _, ngt, nc, _ = u_tm.shape
    p = bl2.shape[3]
    wid = (LANE_TILE // h) * p
    return pl.pallas_call(
        _s5in_kernel,
        grid=(ngt, 2),
        in_specs=[pl.BlockSpec((bsz, CHUNK, 1, nc, LANE_TILE), lambda gt, ri: (0, 0, gt, 0, 0)),
                  pl.BlockSpec((CHUNK, 1, LANE_TILE, p), lambda gt, ri: (0, ri, gt, 0))],
        out_specs=pl.BlockSpec((bsz, 1, nc, wid), lambda gt, ri: (0, ri, 0, gt)),
        out_shape=jax.ShapeDtypeStruct((bsz, 2, nc, ngt * wid), F32),
        scratch_shapes=[pltpu.VMEM((CHUNK * LANE_TILE, wid), BF16)],
        compiler_params=_cparams(2),
        name="s5in",
    )(u_tm, bl2)


def _chunk_scan(s_ref, a_ref, x_ref, b):
    nc, w = s_ref.shape[2], s_ref.shape[3]
    a1r, a1i = a_ref[0], a_ref[1]

    def cmul(xr, xi, yr, yi):
        return xr * yr - xi * yi, xr * yi + xi * yr

    n = SUBLANES
    row = lax.broadcasted_iota(jnp.int32, (n, w), 0)
    pows = [(a1r, a1i)]
    for _ in range(n - 1):
        pows.append(cmul(pows[-1][0], pows[-1][1], a1r, a1i))
    steps = []
    for k in (1 << i for i in range(n.bit_length() - 1)):
        steps.append((k, jnp.where(row >= k, pows[k - 1][0], 0.0), jnp.where(row >= k, pows[k - 1][1], 0.0)))
    pcr = jnp.zeros((n, w), F32)
    pci = jnp.zeros((n, w), F32)
    for r in range(n):
        pcr = jnp.where(row == r, pows[r][0], pcr)
        pci = jnp.where(row == r, pows[r][1], pci)

    def body(blk, carry):
        cr, ci = carry
        off = pl.multiple_of(blk * n, n)
        xr = s_ref[b, 0, pl.ds(off, n), :]
        xi = s_ref[b, 1, pl.ds(off, n), :]
        for k, kr, ki in steps:
            sr, si = pltpu.roll(xr, k, 0), pltpu.roll(xi, k, 0)
            dr, di = cmul(sr, si, kr, ki)
            xr, xi = xr + dr, xi + di
        dr, di = cmul(jnp.broadcast_to(cr, (n, w)), jnp.broadcast_to(ci, (n, w)), pcr, pci)
        xr, xi = xr + dr, xi + di
        x_ref[b, 0, pl.ds(off, n), :] = jnp.where(row == 0, cr, pltpu.roll(xr, 1, 0))
        x_ref[b, 1, pl.ds(off, n), :] = jnp.where(row == 0, ci, pltpu.roll(xi, 1, 0))
        return xr[n - 1:n, :], xi[n - 1:n, :]

    zero = jnp.zeros((1, w), F32)
    lax.fori_loop(0, nc // n, body, (zero, zero))


def _s5out_kernel(u_ref, kx_ref, cl_ref, s_ref, a_ref, e_ref, o_ref, tr_ref, x_ref):
    bsz, _, _, nc, _ = u_ref.shape
    p, wid = cl_ref.shape[3], x_ref.shape[3]
    h = LANE_TILE // CHUNK
    mask_t = _group_mask((LANE_TILE, LANE_TILE), h, h)
    mask_c = _group_mask((LANE_TILE, wid), h, p)
    nt = (((1,), (1,)), ((), ()))
    for b in range(bsz):
        _chunk_scan(s_ref, a_ref, x_ref, b)
    kx = kx_ref[...].astype(BF16)
    for j in range(CHUNK):
        tj = jnp.where(mask_t, jnp.dot(kx, e_ref[j], preferred_element_type=F32), 0.0)
        tr_ref[(CHUNK - 1 - j) * LANE_TILE:(CHUNK - j) * LANE_TILE, :] = tj.astype(BF16)
    ucat = jnp.concatenate(
        [jnp.concatenate([u_ref[b, s, 0] for s in range(CHUNK)], axis=1) for b in range(bsz)], axis=0)
    xr = x_ref[:, 0].reshape(bsz * nc, wid).astype(BF16)
    xi = x_ref[:, 1].reshape(bsz * nc, wid).astype(BF16)
    for t in range(CHUNK):
        wr = jnp.concatenate([cl_ref[t, 0]] * (wid // p), axis=1)
        wi = jnp.concatenate([cl_ref[t, 1]] * (wid // p), axis=1)
        wr = jnp.where(mask_c, wr, 0.0).astype(BF16)
        wi = jnp.where(mask_c, wi, 0.0).astype(BF16)
        y = (jnp.dot(ucat[:, :(t + 1) * LANE_TILE], tr_ref[(CHUNK - 1 - t) * LANE_TILE:, :],
                     preferred_element_type=F32)
             + lax.dot_general(xr, wr, nt, preferred_element_type=F32)
             + lax.dot_general(xi, wi, nt, preferred_element_type=F32))
        for b in range(bsz):
            o_ref[b, t, 0] = y[b * nc:(b + 1) * nc].astype(BF16)


def s5_outputs(u_tm, kx2, cl2, s, a):
    bsz, _, ngt, nc, _ = u_tm.shape
    wid = s.shape[3] // ngt
    p = cl2.shape[3]
    once = pl.Buffered(1)
    h = LANE_TILE // CHUNK
    r = jnp.arange(LANE_TILE)
    e = ((r[None, :, None] // h == jnp.arange(CHUNK)[:, None, None])
         & (r[None, :, None] % h == r[None, None, :] % h)).astype(BF16)
    tile = pl.BlockSpec((bsz, CHUNK, 1, nc, LANE_TILE), lambda gt: (0, 0, gt, 0, 0))
    return pl.pallas_call(
        _s5out_kernel,
        grid=(ngt,),
        in_specs=[tile,
                  pl.BlockSpec((LANE_TILE, LANE_TILE), lambda gt: (gt, 0)),
                  pl.BlockSpec((CHUNK, 2, LANE_TILE, p), lambda gt: (0, 0, gt, 0)),
                  pl.BlockSpec((bsz, 2, nc, wid), lambda gt: (0, 0, 0, gt)),
                  pl.BlockSpec((2, 1, wid), lambda gt: (0, 0, gt)),
                  pl.BlockSpec((CHUNK, LANE_TILE, LANE_TILE), lambda gt: (0, 0, 0), pipeline_mode=once)],
        out_specs=tile,
        out_shape=jax.ShapeDtypeStruct(u_tm.shape, BF16),
        scratch_shapes=[pltpu.VMEM((CHUNK * LANE_TILE, LANE_TILE), BF16),
                        pltpu.VMEM((bsz, 2, nc, wid), F32)],
        compiler_params=_cparams(1),
        name="s5out",
    )(u_tm, kx2, cl2, s, a, e)


def _attn_kernel(q_ref, k_ref, v_ref, tri_ref, *rest, blk, nh, unroll, ncast):
    src_refs, o_ref, dst_refs, vcat_ref = rest[:ncast], rest[ncast], rest[ncast + 1:2 * ncast + 1], rest[-1]
    for src, dst in zip(src_refs, dst_refs):
        dst[...] = src[...].astype(BF16)
    qi = pl.program_id(2)
    q = q_ref[0]
    lane = lax.broadcasted_iota(jnp.int32, q.shape, 1)
    heads = [(lane >= HEAD_DIM * h) & (lane < HEAD_DIM * (h + 1)) for h in range(nh)]
    zero = jnp.zeros_like(q)

    @pl.when(qi == 0)
    def _():
        def fill(j, _):
            vb = v_ref[0, pl.ds(pl.multiple_of(j * blk, blk), blk), :]
            vcat_ref[j] = jnp.concatenate([jnp.where(m, vb, zero) for m in heads], axis=0)
            return 0

        lax.fori_loop(0, vcat_ref.shape[0], fill, 0)

    qs = jnp.concatenate([jnp.where(m, q, zero) for m in heads], axis=0)
    row = lax.broadcasted_iota(jnp.int32, (blk, blk), 0)
    col = lax.broadcasted_iota(jnp.int32, (blk, blk), 1)
    past = jnp.concatenate([col < row] * nh, axis=0)
    nt = (((1,), (1,)), ((), ()))

    def block(kj, carry, acc, diag):
        kb = k_ref[0, pl.ds(pl.multiple_of(kj * blk, blk), blk), :]
        z = lax.dot_general(qs, kb, nt, preferred_element_type=F32)
        sp = jnp.maximum(z, 0.0) + jnp.log2(1.0 + jnp.exp2(-jnp.abs(z)))
        if diag:
            sp = jnp.where(past, sp, 0.0)
        cs = lax.dot_general(sp, tri_ref[...], (((1,), (0,)), ((), ())), preferred_element_type=F32)
        w = jnp.exp2(((z - sp) - cs) - carry)
        if diag:
            w = jnp.where(past, w, 0.0)
        wcat = jnp.concatenate([w[h * blk:(h + 1) * blk] for h in range(nh)], axis=1)
        acc = acc + lax.dot_general(wcat, vcat_ref[kj], (((1,), (0,)), ((), ())), preferred_element_type=F32)
        return carry + (cs[:, 0:1] + sp[:, 0:1]), acc

    def head(r):
        def run():
            c = block(qi, jnp.zeros((nh * blk, 1), F32), jnp.zeros(q.shape, F32), True)
            for i in range(r):
                c = block(qi - 1 - i, c[0], c[1], False)
            return c
        return run

    rem = qi % unroll
    c = lax.switch(rem, [head(r) for r in range(unroll)])

    left = qi - rem
    for n in (2 * unroll, unroll):
        def group(i, c, n=n, left=left):
            for r in range(n):
                c = block(left - 1 - r - n * i, c[0], c[1], False)
            return c

        c = lax.fori_loop(0, left // n, group, c)
        left = left % n
    o_ref[0] = c[1].astype(BF16)


def stick_attention(qkv3, width, cast=(), blk=256, nh=4, unroll=4):
    bsz, seqlen, _ = qkv3.shape
    wl = nh * HEAD_DIM
    assert wl == LANE_TILE and width % wl == 0 and seqlen % blk == 0
    ngrp = width // wl
    nq = seqlen // blk
    ii = jnp.arange(blk)
    tri = (ii[:, None] > ii[None, :]).astype(BF16)
    nsteps = bsz * ngrp * nq

    def slab_spec(w):
        rows, cols = w.shape
        for c in range(1, cols // LANES + 1):
            r = nsteps // c
            if cols % (LANES * c) == 0 and nsteps % c == 0 and rows % (BF16_ROWS * r) == 0:
                return pl.BlockSpec((rows // r, cols // c),
                                    lambda b, h, i, c=c: (((b * ngrp + h) * nq + i) // c, ((b * ngrp + h) * nq + i) % c))
        raise ValueError(f"cannot cut {w.shape} into {nsteps} blocks")

    slab_specs = [slab_spec(w) for w in cast]
    outs = pl.pallas_call(
        functools.partial(_attn_kernel, blk=blk, nh=nh, unroll=unroll, ncast=len(cast)),
        grid=(bsz, ngrp, nq),
        in_specs=[pl.BlockSpec((1, blk, wl), lambda b, h, i: (b, i, h)),
                  pl.BlockSpec((1, seqlen, wl), lambda b, h, i: (b, 0, ngrp + h)),
                  pl.BlockSpec((1, seqlen, wl), lambda b, h, i: (b, 0, 2 * ngrp + h)),
                  pl.BlockSpec((blk, blk), lambda b, h, i: (0, 0))] + slab_specs,
        out_specs=[pl.BlockSpec((1, blk, wl), lambda b, h, i: (b, i, h))] + slab_specs,
        out_shape=[jax.ShapeDtypeStruct((bsz, seqlen, width), BF16)]
                  + [jax.ShapeDtypeStruct(w.shape, BF16) for w in cast],
        scratch_shapes=[pltpu.VMEM((nq, nh * blk, wl), BF16)],
        compiler_params=_cparams(3),
        name="attn",
    )(qkv3, qkv3, qkv3, tri, *cast)
    return outs[0], list(outs[1:])


def _out_kernel(y_ref, a_ref, x_ref, mod_ref, wglu_ref, gs_ref, ga_ref, w0_ref, w1_ref, gf_ref,
                h_ref, xf_ref, ys_ref, *, rows):
    cpr = rows // CHUNK
    per = LANE_TILE // LANES
    for r0 in range(0, a_ref.shape[0], rows):
        rs = slice(r0, r0 + rows)
        c0 = r0 // CHUNK
        nk = ys_ref.shape[0]
        for s in range(CHUNK):
            for k in range(nk):
                ys_ref[k, pl.ds(s, cpr, stride=CHUNK), :] = (
                    y_ref[0, s, k // per, c0:c0 + cpr, (k % per) * LANES:(k % per + 1) * LANES].astype(F32))
        y = jax.nn.gelu(jnp.concatenate([ys_ref[k] for k in range(nk)], axis=1))
        z = jnp.dot(y.astype(BF16), wglu_ref[...], preferred_element_type=F32)
        ys = y * jax.nn.sigmoid(z)
        ysn = (_rms(ys) * gs_ref[...]).astype(BF16)
        an = (_rms(a_ref[rs, :].astype(F32)) * ga_ref[...]).astype(BF16)
        o = (jnp.dot(ysn, w0_ref[...], preferred_element_type=F32)
             + jnp.dot(an, w1_ref[...], preferred_element_type=F32))
        h = x_ref[rs, :] + mod_ref[0, 2:3, :] * o
        h_ref[rs, :] = h
        xn = _rms(h) * gf_ref[...]
        xf_ref[rs, :] = (xn * (1.0 + mod_ref[0, 4:5, :]) + mod_ref[0, 3:4, :]).astype(BF16)


def out_proj(y_tm, a2, x2, mod3, w_glu_bf, g_ssm, g_attn, w_out_bf, g_ffn, seqlen, tm=512, rows=256):
    t, d = x2.shape
    ngt = y_tm.shape[2]
    ws = ngt * LANE_TILE
    wa = a2.shape[1]
    assert seqlen % tm == 0 and tm % rows == 0 and rows % (CHUNK * BF16_ROWS) == 0 and ws == wa
    tpb = seqlen // tm
    const = lambda shape, r=0: pl.BlockSpec(shape, lambda i: (r, 0), pipeline_mode=pl.Buffered(1))
    return pl.pallas_call(
        functools.partial(_out_kernel, rows=rows),
        grid=(t // tm,),
        in_specs=[pl.BlockSpec((1, CHUNK, ngt, tm // CHUNK, LANE_TILE), lambda i: (i // tpb, 0, 0, i % tpb, 0)),
                  pl.BlockSpec((tm, wa), lambda i: (i, 0)),
                  pl.BlockSpec((tm, d), lambda i: (i, 0)),
                  pl.BlockSpec((1, 6, d), lambda i: (i // tpb, 0, 0)),
                  const((ws, ws)), const((1, ws)), const((1, wa)),
                  const((ws, d)), const((wa, d), 1), const((1, d))],
        out_specs=[pl.BlockSpec((tm, d), lambda i: (i, 0)),
                   pl.BlockSpec((tm, d), lambda i: (i, 0))],
        out_shape=[jax.ShapeDtypeStruct((t, d), F32), jax.ShapeDtypeStruct((t, d), BF16)],
        scratch_shapes=[pltpu.VMEM((ws // LANES, rows, LANES), F32)],
        compiler_params=_cparams(1),
        name="outproj",
    )(y_tm, a2, x2, mod3, w_glu_bf, g_ssm.reshape(1, ws), g_attn.reshape(1, wa), w_out_bf, w_out_bf,
      g_ffn.reshape(1, d))


def _ffn_kernel(xf_ref, h_ref, mod_ref, wg_ref, wu_ref, wd_ref, o_ref):
    j = pl.program_id(1)

    @pl.when(j == 0)
    def _():
        o_ref[...] = jnp.zeros_like(o_ref)

    xf = xf_ref[...]
    g = jnp.dot(xf, wg_ref[...], preferred_element_type=F32)
    u = jnp.dot(xf, wu_ref[...], preferred_element_type=F32)
    hm = (jax.nn.silu(g) * u).astype(BF16)
    o_ref[...] += jnp.dot(hm, wd_ref[...], preferred_element_type=F32)

    @pl.when(j == pl.num_programs(1) - 1)
    def _():
        o_ref[...] = h_ref[...] + mod_ref[0, 5:6, :] * o_ref[...]


def ffn(xf2, h2, mod3, w_gate, w_up, w_down, seqlen, tm=1024, tf=512):
    t, d = h2.shape
    dff = w_gate.shape[1]
    assert seqlen % tm == 0 and dff % tf == 0
    row = pl.BlockSpec((tm, d), lambda i, j: (i, 0), pipeline_mode=pl.Buffered(1))
    return pl.pallas_call(
        _ffn_kernel,
        grid=(t // tm, dff // tf),
        in_specs=[row, row,
                  pl.BlockSpec((1, 6, d), lambda i, j: ((i * tm) // seqlen, 0, 0)),
                  pl.BlockSpec((d, tf), lambda i, j: (0, j)),
                  pl.BlockSpec((d, tf), lambda i, j: (0, j)),
                  pl.BlockSpec((tf, d), lambda i, j: (j, 0))],
        out_specs=pl.BlockSpec((tm, d), lambda i, j: (i, 0)),
        out_shape=jax.ShapeDtypeStruct((t, d), F32),
        compiler_params=_cparams(2),
        name="ffn",
    )(xf2, h2, mod3, w_gate, w_up, w_down)


def s5_mixer_chunked(u_tm, a_re, a_im, log_dt, b_re, b_im, c_re, c_im, d_skip):
    bsz, _, ngt, nc, _ = u_tm.shape
    g, p = a_re.shape
    h = c_re.shape[1]
    assert h * CHUNK == LANE_TILE and g * h == ngt * LANE_TILE
    bl, cl, kx, a = s5_prep(a_re, a_im, log_dt, b_re, b_im, c_re, c_im, d_skip)
    s = s5_chunk_inputs(u_tm, bl.reshape(CHUNK, 2, g * h, p), h)
    return s5_outputs(u_tm, kx.reshape(g * h, CHUNK * h), cl.reshape(CHUNK, 2, g * h, p), s,
                      a.reshape(2, 1, g * p))


def kernel(x, c, w_ada, b_ada, g_mix, w_in, a_re, a_im, log_dt, b_re, b_im, c_re, c_im, d_skip,
           w_glu, q_gain, k_gain, g_ssm_out, g_attn_out, w_out, g_ffn, w_gate, w_up, w_down):
    bsz, seqlen, d = x.shape
    depth = w_ada.shape[0]
    width = w_glu.shape[1]
    h2 = x.reshape(bsz * seqlen, d)
    for l in range(depth):
        c3 = c.reshape(bsz, d, 1)
        b_ada2 = b_ada[l].reshape(1, -1)
        mod_a3 = ada_mod(c3, w_ada[l], b_ada2, 2 * d).reshape(bsz, 2, d)
        u_tm, qkv, mod_b = in_proj(h2, mod_a3, g_mix[l], w_in[l].astype(BF16), q_gain[l], k_gain[l],
                                   c3, w_ada[l], b_ada2, bsz, seqlen)
        mod3 = jnp.concatenate([mod_a3, mod_b.reshape(bsz, -1, d)], axis=1)
        y_tm = s5_mixer_chunked(u_tm, a_re[l], a_im[l], log_dt[l], b_re[l], b_im[l],
                                c_re[l], c_im[l], d_skip[l])
        att, (wglu_bf, wout_bf, wg_bf, wu_bf, wd_bf) = stick_attention(
            qkv.reshape(bsz, seqlen, 3 * width), width,
            cast=(w_glu[l], w_out[l], w_gate[l], w_up[l], w_down[l]))
        h2, xf = out_proj(y_tm, att.reshape(bsz * seqlen, width), h2, mod3,
                          wglu_bf, g_ssm_out[l], g_attn_out[l], wout_bf, g_ffn[l], seqlen)
        h2 = ffn(xf, h2, mod3, wg_bf, wu_bf, wd_bf, seqlen)
    return h2.reshape(bsz, seqlen, d)
```

```python
import functools

import jax
import jax.numpy as jnp
from jax import lax
from jax.experimental import pallas as pl
from jax.experimental.pallas import tpu as pltpu

F32 = jnp.float32
BF16 = jnp.bfloat16
EPS = 1e-6
HEAD_DIM = 64
CHUNK = 16
LANES = 128
SUBLANES = 8
BF16_ROWS = 16
LANE_TILE = 256
VMEM_LIMIT = 56 * 1024 * 1024
Q_SCALE = HEAD_DIM ** -0.5 * 1.4426950408889634


def _cparams(n_axes, vmem=VMEM_LIMIT):
    return pltpu.CompilerParams(dimension_semantics=("arbitrary",) * n_axes,
                                vmem_limit_bytes=vmem)


def _rms(x):
    return x * lax.rsqrt(jnp.mean(x * x, axis=-1, keepdims=True) + EPS)


def _ada_kernel(c_ref, w_ref, b_ref, o_ref):
    d, tn = w_ref.shape
    rows = LANE_TILE
    for b in range(c_ref.shape[0]):
        acc = jnp.zeros((SUBLANES, tn), F32)
        for kc in range(d // rows):
            cond = jax.nn.silu(c_ref[b, kc * rows:(kc + 1) * rows, :])
            blk = cond * w_ref[kc * rows:(kc + 1) * rows, :]
            acc = acc + jnp.sum(blk.reshape(rows // SUBLANES, SUBLANES, tn), axis=0)
        o_ref[b:b + 1, :] = jnp.sum(acc, axis=0, keepdims=True) + b_ref[...]


def ada_mod(c, w_ada, b_ada, tn=1024):
    bsz, d = c.shape
    n = w_ada.shape[1]
    return pl.pallas_call(
        _ada_kernel,
        grid=(n // tn,),
        in_specs=[pl.BlockSpec((bsz, d, 1), lambda j: (0, 0, 0)),
                  pl.BlockSpec((d, tn), lambda j: (0, j)),
                  pl.BlockSpec((1, tn), lambda j: (0, j))],
        out_specs=pl.BlockSpec((bsz, tn), lambda j: (0, j)),
        out_shape=jax.ShapeDtypeStruct((bsz, n), F32),
        compiler_params=_cparams(1),
        name="ada",
    )(c.reshape(bsz, d, 1), w_ada, b_ada.reshape(1, n))


def _head_rmsnorm(r, ones_ref):
    outs = []
    for s in range(r.shape[1] // LANE_TILE):
        rs = r[:, s * LANE_TILE:(s + 1) * LANE_TILE]
        ss = jnp.dot((rs * rs).astype(BF16), ones_ref[...], preferred_element_type=F32)
        outs.append(rs * lax.rsqrt(ss * (1.0 / HEAD_DIM) + EPS))
    return jnp.concatenate(outs, axis=1)


def _proj_kernel(x_ref, mod_ref, g_ref, w_ref, qg_ref, kg_ref, ones_ref, u_ref, o_ref, us_ref, *, rows):
    tm = x_ref.shape[0]
    tn = w_ref.shape[1] // 4
    cpr = rows // CHUNK
    for r0 in range(0, tm, rows):
        xn = _rms(x_ref[r0:r0 + rows, :]) * g_ref[...]
        xm = (xn * (1.0 + mod_ref[0, 1:2, :]) + mod_ref[0, 0:1, :]).astype(BF16)
        for j in range(4):
            r = jnp.dot(xm, w_ref[:, j * tn:(j + 1) * tn], preferred_element_type=F32)
            if j == 0:
                c0 = r0 // CHUNK
                per = LANE_TILE // LANES
                for k in range(tn // LANES):
                    us_ref[k] = r[:, k * LANES:(k + 1) * LANES]
                for s in range(CHUNK):
                    for k in range(tn // LANES):
                        us = us_ref[k, pl.ds(s, cpr, stride=CHUNK), :]
                        u_ref[0, s, k // per, c0:c0 + cpr, (k % per) * LANES:(k % per + 1) * LANES] = us.astype(BF16)
                continue
            if j == 1:
                r = (_head_rmsnorm(r, ones_ref) * qg_ref[...]) * Q_SCALE
            elif j == 2:
                r = _head_rmsnorm(r, ones_ref) * kg_ref[...]
            o_ref[r0:r0 + rows, (j - 1) * tn:j * tn] = r.astype(BF16)


def in_proj(x2, mod3, g_mix, w_in_bf, q_gain, k_gain, bsz, seqlen, tm=512, rows=256):
    t, d = x2.shape
    n = w_in_bf.shape[1]
    tn = n // 4
    assert seqlen % tm == 0 and tm % rows == 0 and rows % (CHUNK * BF16_ROWS) == 0 and tn % LANE_TILE == 0
    reps = tn // HEAD_DIM
    qg = jnp.tile(q_gain.reshape(1, HEAD_DIM), (1, reps))
    kg = jnp.tile(k_gain.reshape(1, HEAD_DIM), (1, reps))
    idx = jnp.arange(LANE_TILE) // HEAD_DIM
    ones = (idx[:, None] == idx[None, :]).astype(BF16)
    const = lambda shape: pl.BlockSpec(shape, lambda i: (0, 0), pipeline_mode=pl.Buffered(1))
    tpb = seqlen // tm
    ngt = tn // LANE_TILE
    return pl.pallas_call(
        functools.partial(_proj_kernel, rows=rows),
        grid=(t // tm,),
        in_specs=[pl.BlockSpec((tm, d), lambda i: (i, 0)),
                  pl.BlockSpec((1, 6, d), lambda i: (i // tpb, 0, 0)),
                  const((1, d)), const((d, n)), const((1, tn)), const((1, tn)),
                  const((LANE_TILE, LANE_TILE))],
        out_specs=[pl.BlockSpec((1, CHUNK, ngt, tm // CHUNK, LANE_TILE), lambda i: (i // tpb, 0, 0, i % tpb, 0)),
                   pl.BlockSpec((tm, 3 * tn), lambda i: (i, 0))],
        out_shape=[jax.ShapeDtypeStruct((bsz, CHUNK, ngt, seqlen // CHUNK, LANE_TILE), BF16),
                   jax.ShapeDtypeStruct((t, 3 * tn), BF16)],
        scratch_shapes=[pltpu.VMEM((tn // LANES, rows, LANES), F32)],
        compiler_params=_cparams(1),
        name="proj",
    )(x2, mod3, g_mix.reshape(1, d), w_in_bf, qg, kg, ones)


def _zoh(ar, ai, dt):
    mag = jnp.exp(ar * dt)
    th = ai * dt
    lbr, lbi = mag * jnp.cos(th), mag * jnp.sin(th)
    den = ar * ar + ai * ai
    nr, ni = lbr - 1.0, lbi
    return lbr, lbi, (nr * ar + ni * ai) / den, (ni * ar - nr * ai) / den


def _s5prep_kernel(ar_ref, ai_ref, ldt_ref, btr_ref, bti_ref, cr_ref, ci_ref, d_ref,
                   bl_ref, cl_ref, kx_ref, a_ref):
    dt = jnp.exp(ldt_ref[...])
    lbr, lbi, kr, ki = _zoh(ar_ref[...], ai_ref[...], dt)
    btr, bti = btr_ref[...], bti_ref[...]
    bbr = kr * btr - ki * bti
    bbi = kr * bti + ki * btr
    c_re, c_im = cr_ref[...], ci_ref[...]
    h = c_re.shape[1]
    eye = (lax.broadcasted_iota(jnp.int32, (1, h, h), 1) ==
           lax.broadcasted_iota(jnp.int32, (1, h, h), 2))
    dn = (((2,), (2,)), ((0,), (0,)))
    pr, pi = jnp.ones_like(lbr), jnp.zeros_like(lbr)
    blrs, blis = [], []
    for j in range(CHUNK):
        blr = pr * bbr - pi * bbi
        bli = pr * bbi + pi * bbr
        bl_ref[j, 0] = blr
        bl_ref[j, 1] = bli
        blrs.append(blr)
        blis.append(bli)
        pr, pi = pr * lbr - pi * lbi, pr * lbi + pi * lbr
        cl_ref[j, 0] = c_re * pr - c_im * pi
        cl_ref[j, 1] = -(c_re * pi + c_im * pr)
    a_ref[0] = pr
    a_ref[1] = pi
    kx = (lax.dot_general(jnp.concatenate(blrs, axis=1), c_re, dn, precision=lax.Precision.HIGHEST,
                          preferred_element_type=F32)
          - lax.dot_general(jnp.concatenate(blis, axis=1), c_im, dn, precision=lax.Precision.HIGHEST,
                            preferred_element_type=F32))
    for j in range(CHUNK):
        kxj = kx[:, j * h:(j + 1) * h, :]
        if j == 0:
            kxj = kxj + jnp.where(eye, d_ref[...], 0.0)
        kx_ref[:, :, j * h:(j + 1) * h] = kxj


def s5_prep(a_re, a_im, log_dt, b_re, b_im, c_re, c_im, d_skip, gb=8):
    g, p = a_re.shape
    h = c_re.shape[1]
    vec = pl.BlockSpec((gb, 1, p), lambda i: (i, 0, 0))
    mat = pl.BlockSpec((gb, h, p), lambda i: (i, 0, 0))
    lag = pl.BlockSpec((CHUNK, 2, gb, h, p), lambda i: (0, 0, i, 0, 0))
    lag_shape = jax.ShapeDtypeStruct((CHUNK, 2, g, h, p), F32)
    return pl.pallas_call(
        _s5prep_kernel,
        grid=(g // gb,),
        in_specs=[vec, vec, pl.BlockSpec((gb, 1, 1), lambda i: (i, 0, 0)),
                  mat, mat, mat, mat, pl.BlockSpec((gb, 1, h), lambda i: (i, 0, 0))],
        out_specs=[lag, lag,
                   pl.BlockSpec((gb, h, CHUNK * h), lambda i: (i, 0, 0)),
                   pl.BlockSpec((2, gb, 1, p), lambda i: (0, i, 0, 0))],
        out_shape=[lag_shape, lag_shape,
                   jax.ShapeDtypeStruct((g, h, CHUNK * h), F32),
                   jax.ShapeDtypeStruct((2, g, 1, p), F32)],
        compiler_params=_cparams(1),
        name="s5prep",
    )(a_re.reshape(g, 1, p), a_im.reshape(g, 1, p), log_dt.reshape(g, 1, 1),
      b_re.transpose(0, 2, 1), b_im.transpose(0, 2, 1), c_re, c_im, d_skip.reshape(g, 1, h))


def _group_mask(shape, row_div, col_div):
    return (lax.broadcasted_iota(jnp.int32, shape, 0) // row_div ==
            lax.broadcasted_iota(jnp.int32, shape, 1) // col_div)


def _s5in_kernel(u_ref, bl_ref, o_ref, w_ref):
    p, wid = bl_ref.shape[3], o_ref.shape[3]
    h = LANE_TILE // (wid // p)
    mask = _group_mask((LANE_TILE, wid), h, p)
    for s in range(CHUNK):
        ex = jnp.concatenate([bl_ref[CHUNK - 1 - s, 0]] * (wid // p), axis=1)
        w_ref[s * LANE_TILE:(s + 1) * LANE_TILE, :] = jnp.where(mask, ex, 0.0).astype(BF16)
    for b in range(u_ref.shape[0]):
        ucat = jnp.concatenate([u_ref[b, s, 0] for s in range(CHUNK)], axis=1)
        o_ref[b, 0] = jnp.dot(ucat, w_ref[...], preferred_element_type=F32)


def s5_chunk_inputs(u_tm, bl2, h):
    bsz, _, ngt, nc, _ = u_tm.shape
    p = bl2.shape[3]
    wid = (LANE_TILE // h) * p
    return pl.pallas_call(
        _s5in_kernel,
        grid=(ngt, 2),
        in_specs=[pl.BlockSpec((bsz, CHUNK, 1, nc, LANE_TILE), lambda gt, ri: (0, 0, gt, 0, 0)),
                  pl.BlockSpec((CHUNK, 1, LANE_TILE, p), lambda gt, ri: (0, ri, gt, 0))],
        out_specs=pl.BlockSpec((bsz, 1, nc, wid), lambda gt, ri: (0, ri, 0, gt)),
        out_shape=jax.ShapeDtypeStruct((bsz, 2, nc, ngt * wid), F32),
        scratch_shapes=[pltpu.VMEM((CHUNK * LANE_TILE, wid), BF16)],
        compiler_params=_cparams(2),
        name="s5in",
    )(u_tm, bl2)


def _chunk_scan(s_ref, a_ref, x_ref, b):
    nc, w = s_ref.shape[2], s_ref.shape[3]
    a1r, a1i = a_ref[0], a_ref[1]

    def cmul(xr, xi, yr, yi):
        return xr * yr - xi * yi, xr * yi + xi * yr

    n = SUBLANES
    row = lax.broadcasted_iota(jnp.int32, (n, w), 0)
    pows = [(a1r, a1i)]
    for _ in range(n - 1):
        pows.append(cmul(pows[-1][0], pows[-1][1], a1r, a1i))
    steps = []
    for k in (1 << i for i in range(n.bit_length() - 1)):
        steps.append((k, jnp.where(row >= k, pows[k - 1][0], 0.0), jnp.where(row >= k, pows[k - 1][1], 0.0)))
    pcr = jnp.zeros((n, w), F32)
    pci = jnp.zeros((n, w), F32)
    for r in range(n):
        pcr = jnp.where(row == r, pows[r][0], pcr)
        pci = jnp.where(row == r, pows[r][1], pci)

    def body(blk, carry):
        cr, ci = carry
        off = pl.multiple_of(blk * n, n)
        xr = s_ref[b, 0, pl.ds(off, n), :]
        xi = s_ref[b, 1, pl.ds(off, n), :]
        for k, kr, ki in steps:
            sr, si = pltpu.roll(xr, k, 0), pltpu.roll(xi, k, 0)
            dr, di = cmul(sr, si, kr, ki)
            xr, xi = xr + dr, xi + di
        dr, di = cmul(jnp.broadcast_to(cr, (n, w)), jnp.broadcast_to(ci, (n, w)), pcr, pci)
        xr, xi = xr + dr, xi + di
        x_ref[b, 0, pl.ds(off, n), :] = jnp.where(row == 0, cr, pltpu.roll(xr, 1, 0))
        x_ref[b, 1, pl.ds(off, n), :] = jnp.where(row == 0, ci, pltpu.roll(xi, 1, 0))
        return xr[n - 1:n, :], xi[n - 1:n, :]

    zero = jnp.zeros((1, w), F32)
    lax.fori_loop(0, nc // n, body, (zero, zero))


def _s5out_kernel(u_ref, kx_ref, cl_ref, s_ref, a_ref, e_ref, o_ref, tr_ref, x_ref):
    bsz, _, _, nc, _ = u_ref.shape
    p, wid = cl_ref.shape[3], x_ref.shape[3]
    h = LANE_TILE // CHUNK
    mask_t = _group_mask((LANE_TILE, LANE_TILE), h, h)
    mask_c = _group_mask((LANE_TILE, wid), h, p)
    nt = (((1,), (1,)), ((), ()))
    for b in range(bsz):
        _chunk_scan(s_ref, a_ref, x_ref, b)
    kx = kx_ref[...].astype(BF16)
    for j in range(CHUNK):
        tj = jnp.where(mask_t, jnp.dot(kx, e_ref[j], preferred_element_type=F32), 0.0)
        tr_ref[(CHUNK - 1 - j) * LANE_TILE:(CHUNK - j) * LANE_TILE, :] = tj.astype(BF16)
    ucat = jnp.concatenate(
        [jnp.concatenate([u_ref[b, s, 0] for s in range(CHUNK)], axis=1) for b in range(bsz)], axis=0)
    xr = x_ref[:, 0].reshape(bsz * nc, wid).astype(BF16)
    xi = x_ref[:, 1].reshape(bsz * nc, wid).astype(BF16)
    for t in range(CHUNK):
        wr = jnp.concatenate([cl_ref[t, 0]] * (wid // p), axis=1)
        wi = jnp.concatenate([cl_ref[t, 1]] * (wid // p), axis=1)
        wr = jnp.where(mask_c, wr, 0.0).astype(BF16)
        wi = jnp.where(mask_c, wi, 0.0).astype(BF16)
        y = (jnp.dot(ucat[:, :(t + 1) * LANE_TILE], tr_ref[(CHUNK - 1 - t) * LANE_TILE:, :],
                     preferred_element_type=F32)
             + lax.dot_general(xr, wr, nt, preferred_element_type=F32)
             + lax.dot_general(xi, wi, nt, preferred_element_type=F32))
        for b in range(bsz):
            o_ref[b, t, 0] = y[b * nc:(b + 1) * nc].astype(BF16)


def s5_outputs(u_tm, kx2, cl2, s, a):
    bsz, _, ngt, nc, _ = u_tm.shape
    wid = s.shape[3] // ngt
    p = cl2.shape[3]
    once = pl.Buffered(1)
    h = LANE_TILE // CHUNK
    r = jnp.arange(LANE_TILE)
    e = ((r[None, :, None] // h == jnp.arange(CHUNK)[:, None, None])
         & (r[None, :, None] % h == r[None, None, :] % h)).astype(BF16)
    tile = pl.BlockSpec((bsz, CHUNK, 1, nc, LANE_TILE), lambda gt: (0, 0, gt, 0, 0))
    return pl.pallas_call(
        _s5out_kernel,
        grid=(ngt,),
        in_specs=[tile,
                  pl.BlockSpec((LANE_TILE, LANE_TILE), lambda gt: (gt, 0)),
                  pl.BlockSpec((CHUNK, 2, LANE_TILE, p), lambda gt: (0, 0, gt, 0)),
                  pl.BlockSpec((bsz, 2, nc, wid), lambda gt: (0, 0, 0, gt)),
                  pl.BlockSpec((2, 1, wid), lambda gt: (0, 0, gt)),
                  pl.BlockSpec((CHUNK, LANE_TILE, LANE_TILE), lambda gt: (0, 0, 0), pipeline_mode=once)],
        out_specs=tile,
        out_shape=jax.ShapeDtypeStruct(u_tm.shape, BF16),
        scratch_shapes=[pltpu.VMEM((CHUNK * LANE_TILE, LANE_TILE), BF16),
                        pltpu.VMEM((bsz, 2, nc, wid), F32)],
        compiler_params=_cparams(1),
        name="s5out",
    )(u_tm, kx2, cl2, s, a, e)


def _attn_kernel(q_ref, k_ref, v_ref, tri_ref, *rest, blk, nh, unroll, ncast):
    src_refs, o_ref, dst_refs, vcat_ref = rest[:ncast], rest[ncast], rest[ncast + 1:2 * ncast + 1], rest[-1]
    for src, dst in zip(src_refs, dst_refs):
        dst[...] = src[...].astype(BF16)
    qi = pl.program_id(2)
    q = q_ref[0]
    lane = lax.broadcasted_iota(jnp.int32, q.shape, 1)
    heads = [(lane >= HEAD_DIM * h) & (lane < HEAD_DIM * (h + 1)) for h in range(nh)]
    zero = jnp.zeros_like(q)

    @pl.when(qi == 0)
    def _():
        def fill(j, _):
            vb = v_ref[0, pl.ds(pl.multiple_of(j * blk, blk), blk), :]
            vcat_ref[j] = jnp.concatenate([jnp.where(m, vb, zero) for m in heads], axis=0)
            return 0

        lax.fori_loop(0, vcat_ref.shape[0], fill, 0)

    qs = jnp.concatenate([jnp.where(m, q, zero) for m in heads], axis=0)
    row = lax.broadcasted_iota(jnp.int32, (blk, blk), 0)
    col = lax.broadcasted_iota(jnp.int32, (blk, blk), 1)
    past = jnp.concatenate([col < row] * nh, axis=0)
    nt = (((1,), (1,)), ((), ()))

    def block(kj, carry, acc, diag):
        kb = k_ref[0, pl.ds(pl.multiple_of(kj * blk, blk), blk), :]
        z = lax.dot_general(qs, kb, nt, preferred_element_type=F32)
        sp = jnp.maximum(z, 0.0) + jnp.log2(1.0 + jnp.exp2(-jnp.abs(z)))
        if diag:
            sp = jnp.where(past, sp, 0.0)
        cs = lax.dot_general(sp, tri_ref[...], (((1,), (0,)), ((), ())), preferred_element_type=F32)
        w = jnp.exp2(((z - sp) - cs) - carry)
        if diag:
            w = jnp.where(past, w, 0.0)
        wcat = jnp.concatenate([w[h * blk:(h + 1) * blk] for h in range(nh)], axis=1)
        acc = acc + lax.dot_general(wcat, vcat_ref[kj], (((1,), (0,)), ((), ())), preferred_element_type=F32)
        return carry + (cs[:, 0:1] + sp[:, 0:1]), acc

    def head(r):
        def run():
            c = block(qi, jnp.zeros((nh * blk, 1), F32), jnp.zeros(q.shape, F32), True)
            for i in range(r):
                c = block(qi - 1 - i, c[0], c[1], False)
            return c
        return run

    rem = qi % unroll
    c = lax.switch(rem, [head(r) for r in range(unroll)])

    left = qi - rem
    for n in (2 * unroll, unroll):
        def group(i, c, n=n, left=left):
            for r in range(n):
                c = block(left - 1 - r - n * i, c[0], c[1], False)
            return c

        c = lax.fori_loop(0, left // n, group, c)
        left = left % n
    o_ref[0] = c[1].astype(BF16)


def stick_attention(qkv3, width, cast=(), blk=256, nh=4, unroll=4):
    bsz, seqlen, _ = qkv3.shape
    wl = nh * HEAD_DIM
    assert wl == LANE_TILE and width % wl == 0 and seqlen % blk == 0
    ngrp = width // wl
    nq = seqlen // blk
    ii = jnp.arange(blk)
    tri = (ii[:, None] > ii[None, :]).astype(BF16)
    nsteps = bsz * ngrp * nq

    def slab_spec(w):
        rows, cols = w.shape
        for c in range(1, cols // LANES + 1):
            r = nsteps // c
            if cols % (LANES * c) == 0 and nsteps % c == 0 and rows % (BF16_ROWS * r) == 0:
                return pl.BlockSpec((rows // r, cols // c),
                                    lambda b, h, i, c=c: (((b * ngrp + h) * nq + i) // c, ((b * ngrp + h) * nq + i) % c))
        raise ValueError(f"cannot cut {w.shape} into {nsteps} blocks")

    slab_specs = [slab_spec(w) for w in cast]
    outs = pl.pallas_call(
        functools.partial(_attn_kernel, blk=blk, nh=nh, unroll=unroll, ncast=len(cast)),
        grid=(bsz, ngrp, nq),
        in_specs=[pl.BlockSpec((1, blk, wl), lambda b, h, i: (b, i, h)),
                  pl.BlockSpec((1, seqlen, wl), lambda b, h, i: (b, 0, ngrp + h)),
                  pl.BlockSpec((1, seqlen, wl), lambda b, h, i: (b, 0, 2 * ngrp + h)),
                  pl.BlockSpec((blk, blk), lambda b, h, i: (0, 0))] + slab_specs,
        out_specs=[pl.BlockSpec((1, blk, wl), lambda b, h, i: (b, i, h))] + slab_specs,
        out_shape=[jax.ShapeDtypeStruct((bsz, seqlen, width), BF16)]
                  + [jax.ShapeDtypeStruct(w.shape, BF16) for w in cast],
        scratch_shapes=[pltpu.VMEM((nq, nh * blk, wl), BF16)],
        compiler_params=_cparams(3),
        name="attn",
    )(qkv3, qkv3, qkv3, tri, *cast)
    return outs[0], list(outs[1:])


def _out_kernel(y_ref, a_ref, x_ref, mod_ref, wglu_ref, gs_ref, ga_ref, w0_ref, w1_ref, gf_ref,
                h_ref, xf_ref, ys_ref, *, rows):
    cpr = rows // CHUNK
    per = LANE_TILE // LANES
    for r0 in range(0, a_ref.shape[0], rows):
        rs = slice(r0, r0 + rows)
        c0 = r0 // CHUNK
        nk = ys_ref.shape[0]
        for s in range(CHUNK):
            for k in range(nk):
                ys_ref[k, pl.ds(s, cpr, stride=CHUNK), :] = (
                    y_ref[0, s, k // per, c0:c0 + cpr, (k % per) * LANES:(k % per + 1) * LANES].astype(F32))
        y = jax.nn.gelu(jnp.concatenate([ys_ref[k] for k in range(nk)], axis=1))
        z = jnp.dot(y.astype(BF16), wglu_ref[...], preferred_element_type=F32)
        ys = y * jax.nn.sigmoid(z)
        ysn = (_rms(ys) * gs_ref[...]).astype(BF16)
        an = (_rms(a_ref[rs, :].astype(F32)) * ga_ref[...]).astype(BF16)
        o = (jnp.dot(ysn, w0_ref[...], preferred_element_type=F32)
             + jnp.dot(an, w1_ref[...], preferred_element_type=F32))
        h = x_ref[rs, :] + mod_ref[0, 2:3, :] * o
        h_ref[rs, :] = h
        xn = _rms(h) * gf_ref[...]
        xf_ref[rs, :] = (xn * (1.0 + mod_ref[0, 4:5, :]) + mod_ref[0, 3:4, :]).astype(BF16)


def out_proj(y_tm, a2, x2, mod3, w_glu_bf, g_ssm, g_attn, w_out_bf, g_ffn, seqlen, tm=512, rows=256):
    t, d = x2.shape
    ngt = y_tm.shape[2]
    ws = ngt * LANE_TILE
    wa = a2.shape[1]
    assert seqlen % tm == 0 and tm % rows == 0 and rows % (CHUNK * BF16_ROWS) == 0 and ws == wa
    tpb = seqlen // tm
    const = lambda shape, r=0: pl.BlockSpec(shape, lambda i: (r, 0), pipeline_mode=pl.Buffered(1))
    return pl.pallas_call(
        functools.partial(_out_kernel, rows=rows),
        grid=(t // tm,),
        in_specs=[pl.BlockSpec((1, CHUNK, ngt, tm // CHUNK, LANE_TILE), lambda i: (i // tpb, 0, 0, i % tpb, 0)),
                  pl.BlockSpec((tm, wa), lambda i: (i, 0)),
                  pl.BlockSpec((tm, d), lambda i: (i, 0)),
                  pl.BlockSpec((1, 6, d), lambda i: (i // tpb, 0, 0)),
                  const((ws, ws)), const((1, ws)), const((1, wa)),
                  const((ws, d)), const((wa, d), 1), const((1, d))],
        out_specs=[pl.BlockSpec((tm, d), lambda i: (i, 0)),
                   pl.BlockSpec((tm, d), lambda i: (i, 0))],
        out_shape=[jax.ShapeDtypeStruct((t, d), F32), jax.ShapeDtypeStruct((t, d), BF16)],
        scratch_shapes=[pltpu.VMEM((ws // LANES, rows, LANES), F32)],
        compiler_params=_cparams(1),
        name="outproj",
    )(y_tm, a2, x2, mod3, w_glu_bf, g_ssm.reshape(1, ws), g_attn.reshape(1, wa), w_out_bf, w_out_bf,
      g_ffn.reshape(1, d))


def _ffn_kernel(xf_ref, h_ref, mod_ref, wg_ref, wu_ref, wd_ref, o_ref):
    j = pl.program_id(1)

    @pl.when(j == 0)
    def _():
        o_ref[...] = jnp.zeros_like(o_ref)

    xf = xf_ref[...]
    g = jnp.dot(xf, wg_ref[...], preferred_element_type=F32)
    u = jnp.dot(xf, wu_ref[...], preferred_element_type=F32)
    hm = (jax.nn.silu(g) * u).astype(BF16)
    o_ref[...] += jnp.dot(hm, wd_ref[...], preferred_element_type=F32)

    @pl.when(j == pl.num_programs(1) - 1)
    def _():
        o_ref[...] = h_ref[...] + mod_ref[0, 5:6, :] * o_ref[...]


def ffn(xf2, h2, mod3, w_gate, w_up, w_down, seqlen, tm=1024, tf=512):
    t, d = h2.shape
    dff = w_gate.shape[1]
    assert seqlen % tm == 0 and dff % tf == 0
    return pl.pallas_call(
        _ffn_kernel,
        grid=(t // tm, dff // tf),
        in_specs=[pl.BlockSpec((tm, d), lambda i, j: (i, 0)),
                  pl.BlockSpec((tm, d), lambda i, j: (i, 0), pipeline_mode=pl.Buffered(1)),
                  pl.BlockSpec((1, 6, d), lambda i, j: ((i * tm) // seqlen, 0, 0)),
                  pl.BlockSpec((d, tf), lambda i, j: (0, j)),
                  pl.BlockSpec((d, tf), lambda i, j: (0, j)),
                  pl.BlockSpec((tf, d), lambda i, j: (j, 0))],
        out_specs=pl.BlockSpec((tm, d), lambda i, j: (i, 0)),
        out_shape=jax.ShapeDtypeStruct((t, d), F32),
        compiler_params=_cparams(2),
        name="ffn",
    )(xf2, h2, mod3, w_gate, w_up, w_down)


def s5_mixer_chunked(u_tm, a_re, a_im, log_dt, b_re, b_im, c_re, c_im, d_skip):
    bsz, _, ngt, nc, _ = u_tm.shape
    g, p = a_re.shape
    h = c_re.shape[1]
    assert h * CHUNK == LANE_TILE and g * h == ngt * LANE_TILE
    bl, cl, kx, a = s5_prep(a_re, a_im, log_dt, b_re, b_im, c_re, c_im, d_skip)
    s = s5_chunk_inputs(u_tm, bl.reshape(CHUNK, 2, g * h, p), h)
    return s5_outputs(u_tm, kx.reshape(g * h, CHUNK * h), cl.reshape(CHUNK, 2, g * h, p), s,
                      a.reshape(2, 1, g * p))


def kernel(x, c, w_ada, b_ada, g_mix, w_in, a_re, a_im, log_dt, b_re, b_im, c_re, c_im, d_skip,
           w_glu, q_gain, k_gain, g_ssm_out, g_attn_out, w_out, g_ffn, w_gate, w_up, w_down):
    bsz, seqlen, d = x.shape
    depth = w_ada.shape[0]
    width = w_glu.shape[1]
    h2 = x.reshape(bsz * seqlen, d)
    for l in range(depth):
        mod3 = ada_mod(c, w_ada[l], b_ada[l]).reshape(bsz, 6, d)
        u_tm, qkv = in_proj(h2, mod3, g_mix[l], w_in[l].astype(BF16), q_gain[l], k_gain[l], bsz, seqlen)
        y_tm = s5_mixer_chunked(u_tm, a_re[l], a_im[l], log_dt[l], b_re[l], b_im[l],
                                c_re[l], c_im[l], d_skip[l])
        att, (wglu_bf, wout_bf, wg_bf, wu_bf, wd_bf) = stick_attention(
            qkv.reshape(bsz, seqlen, 3 * width), width,
            cast=(w_glu[l], w_out[l], w_gate[l], w_up[l], w_down[l]))
        h2, xf = out_proj(y_tm, att.reshape(bsz * seqlen, width), h2, mod3,
                          wglu_bf, g_ssm_out[l], g_attn_out[l], wout_bf, g_ffn[l], seqlen)
        h2 = ffn(xf, h2, mod3, wg_bf, wu_bf, wd_bf, seqlen)
    return h2.reshape(bsz, seqlen, d)
```

```python
import functools

import jax
import jax.numpy as jnp
from jax import lax
from jax.experimental import pallas as pl
from jax.experimental.pallas import tpu as pltpu

F32 = jnp.float32
BF16 = jnp.bfloat16
EPS = 1e-6
HEAD_DIM = 64
CHUNK = 16
LANES = 128
SUBLANES = 8
BF16_ROWS = 16
LANE_TILE = 256
VMEM_LIMIT = 56 * 1024 * 1024
Q_SCALE = HEAD_DIM ** -0.5 * 1.4426950408889634


def _cparams(n_axes, vmem=VMEM_LIMIT):
    return pltpu.CompilerParams(dimension_semantics=("arbitrary",) * n_axes,
                                vmem_limit_bytes=vmem)


def _rms(x):
    return x * lax.rsqrt(jnp.mean(x * x, axis=-1, keepdims=True) + EPS)


def _ada_kernel(c_ref, w_ref, b_ref, o_ref):
    d, tn = w_ref.shape
    rows = LANE_TILE
    for b in range(c_ref.shape[0]):
        acc = jnp.zeros((SUBLANES, tn), F32)
        for kc in range(d // rows):
            cond = jax.nn.silu(c_ref[b, kc * rows:(kc + 1) * rows, :])
            blk = cond * w_ref[kc * rows:(kc + 1) * rows, :]
            acc = acc + jnp.sum(blk.reshape(rows // SUBLANES, SUBLANES, tn), axis=0)
        o_ref[b:b + 1, :] = jnp.sum(acc, axis=0, keepdims=True) + b_ref[...]


def ada_mod(c, w_ada, b_ada, tn=1024):
    bsz, d = c.shape
    n = w_ada.shape[1]
    return pl.pallas_call(
        _ada_kernel,
        grid=(n // tn,),
        in_specs=[pl.BlockSpec((bsz, d, 1), lambda j: (0, 0, 0)),
                  pl.BlockSpec((d, tn), lambda j: (0, j)),
                  pl.BlockSpec((1, tn), lambda j: (0, j))],
        out_specs=pl.BlockSpec((bsz, tn), lambda j: (0, j)),
        out_shape=jax.ShapeDtypeStruct((bsz, n), F32),
        compiler_params=_cparams(1),
        name="ada",
    )(c.reshape(bsz, d, 1), w_ada, b_ada.reshape(1, n))


def _head_rmsnorm(r, ones_ref):
    outs = []
    for s in range(r.shape[1] // LANE_TILE):
        rs = r[:, s * LANE_TILE:(s + 1) * LANE_TILE]
        ss = jnp.dot((rs * rs).astype(BF16), ones_ref[...], preferred_element_type=F32)
        outs.append(rs * lax.rsqrt(ss * (1.0 / HEAD_DIM) + EPS))
    return jnp.concatenate(outs, axis=1)


def _proj_kernel(x_ref, mod_ref, g_ref, w_ref, qg_ref, kg_ref, ones_ref, u_ref, o_ref, us_ref, *, rows):
    tm = x_ref.shape[0]
    tn = w_ref.shape[1] // 4
    cpr = rows // CHUNK
    for r0 in range(0, tm, rows):
        xn = _rms(x_ref[r0:r0 + rows, :]) * g_ref[...]
        xm = (xn * (1.0 + mod_ref[0, 1:2, :]) + mod_ref[0, 0:1, :]).astype(BF16)
        for j in range(4):
            r = jnp.dot(xm, w_ref[:, j * tn:(j + 1) * tn], preferred_element_type=F32)
            if j == 0:
                c0 = r0 // CHUNK
                per = LANE_TILE // LANES
                for k in range(tn // LANES):
                    us_ref[k] = r[:, k * LANES:(k + 1) * LANES]
                for s in range(CHUNK):
                    for k in range(tn // LANES):
                        us = us_ref[k, pl.ds(s, cpr, stride=CHUNK), :]
                        u_ref[0, s, k // per, c0:c0 + cpr, (k % per) * LANES:(k % per + 1) * LANES] = us.astype(BF16)
                continue
            if j == 1:
                r = (_head_rmsnorm(r, ones_ref) * qg_ref[...]) * Q_SCALE
            elif j == 2:
                r = _head_rmsnorm(r, ones_ref) * kg_ref[...]
            o_ref[r0:r0 + rows, (j - 1) * tn:j * tn] = r.astype(BF16)


def in_proj(x2, mod3, g_mix, w_in_bf, q_gain, k_gain, bsz, seqlen, tm=512, rows=256):
    t, d = x2.shape
    n = w_in_bf.shape[1]
    tn = n // 4
    assert seqlen % tm == 0 and tm % rows == 0 and rows % (CHUNK * BF16_ROWS) == 0 and tn % LANE_TILE == 0
    reps = tn // HEAD_DIM
    qg = jnp.tile(q_gain.reshape(1, HEAD_DIM), (1, reps))
    kg = jnp.tile(k_gain.reshape(1, HEAD_DIM), (1, reps))
    idx = jnp.arange(LANE_TILE) // HEAD_DIM
    ones = (idx[:, None] == idx[None, :]).astype(BF16)
    const = lambda shape: pl.BlockSpec(shape, lambda i: (0, 0), pipeline_mode=pl.Buffered(1))
    tpb = seqlen // tm
    ngt = tn // LANE_TILE
    return pl.pallas_call(
        functools.partial(_proj_kernel, rows=rows),
        grid=(t // tm,),
        in_specs=[pl.BlockSpec((tm, d), lambda i: (i, 0)),
                  pl.BlockSpec((1, 6, d), lambda i: (i // tpb, 0, 0)),
                  const((1, d)), const((d, n)), const((1, tn)), const((1, tn)),
                  const((LANE_TILE, LANE_TILE))],
        out_specs=[pl.BlockSpec((1, CHUNK, ngt, tm // CHUNK, LANE_TILE), lambda i: (i // tpb, 0, 0, i % tpb, 0)),
                   pl.BlockSpec((tm, 3 * tn), lambda i: (i, 0))],
        out_shape=[jax.ShapeDtypeStruct((bsz, CHUNK, ngt, seqlen // CHUNK, LANE_TILE), BF16),
                   jax.ShapeDtypeStruct((t, 3 * tn), BF16)],
        scratch_shapes=[pltpu.VMEM((tn // LANES, rows, LANES), F32)],
        compiler_params=_cparams(1),
        name="proj",
    )(x2, mod3, g_mix.reshape(1, d), w_in_bf, qg, kg, ones)


def _zoh(ar, ai, dt):
    mag = jnp.exp(ar * dt)
    th = ai * dt
    lbr, lbi = mag * jnp.cos(th), mag * jnp.sin(th)
    den = ar * ar + ai * ai
    nr, ni = lbr - 1.0, lbi
    return lbr, lbi, (nr * ar + ni * ai) / den, (ni * ar - nr * ai) / den


def _s5prep_kernel(ar_ref, ai_ref, ldt_ref, btr_ref, bti_ref, cr_ref, ci_ref, d_ref,
                   bl_ref, cl_ref, kx_ref, a_ref):
    dt = jnp.exp(ldt_ref[...])
    lbr, lbi, kr, ki = _zoh(ar_ref[...], ai_ref[...], dt)
    btr, bti = btr_ref[...], bti_ref[...]
    bbr = kr * btr - ki * bti
    bbi = kr * bti + ki * btr
    c_re, c_im = cr_ref[...], ci_ref[...]
    h = c_re.shape[1]
    eye = (lax.broadcasted_iota(jnp.int32, (1, h, h), 1) ==
           lax.broadcasted_iota(jnp.int32, (1, h, h), 2))
    dn = (((2,), (2,)), ((0,), (0,)))
    pr, pi = jnp.ones_like(lbr), jnp.zeros_like(lbr)
    blrs, blis = [], []
    for j in range(CHUNK):
        blr = pr * bbr - pi * bbi
        bli = pr * bbi + pi * bbr
        bl_ref[j, 0] = blr
        bl_ref[j, 1] = bli
        blrs.append(blr)
        blis.append(bli)
        pr, pi = pr * lbr - pi * lbi, pr * lbi + pi * lbr
        cl_ref[j, 0] = c_re * pr - c_im * pi
        cl_ref[j, 1] = -(c_re * pi + c_im * pr)
    a_ref[0] = pr
    a_ref[1] = pi
    kx = (lax.dot_general(jnp.concatenate(blrs, axis=1), c_re, dn, precision=lax.Precision.HIGHEST,
                          preferred_element_type=F32)
          - lax.dot_general(jnp.concatenate(blis, axis=1), c_im, dn, precision=lax.Precision.HIGHEST,
                            preferred_element_type=F32))
    for j in range(CHUNK):
        kxj = kx[:, j * h:(j + 1) * h, :]
        if j == 0:
            kxj = kxj + jnp.where(eye, d_ref[...], 0.0)
        kx_ref[:, :, j * h:(j + 1) * h] = kxj


def s5_prep(a_re, a_im, log_dt, b_re, b_im, c_re, c_im, d_skip, gb=8):
    g, p = a_re.shape
    h = c_re.shape[1]
    vec = pl.BlockSpec((gb, 1, p), lambda i: (i, 0, 0))
    mat = pl.BlockSpec((gb, h, p), lambda i: (i, 0, 0))
    lag = pl.BlockSpec((CHUNK, 2, gb, h, p), lambda i: (0, 0, i, 0, 0))
    lag_shape = jax.ShapeDtypeStruct((CHUNK, 2, g, h, p), F32)
    return pl.pallas_call(
        _s5prep_kernel,
        grid=(g // gb,),
        in_specs=[vec, vec, pl.BlockSpec((gb, 1, 1), lambda i: (i, 0, 0)),
                  mat, mat, mat, mat, pl.BlockSpec((gb, 1, h), lambda i: (i, 0, 0))],
        out_specs=[lag, lag,
                   pl.BlockSpec((gb, h, CHUNK * h), lambda i: (i, 0, 0)),
                   pl.BlockSpec((2, gb, 1, p), lambda i: (0, i, 0, 0))],
        out_shape=[lag_shape, lag_shape,
                   jax.ShapeDtypeStruct((g, h, CHUNK * h), F32),
                   jax.ShapeDtypeStruct((2, g, 1, p), F32)],
        compiler_params=_cparams(1),
        name="s5prep",
    )(a_re.reshape(g, 1, p), a_im.reshape(g, 1, p), log_dt.reshape(g, 1, 1),
      b_re.transpose(0, 2, 1), b_im.transpose(0, 2, 1), c_re, c_im, d_skip.reshape(g, 1, h))


def _group_mask(shape, row_div, col_div):
    return (lax.broadcasted_iota(jnp.int32, shape, 0) // row_div ==
            lax.broadcasted_iota(jnp.int32, shape, 1) // col_div)


def _s5in_kernel(u_ref, bl_ref, o_ref, w_ref):
    p, wid = bl_ref.shape[3], o_ref.shape[3]
    h = LANE_TILE // (wid // p)
    mask = _group_mask((LANE_TILE, wid), h, p)
    for s in range(CHUNK):
        ex = jnp.concatenate([bl_ref[CHUNK - 1 - s, 0]] * (wid // p), axis=1)
        w_ref[s * LANE_TILE:(s + 1) * LANE_TILE, :] = jnp.where(mask, ex, 0.0).astype(BF16)
    for b in range(u_ref.shape[0]):
        ucat = jnp.concatenate([u_ref[b, s, 0] for s in range(CHUNK)], axis=1)
        o_ref[b, 0] = jnp.dot(ucat, w_ref[...], preferred_element_type=F32)


def s5_chunk_inputs(u_tm, bl2, h):
    bsz, _, ngt, nc, _ = u_tm.shape
    p = bl2.shape[3]
    wid = (LANE_TILE // h) * p
    return pl.pallas_call(
        _s5in_kernel,
        grid=(ngt, 2),
        in_specs=[pl.BlockSpec((bsz, CHUNK, 1, nc, LANE_TILE), lambda gt, ri: (0, 0, gt, 0, 0)),
                  pl.BlockSpec((CHUNK, 1, LANE_TILE, p), lambda gt, ri: (0, ri, gt, 0))],
        out_specs=pl.BlockSpec((bsz, 1, nc, wid), lambda gt, ri: (0, ri, 0, gt)),
        out_shape=jax.ShapeDtypeStruct((bsz, 2, nc, ngt * wid), F32),
        scratch_shapes=[pltpu.VMEM((CHUNK * LANE_TILE, wid), BF16)],
        compiler_params=_cparams(2),
        name="s5in",
    )(u_tm, bl2)


def _chunk_scan(s_ref, a_ref, x_ref, b):
    nc, w = s_ref.shape[2], s_ref.shape[3]
    a1r, a1i = a_ref[0], a_ref[1]

    def cmul(xr, xi, yr, yi):
        return xr * yr - xi * yi, xr * yi + xi * yr

    n = SUBLANES
    row = lax.broadcasted_iota(jnp.int32, (n, w), 0)
    pows = [(a1r, a1i)]
    for _ in range(n - 1):
        pows.append(cmul(pows[-1][0], pows[-1][1], a1r, a1i))
    steps = []
    for k in (1 << i for i in range(n.bit_length() - 1)):
        steps.append((k, jnp.where(row >= k, pows[k - 1][0], 0.0), jnp.where(row >= k, pows[k - 1][1], 0.0)))
    pcr = jnp.zeros((n, w), F32)
    pci = jnp.zeros((n, w), F32)
    for r in range(n):
        pcr = jnp.where(row == r, pows[r][0], pcr)
        pci = jnp.where(row == r, pows[r][1], pci)

    def body(blk, carry):
        cr, ci = carry
        off = pl.multiple_of(blk * n, n)
        xr = s_ref[b, 0, pl.ds(off, n), :]
        xi = s_ref[b, 1, pl.ds(off, n), :]
        for k, kr, ki in steps:
            sr, si = pltpu.roll(xr, k, 0), pltpu.roll(xi, k, 0)
            dr, di = cmul(sr, si, kr, ki)
            xr, xi = xr + dr, xi + di
        dr, di = cmul(jnp.broadcast_to(cr, (n, w)), jnp.broadcast_to(ci, (n, w)), pcr, pci)
        xr, xi = xr + dr, xi + di
        x_ref[b, 0, pl.ds(off, n), :] = jnp.where(row == 0, cr, pltpu.roll(xr, 1, 0))
        x_ref[b, 1, pl.ds(off, n), :] = jnp.where(row == 0, ci, pltpu.roll(xi, 1, 0))
        return xr[n - 1:n, :], xi[n - 1:n, :]

    zero = jnp.zeros((1, w), F32)
    lax.fori_loop(0, nc // n, body, (zero, zero))


def _s5out_kernel(u_ref, kx_ref, cl_ref, s_ref, a_ref, e_ref, o_ref, tr_ref, x_ref):
    bsz, _, _, nc, _ = u_ref.shape
    p, wid = cl_ref.shape[3], x_ref.shape[3]
    h = LANE_TILE // CHUNK
    mask_t = _group_mask((LANE_TILE, LANE_TILE), h, h)
    mask_c = _group_mask((LANE_TILE, wid), h, p)
    nt = (((1,), (1,)), ((), ()))
    for b in range(bsz):
        _chunk_scan(s_ref, a_ref, x_ref, b)
    kx = kx_ref[...].astype(BF16)
    for j in range(CHUNK):
        tj = jnp.where(mask_t, jnp.dot(kx, e_ref[j], preferred_element_type=F32), 0.0)
        tr_ref[(CHUNK - 1 - j) * LANE_TILE:(CHUNK - j) * LANE_TILE, :] = tj.astype(BF16)
    ucat = jnp.concatenate(
        [jnp.concatenate([u_ref[b, s, 0] for s in range(CHUNK)], axis=1) for b in range(bsz)], axis=0)
    xr = x_ref[:, 0].reshape(bsz * nc, wid).astype(BF16)
    xi = x_ref[:, 1].reshape(bsz * nc, wid).astype(BF16)
    for t in range(CHUNK):
        wr = jnp.concatenate([cl_ref[t, 0]] * (wid // p), axis=1)
        wi = jnp.concatenate([cl_ref[t, 1]] * (wid // p), axis=1)
        wr = jnp.where(mask_c, wr, 0.0).astype(BF16)
        wi = jnp.where(mask_c, wi, 0.0).astype(BF16)
        y = (jnp.dot(ucat[:, :(t + 1) * LANE_TILE], tr_ref[(CHUNK - 1 - t) * LANE_TILE:, :],
                     preferred_element_type=F32)
             + lax.dot_general(xr, wr, nt, preferred_element_type=F32)
             + lax.dot_general(xi, wi, nt, preferred_element_type=F32))
        for b in range(bsz):
            o_ref[b, t, 0] = y[b * nc:(b + 1) * nc].astype(BF16)


def s5_outputs(u_tm, kx2, cl2, s, a):
    bsz, _, ngt, nc, _ = u_tm.shape
    wid = s.shape[3] // ngt
    p = cl2.shape[3]
    once = pl.Buffered(1)
    h = LANE_TILE // CHUNK
    r = jnp.arange(LANE_TILE)
    e = ((r[None, :, None] // h == jnp.arange(CHUNK)[:, None, None])
         & (r[None, :, None] % h == r[None, None, :] % h)).astype(BF16)
    tile = pl.BlockSpec((bsz, CHUNK, 1, nc, LANE_TILE), lambda gt: (0, 0, gt, 0, 0))
    return pl.pallas_call(
        _s5out_kernel,
        grid=(ngt,),
        in_specs=[tile,
                  pl.BlockSpec((LANE_TILE, LANE_TILE), lambda gt: (gt, 0)),
                  pl.BlockSpec((CHUNK, 2, LANE_TILE, p), lambda gt: (0, 0, gt, 0)),
                  pl.BlockSpec((bsz, 2, nc, wid), lambda gt: (0, 0, 0, gt)),
                  pl.BlockSpec((2, 1, wid), lambda gt: (0, 0, gt)),
                  pl.BlockSpec((CHUNK, LANE_TILE, LANE_TILE), lambda gt: (0, 0, 0), pipeline_mode=once)],
        out_specs=tile,
        out_shape=jax.ShapeDtypeStruct(u_tm.shape, BF16),
        scratch_shapes=[pltpu.VMEM((CHUNK * LANE_TILE, LANE_TILE), BF16),
                        pltpu.VMEM((bsz, 2, nc, wid), F32)],
        compiler_params=_cparams(1),
        name="s5out",
    )(u_tm, kx2, cl2, s, a, e)


def _attn_kernel(q_ref, k_ref, v_ref, tri_ref, *rest, blk, nh, unroll, ncast):
    src_refs, o_ref, dst_refs, vcat_ref = rest[:ncast], rest[ncast], rest[ncast + 1:2 * ncast + 1], rest[-1]
    for src, dst in zip(src_refs, dst_refs):
        dst[...] = src[...].astype(BF16)
    qi = pl.program_id(2)
    q = q_ref[0]
    lane = lax.broadcasted_iota(jnp.int32, q.shape, 1)
    heads = [(lane >= HEAD_DIM * h) & (lane < HEAD_DIM * (h + 1)) for h in range(nh)]
    zero = jnp.zeros_like(q)

    @pl.when(qi == 0)
    def _():
        def fill(j, _):
            vb = v_ref[0, pl.ds(pl.multiple_of(j * blk, blk), blk), :]
            vcat_ref[j] = jnp.concatenate([jnp.where(m, vb, zero) for m in heads], axis=0)
            return 0

        lax.fori_loop(0, vcat_ref.shape[0], fill, 0)

    qs = jnp.concatenate([jnp.where(m, q, zero) for m in heads], axis=0)
    row = lax.broadcasted_iota(jnp.int32, (blk, blk), 0)
    col = lax.broadcasted_iota(jnp.int32, (blk, blk), 1)
    past = jnp.concatenate([col < row] * nh, axis=0)
    nt = (((1,), (1,)), ((), ()))

    def block(kj, carry, acc, diag):
        kb = k_ref[0, pl.ds(pl.multiple_of(kj * blk, blk), blk), :]
        z = lax.dot_general(qs, kb, nt, preferred_element_type=F32)
        sp = jnp.maximum(z, 0.0) + jnp.log2(1.0 + jnp.exp2(-jnp.abs(z)))
        if diag:
            sp = jnp.where(past, sp, 0.0)
        cs = lax.dot_general(sp, tri_ref[...], (((1,), (0,)), ((), ())), preferred_element_type=F32)
        w = jnp.exp2(((z - sp) - cs) - carry)
        if diag:
            w = jnp.where(past, w, 0.0)
        wcat = jnp.concatenate([w[h * blk:(h + 1) * blk] for h in range(nh)], axis=1)
        acc = acc + lax.dot_general(wcat, vcat_ref[kj], (((1,), (0,)), ((), ())), preferred_element_type=F32)
        return carry + (cs[:, 0:1] + sp[:, 0:1]), acc

    def head(r):
        def run():
            c = block(qi, jnp.zeros((nh * blk, 1), F32), jnp.zeros(q.shape, F32), True)
            for i in range(r):
                c = block(qi - 1 - i, c[0], c[1], False)
            return c
        return run

    rem = qi % unroll
    c = lax.switch(rem, [head(r) for r in range(unroll)])

    left = qi - rem
    for n in (2 * unroll, unroll):
        def group(i, c, n=n, left=left):
            for r in range(n):
                c = block(left - 1 - r - n * i, c[0], c[1], False)
            return c

        c = lax.fori_loop(0, left // n, group, c)
        left = left % n
    o_ref[0] = c[1].astype(BF16)


def stick_attention(qkv3, width, cast=(), blk=256, nh=4, unroll=4):
    bsz, seqlen, _ = qkv3.shape
    wl = nh * HEAD_DIM
    assert wl == LANE_TILE and width % wl == 0 and seqlen % blk == 0
    ngrp = width // wl
    nq = seqlen // blk
    ii = jnp.arange(blk)
    tri = (ii[:, None] > ii[None, :]).astype(BF16)
    nsteps = bsz * ngrp * nq

    def slab_spec(w):
        rows, cols = w.shape
        for c in range(1, cols // LANES + 1):
            r = nsteps // c
            if cols % (LANES * c) == 0 and nsteps % c == 0 and rows % (BF16_ROWS * r) == 0:
                return pl.BlockSpec((rows // r, cols // c),
                                    lambda b, h, i, c=c: (((b * ngrp + h) * nq + i) // c, ((b * ngrp + h) * nq + i) % c))
        raise ValueError(f"cannot cut {w.shape} into {nsteps} blocks")

    slab_specs = [slab_spec(w) for w in cast]
    outs = pl.pallas_call(
        functools.partial(_attn_kernel, blk=blk, nh=nh, unroll=unroll, ncast=len(cast)),
        grid=(bsz, ngrp, nq),
        in_specs=[pl.BlockSpec((1, blk, wl), lambda b, h, i: (b, i, h)),
                  pl.BlockSpec((1, seqlen, wl), lambda b, h, i: (b, 0, ngrp + h)),
                  pl.BlockSpec((1, seqlen, wl), lambda b, h, i: (b, 0, 2 * ngrp + h)),
                  pl.BlockSpec((blk, blk), lambda b, h, i: (0, 0))] + slab_specs,
        out_specs=[pl.BlockSpec((1, blk, wl), lambda b, h, i: (b, i, h))] + slab_specs,
        out_shape=[jax.ShapeDtypeStruct((bsz, seqlen, width), BF16)]
                  + [jax.ShapeDtypeStruct(w.shape, BF16) for w in cast],
        scratch_shapes=[pltpu.VMEM((nq, nh * blk, wl), BF16)],
        compiler_params=_cparams(3),
        name="attn",
    )(qkv3, qkv3, qkv3, tri, *cast)
    return outs[0], list(outs[1:])


def _out_kernel(y_ref, a_ref, x_ref, mod_ref, wglu_ref, gs_ref, ga_ref, w0_ref, w1_ref, gf_ref,
                h_ref, xf_ref, ys_ref, *, rows):
    cpr = rows // CHUNK
    per = LANE_TILE // LANES
    for r0 in range(0, a_ref.shape[0], rows):
        rs = slice(r0, r0 + rows)
        c0 = r0 // CHUNK
        nk = ys_ref.shape[0]
        for s in range(CHUNK):
            for k in range(nk):
                ys_ref[k, pl.ds(s, cpr, stride=CHUNK), :] = (
                    y_ref[0, s, k // per, c0:c0 + cpr, (k % per) * LANES:(k % per + 1) * LANES].astype(F32))
        y = jax.nn.gelu(jnp.concatenate([ys_ref[k] for k in range(nk)], axis=1))
        z = jnp.dot(y.astype(BF16), wglu_ref[...], preferred_element_type=F32)
        ys = y * jax.nn.sigmoid(z)
        ysn = (_rms(ys) * gs_ref[...]).astype(BF16)
        an = (_rms(a_ref[rs, :].astype(F32)) * ga_ref[...]).astype(BF16)
        o = (jnp.dot(ysn, w0_ref[...], preferred_element_type=F32)
             + jnp.dot(an, w1_ref[...], preferred_element_type=F32))
        h = x_ref[rs, :] + mod_ref[0, 2:3, :] * o
        h_ref[rs, :] = h
        xn = _rms(h) * gf_ref[...]
        xf_ref[rs, :] = (xn * (1.0 + mod_ref[0, 4:5, :]) + mod_ref[0, 3:4, :]).astype(BF16)


def out_proj(y_tm, a2, x2, mod3, w_glu_bf, g_ssm, g_attn, w_out_bf, g_ffn, seqlen, tm=512, rows=256):
    t, d = x2.shape
    ngt = y_tm.shape[2]
    ws = ngt * LANE_TILE
    wa = a2.shape[1]
    assert seqlen % tm == 0 and tm % rows == 0 and rows % (CHUNK * BF16_ROWS) == 0 and ws == wa
    tpb = seqlen // tm
    const = lambda shape, r=0: pl.BlockSpec(shape, lambda i: (r, 0), pipeline_mode=pl.Buffered(1))
    return pl.pallas_call(
        functools.partial(_out_kernel, rows=rows),
        grid=(t // tm,),
        in_specs=[pl.BlockSpec((1, CHUNK, ngt, tm // CHUNK, LANE_TILE), lambda i: (i // tpb, 0, 0, i % tpb, 0)),
                  pl.BlockSpec((tm, wa), lambda i: (i, 0)),
                  pl.BlockSpec((tm, d), lambda i: (i, 0)),
                  pl.BlockSpec((1, 6, d), lambda i: (i // tpb, 0, 0)),
                  const((ws, ws)), const((1, ws)), const((1, wa)),
                  const((ws, d)), const((wa, d), 1), const((1, d))],
        out_specs=[pl.BlockSpec((tm, d), lambda i: (i, 0)),
                   pl.BlockSpec((tm, d), lambda i: (i, 0))],
        out_shape=[jax.ShapeDtypeStruct((t, d), F32), jax.ShapeDtypeStruct((t, d), BF16)],
        scratch_shapes=[pltpu.VMEM((ws // LANES, rows, LANES), F32)],
        compiler_params=_cparams(1),
        name="outproj",
    )(y_tm, a2, x2, mod3, w_glu_bf, g_ssm.reshape(1, ws), g_attn.reshape(1, wa), w_out_bf, w_out_bf,
      g_ffn.reshape(1, d))


def _ffn_kernel(xf_ref, h_ref, mod_ref, wg_ref, wu_ref, wd_ref, o_ref):
    j = pl.program_id(1)

    def partial_out():
        xf = xf_ref[...]
        g = jnp.dot(xf, wg_ref[...], preferred_element_type=F32)
        u = jnp.dot(xf, wu_ref[...], preferred_element_type=F32)
        hm = (jax.nn.silu(g) * u).astype(BF16)
        return jnp.dot(hm, wd_ref[...], preferred_element_type=F32)

    @pl.when(j == 0)
    def _():
        o_ref[...] = partial_out()

    @pl.when(j > 0)
    def _():
        o_ref[...] += partial_out()

    @pl.when(j == pl.num_programs(1) - 1)
    def _():
        o_ref[...] = h_ref[...] + mod_ref[0, 5:6, :] * o_ref[...]


def ffn(xf2, h2, mod3, w_gate, w_up, w_down, seqlen, tm=1024, tf=512):
    t, d = h2.shape
    dff = w_gate.shape[1]
    assert seqlen % tm == 0 and dff % tf == 0
    return pl.pallas_call(
        _ffn_kernel,
        grid=(t // tm, dff // tf),
        in_specs=[pl.BlockSpec((tm, d), lambda i, j: (i, 0)),
                  pl.BlockSpec((tm, d), lambda i, j: (i, 0), pipeline_mode=pl.Buffered(1)),
                  pl.BlockSpec((1, 6, d), lambda i, j: ((i * tm) // seqlen, 0, 0)),
                  pl.BlockSpec((d, tf), lambda i, j: (0, j)),
                  pl.BlockSpec((d, tf), lambda i, j: (0, j)),
                  pl.BlockSpec((tf, d), lambda i, j: (j, 0))],
        out_specs=pl.BlockSpec((tm, d), lambda i, j: (i, 0)),
        out_shape=jax.ShapeDtypeStruct((t, d), F32),
        compiler_params=_cparams(2),
        name="ffn",
    )(xf2, h2, mod3, w_gate, w_up, w_down)


def s5_mixer_chunked(u_tm, a_re, a_im, log_dt, b_re, b_im, c_re, c_im, d_skip):
    bsz, _, ngt, nc, _ = u_tm.shape
    g, p = a_re.shape
    h = c_re.shape[1]
    assert h * CHUNK == LANE_TILE and g * h == ngt * LANE_TILE
    bl, cl, kx, a = s5_prep(a_re, a_im, log_dt, b_re, b_im, c_re, c_im, d_skip)
    s = s5_chunk_inputs(u_tm, bl.reshape(CHUNK, 2, g * h, p), h)
    return s5_outputs(u_tm, kx.reshape(g * h, CHUNK * h), cl.reshape(CHUNK, 2, g * h, p), s,
                      a.reshape(2, 1, g * p))


def kernel(x, c, w_ada, b_ada, g_mix, w_in, a_re, a_im, log_dt, b_re, b_im, c_re, c_im, d_skip,
           w_glu, q_gain, k_gain, g_ssm_out, g_attn_out, w_out, g_ffn, w_gate, w_up, w_down):
    bsz, seqlen, d = x.shape
    depth = w_ada.shape[0]
    width = w_glu.shape[1]
    h2 = x.reshape(bsz * seqlen, d)
    for l in range(depth):
        mod3 = ada_mod(c, w_ada[l], b_ada[l]).reshape(bsz, 6, d)
        u_tm, qkv = in_proj(h2, mod3, g_mix[l], w_in[l].astype(BF16), q_gain[l], k_gain[l], bsz, seqlen)
        y_tm = s5_mixer_chunked(u_tm, a_re[l], a_im[l], log_dt[l], b_re[l], b_im[l],
                                c_re[l], c_im[l], d_skip[l])
        att, (wglu_bf, wout_bf, wg_bf, wu_bf, wd_bf) = stick_attention(
            qkv.reshape(bsz, seqlen, 3 * width), width,
            cast=(w_glu[l], w_out[l], w_gate[l], w_up[l], w_down[l]))
        h2, xf = out_proj(y_tm, att.reshape(bsz * seqlen, width), h2, mod3,
                          wglu_bf, g_ssm_out[l], g_attn_out[l], wout_bf, g_ffn[l], seqlen)
        h2 = ffn(xf, h2, mod3, wg_bf, wu_bf, wd_bf, seqlen)
    return h2.reshape(bsz, seqlen, d)
```

```python
import functools

import jax
import jax.numpy as jnp
from jax import lax
from jax.experimental import pallas as pl
from jax.experimental.pallas import tpu as pltpu

F32 = jnp.float32
BF16 = jnp.bfloat16
EPS = 1e-6
HEAD_DIM = 64
CHUNK = 16
LANES = 128
SUBLANES = 8
BF16_ROWS = 16
LANE_TILE = 256
VMEM_LIMIT = 56 * 1024 * 1024
Q_SCALE = HEAD_DIM ** -0.5 * 1.4426950408889634


def _cparams(n_axes, vmem=VMEM_LIMIT):
    return pltpu.CompilerParams(dimension_semantics=("arbitrary",) * n_axes,
                                vmem_limit_bytes=vmem)


def _rms(x):
    return x * lax.rsqrt(jnp.mean(x * x, axis=-1, keepdims=True) + EPS)


def _ada_kernel(c_ref, w_ref, b_ref, o_ref):
    d, tn = w_ref.shape
    rows = LANE_TILE
    for b in range(c_ref.shape[0]):
        acc = jnp.zeros((SUBLANES, tn), F32)
        for kc in range(d // rows):
            cond = jax.nn.silu(c_ref[b, kc * rows:(kc + 1) * rows, :])
            blk = cond * w_ref[kc * rows:(kc + 1) * rows, :]
            acc = acc + jnp.sum(blk.reshape(rows // SUBLANES, SUBLANES, tn), axis=0)
        o_ref[b:b + 1, :] = jnp.sum(acc, axis=0, keepdims=True) + b_ref[...]


def ada_mod(c, w_ada, b_ada, tn=1024):
    bsz, d = c.shape
    n = w_ada.shape[1]
    return pl.pallas_call(
        _ada_kernel,
        grid=(n // tn,),
        in_specs=[pl.BlockSpec((bsz, d, 1), lambda j: (0, 0, 0)),
                  pl.BlockSpec((d, tn), lambda j: (0, j)),
                  pl.BlockSpec((1, tn), lambda j: (0, j))],
        out_specs=pl.BlockSpec((bsz, tn), lambda j: (0, j)),
        out_shape=jax.ShapeDtypeStruct((bsz, n), F32),
        compiler_params=_cparams(1),
        name="ada",
    )(c.reshape(bsz, d, 1), w_ada, b_ada.reshape(1, n))


def _head_rmsnorm(r, ones_ref):
    outs = []
    for s in range(r.shape[1] // LANE_TILE):
        rs = r[:, s * LANE_TILE:(s + 1) * LANE_TILE]
        ss = jnp.dot((rs * rs).astype(BF16), ones_ref[...], preferred_element_type=F32)
        outs.append(rs * lax.rsqrt(ss * (1.0 / HEAD_DIM) + EPS))
    return jnp.concatenate(outs, axis=1)


def _proj_kernel(x_ref, mod_ref, g_ref, w_ref, qg_ref, kg_ref, ones_ref, u_ref, o_ref, us_ref, *, rows):
    tm = x_ref.shape[0]
    tn = w_ref.shape[1] // 4
    cpr = rows // CHUNK
    for r0 in range(0, tm, rows):
        xn = _rms(x_ref[r0:r0 + rows, :]) * g_ref[...]
        xm = (xn * (1.0 + mod_ref[0, 1:2, :]) + mod_ref[0, 0:1, :]).astype(BF16)
        for j in range(4):
            r = jnp.dot(xm, w_ref[:, j * tn:(j + 1) * tn], preferred_element_type=F32)
            if j == 0:
                c0 = r0 // CHUNK
                per = LANE_TILE // LANES
                for k in range(tn // LANES):
                    us_ref[k] = r[:, k * LANES:(k + 1) * LANES]
                for s in range(CHUNK):
                    for k in range(tn // LANES):
                        us = us_ref[k, pl.ds(s, cpr, stride=CHUNK), :]
                        u_ref[0, s, k // per, c0:c0 + cpr, (k % per) * LANES:(k % per + 1) * LANES] = us.astype(BF16)
                continue
            if j == 1:
                r = (_head_rmsnorm(r, ones_ref) * qg_ref[...]) * Q_SCALE
            elif j == 2:
                r = _head_rmsnorm(r, ones_ref) * kg_ref[...]
            o_ref[r0:r0 + rows, (j - 1) * tn:j * tn] = r.astype(BF16)


def in_proj(x2, mod3, g_mix, w_in_bf, q_gain, k_gain, bsz, seqlen, tm=512, rows=256):
    t, d = x2.shape
    n = w_in_bf.shape[1]
    tn = n // 4
    assert seqlen % tm == 0 and tm % rows == 0 and rows % (CHUNK * BF16_ROWS) == 0 and tn % LANE_TILE == 0
    reps = tn // HEAD_DIM
    qg = jnp.tile(q_gain.reshape(1, HEAD_DIM), (1, reps))
    kg = jnp.tile(k_gain.reshape(1, HEAD_DIM), (1, reps))
    idx = jnp.arange(LANE_TILE) // HEAD_DIM
    ones = (idx[:, None] == idx[None, :]).astype(BF16)
    const = lambda shape: pl.BlockSpec(shape, lambda i: (0, 0), pipeline_mode=pl.Buffered(1))
    tpb = seqlen // tm
    ngt = tn // LANE_TILE
    return pl.pallas_call(
        functools.partial(_proj_kernel, rows=rows),
        grid=(t // tm,),
        in_specs=[pl.BlockSpec((tm, d), lambda i: (i, 0)),
                  pl.BlockSpec((1, 6, d), lambda i: (i // tpb, 0, 0)),
                  const((1, d)), const((d, n)), const((1, tn)), const((1, tn)),
                  const((LANE_TILE, LANE_TILE))],
        out_specs=[pl.BlockSpec((1, CHUNK, ngt, tm // CHUNK, LANE_TILE), lambda i: (i // tpb, 0, 0, i % tpb, 0)),
                   pl.BlockSpec((tm, 3 * tn), lambda i: (i, 0))],
        out_shape=[jax.ShapeDtypeStruct((bsz, CHUNK, ngt, seqlen // CHUNK, LANE_TILE), BF16),
                   jax.ShapeDtypeStruct((t, 3 * tn), BF16)],
        scratch_shapes=[pltpu.VMEM((tn // LANES, rows, LANES), F32)],
        compiler_params=_cparams(1),
        name="proj",
    )(x2, mod3, g_mix.reshape(1, d), w_in_bf, qg, kg, ones)


def _zoh(ar, ai, dt):
    mag = jnp.exp(ar * dt)
    th = ai * dt
    lbr, lbi = mag * jnp.cos(th), mag * jnp.sin(th)
    den = ar * ar + ai * ai
    nr, ni = lbr - 1.0, lbi
    return lbr, lbi, (nr * ar + ni * ai) / den, (ni * ar - nr * ai) / den


def _s5prep_kernel(ar_ref, ai_ref, ldt_ref, btr_ref, bti_ref, cr_ref, ci_ref, d_ref,
                   bl_ref, cl_ref, kx_ref, a_ref):
    dt = jnp.exp(ldt_ref[...])
    lbr, lbi, kr, ki = _zoh(ar_ref[...], ai_ref[...], dt)
    btr, bti = btr_ref[...], bti_ref[...]
    bbr = kr * btr - ki * bti
    bbi = kr * bti + ki * btr
    c_re, c_im = cr_ref[...], ci_ref[...]
    h = c_re.shape[1]
    eye = (lax.broadcasted_iota(jnp.int32, (1, h, h), 1) ==
           lax.broadcasted_iota(jnp.int32, (1, h, h), 2))
    dn = (((2,), (2,)), ((0,), (0,)))
    pr, pi = jnp.ones_like(lbr), jnp.zeros_like(lbr)
    blrs, blis = [], []
    for j in range(CHUNK):
        blr = pr * bbr - pi * bbi
        bli = pr * bbi + pi * bbr
        bl_ref[j, 0] = blr
        bl_ref[j, 1] = bli
        blrs.append(blr)
        blis.append(bli)
        pr, pi = pr * lbr - pi * lbi, pr * lbi + pi * lbr
        cl_ref[j, 0] = c_re * pr - c_im * pi
        cl_ref[j, 1] = -(c_re * pi + c_im * pr)
    a_ref[0] = pr
    a_ref[1] = pi
    kx = (lax.dot_general(jnp.concatenate(blrs, axis=1), c_re, dn, precision=lax.Precision.HIGHEST,
                          preferred_element_type=F32)
          - lax.dot_general(jnp.concatenate(blis, axis=1), c_im, dn, precision=lax.Precision.HIGHEST,
                            preferred_element_type=F32))
    for j in range(CHUNK):
        kxj = kx[:, j * h:(j + 1) * h, :]
        if j == 0:
            kxj = kxj + jnp.where(eye, d_ref[...], 0.0)
        kx_ref[:, :, j * h:(j + 1) * h] = kxj


def s5_prep(a_re, a_im, log_dt, b_re, b_im, c_re, c_im, d_skip, gb=8):
    g, p = a_re.shape
    h = c_re.shape[1]
    vec = pl.BlockSpec((gb, 1, p), lambda i: (i, 0, 0))
    mat = pl.BlockSpec((gb, h, p), lambda i: (i, 0, 0))
    lag = pl.BlockSpec((CHUNK, 2, gb, h, p), lambda i: (0, 0, i, 0, 0))
    lag_shape = jax.ShapeDtypeStruct((CHUNK, 2, g, h, p), F32)
    return pl.pallas_call(
        _s5prep_kernel,
        grid=(g // gb,),
        in_specs=[vec, vec, pl.BlockSpec((gb, 1, 1), lambda i: (i, 0, 0)),
                  mat, mat, mat, mat, pl.BlockSpec((gb, 1, h), lambda i: (i, 0, 0))],
        out_specs=[lag, lag,
                   pl.BlockSpec((gb, h, CHUNK * h), lambda i: (i, 0, 0)),
                   pl.BlockSpec((2, gb, 1, p), lambda i: (0, i, 0, 0))],
        out_shape=[lag_shape, lag_shape,
                   jax.ShapeDtypeStruct((g, h, CHUNK * h), F32),
                   jax.ShapeDtypeStruct((2, g, 1, p), F32)],
        compiler_params=_cparams(1),
        name="s5prep",
    )(a_re.reshape(g, 1, p), a_im.reshape(g, 1, p), log_dt.reshape(g, 1, 1),
      b_re.transpose(0, 2, 1), b_im.transpose(0, 2, 1), c_re, c_im, d_skip.reshape(g, 1, h))


def _group_mask(shape, row_div, col_div):
    return (lax.broadcasted_iota(jnp.int32, shape, 0) // row_div ==
            lax.broadcasted_iota(jnp.int32, shape, 1) // col_div)


def _s5in_kernel(u_ref, bl_ref, o_ref, w_ref):
    p, wid = bl_ref.shape[3], o_ref.shape[3]
    h = LANE_TILE // (wid // p)
    mask = _group_mask((LANE_TILE, wid), h, p)
    for s in range(CHUNK):
        ex = jnp.concatenate([bl_ref[CHUNK - 1 - s, 0]] * (wid // p), axis=1)
        w_ref[s * LANE_TILE:(s + 1) * LANE_TILE, :] = jnp.where(mask, ex, 0.0).astype(BF16)
    for b in range(u_ref.shape[0]):
        ucat = jnp.concatenate([u_ref[b, s, 0] for s in range(CHUNK)], axis=1)
        o_ref[b, 0] = jnp.dot(ucat, w_ref[...], preferred_element_type=F32)


def s5_chunk_inputs(u_tm, bl2, h):
    bsz, _, ngt, nc, _ = u_tm.shape
    p = bl2.shape[3]
    wid = (LANE_TILE // h) * p
    return pl.pallas_call(
        _s5in_kernel,
        grid=(ngt, 2),
        in_specs=[pl.BlockSpec((bsz, CHUNK, 1, nc, LANE_TILE), lambda gt, ri: (0, 0, gt, 0, 0)),
                  pl.BlockSpec((CHUNK, 1, LANE_TILE, p), lambda gt, ri: (0, ri, gt, 0))],
        out_specs=pl.BlockSpec((bsz, 1, nc, wid), lambda gt, ri: (0, ri, 0, gt)),
        out_shape=jax.ShapeDtypeStruct((bsz, 2, nc, ngt * wid), F32),
        scratch_shapes=[pltpu.VMEM((CHUNK * LANE_TILE, wid), BF16)],
        compiler_params=_cparams(2),
        name="s5in",
    )(u_tm, bl2)


def _chunk_scan(s_ref, a_ref, x_ref, b):
    nc, w = s_ref.shape[2], s_ref.shape[3]
    a1r, a1i = a_ref[0], a_ref[1]

    def cmul(xr, xi, yr, yi):
        return xr * yr - xi * yi, xr * yi + xi * yr

    n = SUBLANES
    row = lax.broadcasted_iota(jnp.int32, (n, w), 0)
    pows = [(a1r, a1i)]
    for _ in range(n - 1):
        pows.append(cmul(pows[-1][0], pows[-1][1], a1r, a1i))
    steps = []
    for k in (1 << i for i in range(n.bit_length() - 1)):
        steps.append((k, jnp.where(row >= k, pows[k - 1][0], 0.0), jnp.where(row >= k, pows[k - 1][1], 0.0)))
    pcr = jnp.zeros((n, w), F32)
    pci = jnp.zeros((n, w), F32)
    for r in range(n):
        pcr = jnp.where(row == r, pows[r][0], pcr)
        pci = jnp.where(row == r, pows[r][1], pci)

    def body(blk, carry):
        cr, ci = carry
        off = pl.multiple_of(blk * n, n)
        xr = s_ref[b, 0, pl.ds(off, n), :]
        xi = s_ref[b, 1, pl.ds(off, n), :]
        for k, kr, ki in steps:
            sr, si = pltpu.roll(xr, k, 0), pltpu.roll(xi, k, 0)
            dr, di = cmul(sr, si, kr, ki)
            xr, xi = xr + dr, xi + di
        dr, di = cmul(jnp.broadcast_to(cr, (n, w)), jnp.broadcast_to(ci, (n, w)), pcr, pci)
        xr, xi = xr + dr, xi + di
        x_ref[b, 0, pl.ds(off, n), :] = jnp.where(row == 0, cr, pltpu.roll(xr, 1, 0))
        x_ref[b, 1, pl.ds(off, n), :] = jnp.where(row == 0, ci, pltpu.roll(xi, 1, 0))
        return xr[n - 1:n, :], xi[n - 1:n, :]

    zero = jnp.zeros((1, w), F32)
    lax.fori_loop(0, nc // n, body, (zero, zero))


def _s5out_kernel(u_ref, kx_ref, cl_ref, s_ref, a_ref, e_ref, o_ref, tr_ref, x_ref):
    bsz, _, _, nc, _ = u_ref.shape
    p, wid = cl_ref.shape[3], x_ref.shape[3]
    h = LANE_TILE // CHUNK
    mask_t = _group_mask((LANE_TILE, LANE_TILE), h, h)
    mask_c = _group_mask((LANE_TILE, wid), h, p)
    nt = (((1,), (1,)), ((), ()))
    for b in range(bsz):
        _chunk_scan(s_ref, a_ref, x_ref, b)
    kx = kx_ref[...].astype(BF16)
    for j in range(CHUNK):
        tj = jnp.where(mask_t, jnp.dot(kx, e_ref[j], preferred_element_type=F32), 0.0)
        tr_ref[(CHUNK - 1 - j) * LANE_TILE:(CHUNK - j) * LANE_TILE, :] = tj.astype(BF16)
    ucat = jnp.concatenate(
        [jnp.concatenate([u_ref[b, s, 0] for s in range(CHUNK)], axis=1) for b in range(bsz)], axis=0)
    xr = x_ref[:, 0].reshape(bsz * nc, wid).astype(BF16)
    xi = x_ref[:, 1].reshape(bsz * nc, wid).astype(BF16)
    for t in range(CHUNK):
        wr = jnp.concatenate([cl_ref[t, 0]] * (wid // p), axis=1)
        wi = jnp.concatenate([cl_ref[t, 1]] * (wid // p), axis=1)
        wr = jnp.where(mask_c, wr, 0.0).astype(BF16)
        wi = jnp.where(mask_c, wi, 0.0).astype(BF16)
        y = (jnp.dot(ucat[:, :(t + 1) * LANE_TILE], tr_ref[(CHUNK - 1 - t) * LANE_TILE:, :],
                     preferred_element_type=F32)
             + lax.dot_general(xr, wr, nt, preferred_element_type=F32)
             + lax.dot_general(xi, wi, nt, preferred_element_type=F32))
        for b in range(bsz):
            o_ref[b, t, 0] = y[b * nc:(b + 1) * nc].astype(BF16)


def s5_outputs(u_tm, kx2, cl2, s, a):
    bsz, _, ngt, nc, _ = u_tm.shape
    wid = s.shape[3] // ngt
    p = cl2.shape[3]
    once = pl.Buffered(1)
    h = LANE_TILE // CHUNK
    r = jnp.arange(LANE_TILE)
    e = ((r[None, :, None] // h == jnp.arange(CHUNK)[:, None, None])
         & (r[None, :, None] % h == r[None, None, :] % h)).astype(BF16)
    tile = pl.BlockSpec((bsz, CHUNK, 1, nc, LANE_TILE), lambda gt: (0, 0, gt, 0, 0))
    return pl.pallas_call(
        _s5out_kernel,
        grid=(ngt,),
        in_specs=[tile,
                  pl.BlockSpec((LANE_TILE, LANE_TILE), lambda gt: (gt, 0)),
                  pl.BlockSpec((CHUNK, 2, LANE_TILE, p), lambda gt: (0, 0, gt, 0)),
                  pl.BlockSpec((bsz, 2, nc, wid), lambda gt: (0, 0, 0, gt)),
                  pl.BlockSpec((2, 1, wid), lambda gt: (0, 0, gt)),
                  pl.BlockSpec((CHUNK, LANE_TILE, LANE_TILE), lambda gt: (0, 0, 0), pipeline_mode=once)],
        out_specs=tile,
        out_shape=jax.ShapeDtypeStruct(u_tm.shape, BF16),
        scratch_shapes=[pltpu.VMEM((CHUNK * LANE_TILE, LANE_TILE), BF16),
                        pltpu.VMEM((bsz, 2, nc, wid), F32)],
        compiler_params=_cparams(1),
        name="s5out",
    )(u_tm, kx2, cl2, s, a, e)


def _attn_kernel(q_ref, k_ref, v_ref, tri_ref, *rest, blk, nh, unroll, ncast):
    src_refs, o_ref, dst_refs, vcat_ref = rest[:ncast], rest[ncast], rest[ncast + 1:2 * ncast + 1], rest[-1]
    for src, dst in zip(src_refs, dst_refs):
        dst[...] = src[...].astype(BF16)
    qi = pl.program_id(2)
    q = q_ref[0]
    lane = lax.broadcasted_iota(jnp.int32, q.shape, 1)
    heads = [(lane >= HEAD_DIM * h) & (lane < HEAD_DIM * (h + 1)) for h in range(nh)]
    zero = jnp.zeros_like(q)

    @pl.when(qi == 0)
    def _():
        def fill(j, _):
            vb = v_ref[0, pl.ds(pl.multiple_of(j * blk, blk), blk), :]
            vcat_ref[j] = jnp.concatenate([jnp.where(m, vb, zero) for m in heads], axis=0)
            return 0

        lax.fori_loop(0, vcat_ref.shape[0], fill, 0)

    qs = jnp.concatenate([jnp.where(m, q, zero) for m in heads], axis=0)
    row = lax.broadcasted_iota(jnp.int32, (blk, blk), 0)
    col = lax.broadcasted_iota(jnp.int32, (blk, blk), 1)
    past = jnp.concatenate([col < row] * nh, axis=0)
    nt = (((1,), (1,)), ((), ()))

    def block(kj, carry, acc, diag):
        kb = k_ref[0, pl.ds(pl.multiple_of(kj * blk, blk), blk), :]
        z = lax.dot_general(qs, kb, nt, preferred_element_type=F32)
        sp = jnp.maximum(z, 0.0) + jnp.log2(1.0 + jnp.exp2(-jnp.abs(z)))
        if diag:
            sp = jnp.where(past, sp, 0.0)
        cs = lax.dot_general(sp, tri_ref[...], (((1,), (0,)), ((), ())), preferred_element_type=F32)
        w = jnp.exp2(((z - sp) - cs) - carry)
        if diag:
            w = jnp.where(past, w, 0.0)
        wcat = jnp.concatenate([w[h * blk:(h + 1) * blk] for h in range(nh)], axis=1)
        acc = acc + lax.dot_general(wcat, vcat_ref[kj], (((1,), (0,)), ((), ())), preferred_element_type=F32)
        return carry + (cs[:, 0:1] + sp[:, 0:1]), acc

    def head(r):
        def run():
            c = block(qi, jnp.zeros((nh * blk, 1), F32), jnp.zeros(q.shape, F32), True)
            for i in range(r):
                c = block(qi - 1 - i, c[0], c[1], False)
            return c
        return run

    rem = qi % unroll
    c = lax.switch(rem, [head(r) for r in range(unroll)])

    left = qi - rem
    for n in (2 * unroll, unroll):
        def group(i, c, n=n, left=left):
            for r in range(n):
                c = block(left - 1 - r - n * i, c[0], c[1], False)
            return c

        c = lax.fori_loop(0, left // n, group, c)
        left = left % n
    o_ref[0] = c[1].astype(BF16)


def stick_attention(qkv3, width, cast=(), blk=256, nh=4, unroll=4):
    bsz, seqlen, _ = qkv3.shape
    wl = nh * HEAD_DIM
    assert wl == LANE_TILE and width % wl == 0 and seqlen % blk == 0
    ngrp = width // wl
    nq = seqlen // blk
    ii = jnp.arange(blk)
    tri = (ii[:, None] > ii[None, :]).astype(BF16)
    nsteps = bsz * ngrp * nq

    def slab_spec(w):
        rows, cols = w.shape
        for c in range(1, cols // LANES + 1):
            r = nsteps // c
            if cols % (LANES * c) == 0 and nsteps % c == 0 and rows % (BF16_ROWS * r) == 0:
                return pl.BlockSpec((rows // r, cols // c),
                                    lambda b, h, i, c=c: (((b * ngrp + h) * nq + i) // c, ((b * ngrp + h) * nq + i) % c))
        raise ValueError(f"cannot cut {w.shape} into {nsteps} blocks")

    slab_specs = [slab_spec(w) for w in cast]
    outs = pl.pallas_call(
        functools.partial(_attn_kernel, blk=blk, nh=nh, unroll=unroll, ncast=len(cast)),
        grid=(bsz, ngrp, nq),
        in_specs=[pl.BlockSpec((1, blk, wl), lambda b, h, i: (b, i, h)),
                  pl.BlockSpec((1, seqlen, wl), lambda b, h, i: (b, 0, ngrp + h)),
                  pl.BlockSpec((1, seqlen, wl), lambda b, h, i: (b, 0, 2 * ngrp + h)),
                  pl.BlockSpec((blk, blk), lambda b, h, i: (0, 0))] + slab_specs,
        out_specs=[pl.BlockSpec((1, blk, wl), lambda b, h, i: (b, i, h))] + slab_specs,
        out_shape=[jax.ShapeDtypeStruct((bsz, seqlen, width), BF16)]
                  + [jax.ShapeDtypeStruct(w.shape, BF16) for w in cast],
        scratch_shapes=[pltpu.VMEM((nq, nh * blk, wl), BF16)],
        compiler_params=_cparams(3),
        name="attn",
    )(qkv3, qkv3, qkv3, tri, *cast)
    return outs[0], list(outs[1:])


def _out_kernel(y_ref, a_ref, x_ref, mod_ref, wglu_ref, gs_ref, ga_ref, w0_ref, w1_ref, gf_ref,
                h_ref, xf_ref, ys_ref, *, rows):
    cpr = rows // CHUNK
    per = LANE_TILE // LANES
    for r0 in range(0, a_ref.shape[0], rows):
        rs = slice(r0, r0 + rows)
        c0 = r0 // CHUNK
        nk = ys_ref.shape[0]
        for s in range(CHUNK):
            for k in range(nk):
                ys_ref[k, pl.ds(s, cpr, stride=CHUNK), :] = (
                    y_ref[0, s, k // per, c0:c0 + cpr, (k % per) * LANES:(k % per + 1) * LANES].astype(F32))
        y = jax.nn.gelu(jnp.concatenate([ys_ref[k] for k in range(nk)], axis=1))
        z = jnp.dot(y.astype(BF16), wglu_ref[...], preferred_element_type=F32)
        ys = y * jax.nn.sigmoid(z)
        ysn = (_rms(ys) * gs_ref[...]).astype(BF16)
        an = (_rms(a_ref[rs, :].astype(F32)) * ga_ref[...]).astype(BF16)
        o = (jnp.dot(ysn, w0_ref[...], preferred_element_type=F32)
             + jnp.dot(an, w1_ref[...], preferred_element_type=F32))
        h = x_ref[rs, :] + mod_ref[0, 2:3, :] * o
        h_ref[rs, :] = h
        xn = _rms(h) * gf_ref[...]
        xf_ref[rs, :] = (xn * (1.0 + mod_ref[0, 4:5, :]) + mod_ref[0, 3:4, :]).astype(BF16)


def out_proj(y_tm, a2, x2, mod3, w_glu_bf, g_ssm, g_attn, w_out_bf, g_ffn, seqlen, tm=512, rows=256):
    t, d = x2.shape
    ngt = y_tm.shape[2]
    ws = ngt * LANE_TILE
    wa = a2.shape[1]
    assert seqlen % tm == 0 and tm % rows == 0 and rows % (CHUNK * BF16_ROWS) == 0 and ws == wa
    tpb = seqlen // tm
    const = lambda shape, r=0: pl.BlockSpec(shape, lambda i: (r, 0), pipeline_mode=pl.Buffered(1))
    return pl.pallas_call(
        functools.partial(_out_kernel, rows=rows),
        grid=(t // tm,),
        in_specs=[pl.BlockSpec((1, CHUNK, ngt, tm // CHUNK, LANE_TILE), lambda i: (i // tpb, 0, 0, i % tpb, 0)),
                  pl.BlockSpec((tm, wa), lambda i: (i, 0)),
                  pl.BlockSpec((tm, d), lambda i: (i, 0)),
                  pl.BlockSpec((1, 6, d), lambda i: (i // tpb, 0, 0)),
                  const((ws, ws)), const((1, ws)), const((1, wa)),
                  const((ws, d)), const((wa, d), 1), const((1, d))],
        out_specs=[pl.BlockSpec((tm, d), lambda i: (i, 0)),
                   pl.BlockSpec((tm, d), lambda i: (i, 0))],
        out_shape=[jax.ShapeDtypeStruct((t, d), F32), jax.ShapeDtypeStruct((t, d), BF16)],
        scratch_shapes=[pltpu.VMEM((ws // LANES, rows, LANES), F32)],
        compiler_params=_cparams(1),
        name="outproj",
    )(y_tm, a2, x2, mod3, w_glu_bf, g_ssm.reshape(1, ws), g_attn.reshape(1, wa), w_out_bf, w_out_bf,
      g_ffn.reshape(1, d))


def _ffn_kernel(xf_ref, h_ref, mod_ref, wg_ref, wu_ref, wd_ref, o_ref):
    j = pl.program_id(1)

    def partial_out():
        xf = xf_ref[...]
        g = jnp.dot(xf, wg_ref[...], preferred_element_type=F32)
        u = jnp.dot(xf, wu_ref[...], preferred_element_type=F32)
        hm = (jax.nn.silu(g) * u).astype(BF16)
        return jnp.dot(hm, wd_ref[...], preferred_element_type=F32)

    @pl.when(j == 0)
    def _():
        o_ref[...] = partial_out()

    last = pl.num_programs(1) - 1

    @pl.when((j > 0) & (j < last))
    def _():
        o_ref[...] += partial_out()

    @pl.when(j == last)
    def _():
        o_ref[...] = h_ref[...] + mod_ref[0, 5:6, :] * (o_ref[...] + partial_out())


def ffn(xf2, h2, mod3, w_gate, w_up, w_down, seqlen, tm=1024, tf=512):
    t, d = h2.shape
    dff = w_gate.shape[1]
    assert seqlen % tm == 0 and dff % tf == 0 and dff // tf >= 2
    return pl.pallas_call(
        _ffn_kernel,
        grid=(t // tm, dff // tf),
        in_specs=[pl.BlockSpec((tm, d), lambda i, j: (i, 0)),
                  pl.BlockSpec((tm, d), lambda i, j: (i, 0), pipeline_mode=pl.Buffered(1)),
                  pl.BlockSpec((1, 6, d), lambda i, j: ((i * tm) // seqlen, 0, 0)),
                  pl.BlockSpec((d, tf), lambda i, j: (0, j)),
                  pl.BlockSpec((d, tf), lambda i, j: (0, j)),
                  pl.BlockSpec((tf, d), lambda i, j: (j, 0))],
        out_specs=pl.BlockSpec((tm, d), lambda i, j: (i, 0)),
        out_shape=jax.ShapeDtypeStruct((t, d), F32),
        compiler_params=_cparams(2),
        name="ffn",
    )(xf2, h2, mod3, w_gate, w_up, w_down)


def s5_mixer_chunked(u_tm, a_re, a_im, log_dt, b_re, b_im, c_re, c_im, d_skip):
    bsz, _, ngt, nc, _ = u_tm.shape
    g, p = a_re.shape
    h = c_re.shape[1]
    assert h * CHUNK == LANE_TILE and g * h == ngt * LANE_TILE
    bl, cl, kx, a = s5_prep(a_re, a_im, log_dt, b_re, b_im, c_re, c_im, d_skip)
    s = s5_chunk_inputs(u_tm, bl.reshape(CHUNK, 2, g * h, p), h)
    return s5_outputs(u_tm, kx.reshape(g * h, CHUNK * h), cl.reshape(CHUNK, 2, g * h, p), s,
                      a.reshape(2, 1, g * p))


def kernel(x, c, w_ada, b_ada, g_mix, w_in, a_re, a_im, log_dt, b_re, b_im, c_re, c_im, d_skip,
           w_glu, q_gain, k_gain, g_ssm_out, g_attn_out, w_out, g_ffn, w_gate, w_up, w_down):
    bsz, seqlen, d = x.shape
    depth = w_ada.shape[0]
    width = w_glu.shape[1]
    h2 = x.reshape(bsz * seqlen, d)
    for l in range(depth):
        mod3 = ada_mod(c, w_ada[l], b_ada[l]).reshape(bsz, 6, d)
        u_tm, qkv = in_proj(h2, mod3, g_mix[l], w_in[l].astype(BF16), q_gain[l], k_gain[l], bsz, seqlen)
        y_tm = s5_mixer_chunked(u_tm, a_re[l], a_im[l], log_dt[l], b_re[l], b_im[l],
                                c_re[l], c_im[l], d_skip[l])
        att, (wglu_bf, wout_bf, wg_bf, wu_bf, wd_bf) = stick_attention(
            qkv.reshape(bsz, seqlen, 3 * width), width,
            cast=(w_glu[l], w_out[l], w_gate[l], w_up[l], w_down[l]))
        h2, xf = out_proj(y_tm, att.reshape(bsz * seqlen, width), h2, mod3,
                          wglu_bf, g_ssm_out[l], g_attn_out[l], wout_bf, g_ffn[l], seqlen)
        h2 = ffn(xf, h2, mod3, wg_bf, wu_bf, wd_bf, seqlen)
    return h2.reshape(bsz, seqlen, d)
```
